```python
import math
import jax, jax.numpy as jnp
from jax import lax
import numpy as np

D_MODEL = 1024
BATCH = 16
SEQ = 4096
DEPTH = 4

CHUNK = 64
HEAD_DIM = 64
GROUP_WIDTH = D_MODEL // 4
N_HEADS = GROUP_WIDTH // HEAD_DIM
ROPE_THETA = 10000.0
NEG_INF = -1e30

A_PREV_CHUNKS = 8
REL_CLIP = 128
MLA_NOPE_DIM = HEAD_DIM
MLA_ROPE_DIM = HEAD_DIM // 2
MLA_V_DIM = HEAD_DIM
MLA_Q_RANK = 192
MLA_KV_RANK = 128
Q_BLOCK = 128
SWA_WINDOW = 128
SWA_PREV_CHUNKS = SWA_WINDOW // CHUNK
SWA_KV_HEADS = 2
MEM_LEN = 256
MEM_HEADS = 4

SPLIT_WIDTHS = (
    GROUP_WIDTH, GROUP_WIDTH, GROUP_WIDTH, GROUP_WIDTH,
    MLA_Q_RANK, MLA_KV_RANK, MLA_ROPE_DIM, GROUP_WIDTH,
    GROUP_WIDTH, SWA_KV_HEADS * HEAD_DIM, SWA_KV_HEADS * HEAD_DIM, GROUP_WIDTH,
    GROUP_WIDTH, GROUP_WIDTH,
)
PROJ_WIDTH = sum(SPLIT_WIDTHS)

kernel_name = 'hybrid_chunk_causal_head_groups'


def _split_points():
    pts, acc = [], 0
    for w in SPLIT_WIDTHS[:-1]:
        acc += w
        pts.append(acc)
    return pts


def _layer_norm(x, g, b, eps=1e-5):
    xf = x.astype(jnp.float32)
    mu = jnp.mean(xf, axis=-1, keepdims=True)
    var = jnp.mean(jnp.square(xf - mu), axis=-1, keepdims=True)
    y = (xf - mu) * lax.rsqrt(var + eps) * g.astype(jnp.float32) + b.astype(jnp.float32)
    return y.astype(x.dtype)


def _rms_norm(x, g, eps=1e-6):
    xf = x.astype(jnp.float32)
    y = xf * lax.rsqrt(jnp.mean(jnp.square(xf), axis=-1, keepdims=True) + eps) * g.astype(jnp.float32)
    return y.astype(x.dtype)


def _rope(x, positions):
    d = x.shape[-1]
    inv_freq = ROPE_THETA ** (-jnp.arange(0, d, 2, dtype=jnp.float32) / d)
    ang = positions.astype(jnp.float32)[..., None] * inv_freq
    cos = jnp.cos(ang)[:, :, None, :]
    sin = jnp.sin(ang)[:, :, None, :]
    x1, x2 = jnp.split(x.astype(jnp.float32), 2, axis=-1)
    out = jnp.concatenate([x1 * cos - x2 * sin, x2 * cos + x1 * sin], axis=-1)
    return out.astype(x.dtype)


def _banded_chunk_attention(q, k, v, n_prev, rel_bias=None, sinks=None):
    b, s, h, dh = q.shape
    kvh = k.shape[2]
    g = h // kvh
    nc = s // CHUNK
    pad = n_prev * CHUNK
    band = pad + CHUNK
    kp = jnp.pad(k, ((0, 0), (pad, 0), (0, 0), (0, 0)))
    vp = jnp.pad(v, ((0, 0), (pad, 0), (0, 0), (0, 0)))
    qc = q.reshape(b, nc, CHUNK, kvh, g, dh).transpose(1, 0, 2, 3, 4, 5)
    scale = dh ** -0.5
    kk = jnp.arange(band)
    if rel_bias is not None:
        rel = jnp.arange(CHUNK)[:, None] + pad - kk[None, :]
        idx = jnp.clip(rel, -REL_CLIP, REL_CLIP) + REL_CLIP
        bias = rel_bias[:, idx].astype(jnp.float32).reshape(kvh, g, CHUNK, band)

    def one_chunk(args):
        qb, c = args
        kb = lax.dynamic_slice_in_dim(kp, c * CHUNK, band, axis=1)
        vb = lax.dynamic_slice_in_dim(vp, c * CHUNK, band, axis=1)
        sc = jnp.einsum('bqkgd,bskd->bkgqs', qb, kb).astype(jnp.float32) * scale
        if rel_bias is not None:
            sc = sc + bias
        valid = (c * CHUNK - pad + kk) >= 0
        sc = jnp.where(valid, sc, NEG_INF)
        if sinks is not None:
            sink = jnp.broadcast_to(sinks.astype(jnp.float32).reshape(kvh, g, 1, 1), sc.shape[:-1] + (1,))
            p = jax.nn.softmax(jnp.concatenate([sc, sink], axis=-1), axis=-1)[..., :band]
        else:
            p = jax.nn.softmax(sc, axis=-1)
        o = jnp.einsum('bkgqs,bskd->bqkgd', p.astype(v.dtype), vb)
        return o.reshape(b, CHUNK, h, dh)

    out = lax.map(one_chunk, (qc, jnp.arange(nc)))
    return out.transpose(1, 0, 2, 3, 4).reshape(b, s, h * dh)


def _mla(c_q, c_kv, k_rope, positions, q_norm, w_uq, kv_norm, w_ukv):
    b, s, _ = c_q.shape
    q = (_rms_norm(c_q, q_norm) @ w_uq).reshape(b, s, N_HEADS, MLA_NOPE_DIM + MLA_ROPE_DIM)
    q_nope, q_rope = q[..., :MLA_NOPE_DIM], q[..., MLA_NOPE_DIM:]
    q_rope = _rope(q_rope, positions)
    kv = (_rms_norm(c_kv, kv_norm) @ w_ukv).reshape(b, s, N_HEADS, MLA_NOPE_DIM + MLA_V_DIM)
    k_nope, v = kv[..., :MLA_NOPE_DIM], kv[..., MLA_NOPE_DIM:]
    k_r = _rope(k_rope[:, :, None, :], positions)[:, :, 0, :]
    scale = (MLA_NOPE_DIM + MLA_ROPE_DIM) ** -0.5
    nb = s // Q_BLOCK
    qn = q_nope.reshape(b, nb, Q_BLOCK, N_HEADS, MLA_NOPE_DIM).transpose(1, 0, 2, 3, 4)
    qr = q_rope.reshape(b, nb, Q_BLOCK, N_HEADS, MLA_ROPE_DIM).transpose(1, 0, 2, 3, 4)
    key_chunk = jnp.arange(s) // CHUNK

    def one_block(args):
        qn_b, qr_b, i = args
        sc = (jnp.einsum('bqhd,bshd->bhqs', qn_b, k_nope)
              + jnp.einsum('bqhd,bsd->bhqs', qr_b, k_r)).astype(jnp.float32) * scale
        q_chunk = (i * Q_BLOCK + jnp.arange(Q_BLOCK)) // CHUNK
        mask = key_chunk[None, :] <= q_chunk[:, None]
        p = jax.nn.softmax(jnp.where(mask, sc, NEG_INF), axis=-1)
        return jnp.einsum('bhqs,bshd->bqhd', p.astype(v.dtype), v)

    out = lax.map(one_block, (qn, qr, jnp.arange(nb)))
    return out.transpose(1, 0, 2, 3, 4).reshape(b, s, N_HEADS * MLA_V_DIM)


def _memory_attention(q, mem, w_mem_kv):
    b, s, _ = q.shape
    qh = q.reshape(b, s, MEM_HEADS, HEAD_DIM)
    kv = (mem @ w_mem_kv).reshape(b, MEM_LEN, 2, MEM_HEADS, HEAD_DIM)
    k, v = kv[:, :, 0], kv[:, :, 1]
    sc = jnp.einsum('bqhd,bmhd->bhqm', qh, k).astype(jnp.float32) * (HEAD_DIM ** -0.5)
    p = jax.nn.softmax(sc, axis=-1)
    return jnp.einsum('bhqm,bmhd->bqhd', p.astype(v.dtype), v).reshape(b, s, MEM_HEADS * HEAD_DIM)


def _layer(x, mem, positions, w_in, rel_bias, mla_q_norm, w_uq, mla_kv_norm, w_ukv,
           swa_sinks, w_mem_kv, w_out, ln_gain, ln_bias):
    b, s, _ = x.shape
    proj = x @ w_in
    (aq, ak, av, ag, bcq, bckv, bkr, bg, cq, ck, cv, cg, mq, mg) = jnp.split(proj, _split_points(), axis=-1)

    ya = _banded_chunk_attention(aq.reshape(b, s, N_HEADS, HEAD_DIM),
                                 ak.reshape(b, s, N_HEADS, HEAD_DIM),
                                 av.reshape(b, s, N_HEADS, HEAD_DIM),
                                 A_PREV_CHUNKS, rel_bias=rel_bias)
    yb = _mla(bcq, bckv, bkr, positions, mla_q_norm, w_uq, mla_kv_norm, w_ukv)
    qc = _rope(cq.reshape(b, s, N_HEADS, HEAD_DIM), positions)
    kc = _rope(ck.reshape(b, s, SWA_KV_HEADS, HEAD_DIM), positions)
    yc = _banded_chunk_attention(qc, kc, cv.reshape(b, s, SWA_KV_HEADS, HEAD_DIM),
                                 SWA_PREV_CHUNKS, sinks=swa_sinks)
    ym = _memory_attention(mq, mem, w_mem_kv)

    gated = jnp.concatenate([ya * jax.nn.silu(ag), yb * jax.nn.silu(bg),
                             yc * jax.nn.silu(cg), ym * jax.nn.silu(mg)], axis=-1)
    y = gated @ w_out
    alpha = (2.0 * DEPTH) ** 0.25
    return _layer_norm(alpha * x + y, ln_gain, ln_bias)


def setup_inputs(seed: int = 0) -> dict:
    key = jax.random.key(seed)
    ks = jax.random.split(key, 16)
    f32 = jnp.float32
    beta = (8.0 * DEPTH) ** -0.25
    x = jax.random.normal(ks[0], (BATCH, SEQ, D_MODEL), f32)
    mem = jax.random.normal(ks[1], (BATCH, MEM_LEN, D_MODEL), f32)
    offset = jax.random.randint(ks[2], (BATCH, 1), 0, 4096, dtype=jnp.int32)
    positions = offset + jnp.arange(SEQ, dtype=jnp.int32)[None, :]
    w_in = jax.random.normal(ks[3], (DEPTH, D_MODEL, PROJ_WIDTH), f32) * D_MODEL ** -0.5
    rel_bias = jax.random.normal(ks[4], (DEPTH, N_HEADS, 2 * REL_CLIP + 1), f32) * 0.5
    mla_q_norm = 1.0 + 0.02 * jax.random.normal(ks[5], (DEPTH, MLA_Q_RANK), f32)
    w_uq = jax.random.normal(ks[6], (DEPTH, MLA_Q_RANK, N_HEADS * (MLA_NOPE_DIM + MLA_ROPE_DIM)), f32) * MLA_Q_RANK ** -0.5
    mla_kv_norm = 1.0 + 0.02 * jax.random.normal(ks[7], (DEPTH, MLA_KV_RANK), f32)
    w_ukv = jax.random.normal(ks[8], (DEPTH, MLA_KV_RANK, N_HEADS * (MLA_NOPE_DIM + MLA_V_DIM)), f32) * MLA_KV_RANK ** -0.5
    swa_sinks = jax.random.normal(ks[9], (DEPTH, N_HEADS), f32) * 0.5
    w_mem_kv = jax.random.normal(ks[10], (DEPTH, D_MODEL, 2 * MEM_HEADS * HEAD_DIM), f32) * D_MODEL ** -0.5
    w_out = jax.random.normal(ks[11], (DEPTH, D_MODEL, D_MODEL), f32) * (D_MODEL ** -0.5) * beta
    ln_gain = 1.0 + 0.02 * jax.random.normal(ks[12], (DEPTH, D_MODEL), f32)
    ln_bias = 0.02 * jax.random.normal(ks[13], (DEPTH, D_MODEL), f32)
    return {'x': x, 'mem': mem, 'positions': positions, 'w_in': w_in, 'rel_bias': rel_bias,
            'mla_q_norm': mla_q_norm, 'w_uq': w_uq, 'mla_kv_norm': mla_kv_norm, 'w_ukv': w_ukv,
            'swa_sinks': swa_sinks, 'w_mem_kv': w_mem_kv, 'w_out': w_out,
            'ln_gain': ln_gain, 'ln_bias': ln_bias}


def reference(x, mem, positions, w_in, rel_bias, mla_q_norm, w_uq, mla_kv_norm, w_ukv,
              swa_sinks, w_mem_kv, w_out, ln_gain, ln_bias):
    h = x
    for l in range(DEPTH):
        h = _layer(h, mem, positions, w_in[l], rel_bias[l], mla_q_norm[l], w_uq[l],
                   mla_kv_norm[l], w_ukv[l], swa_sinks[l], w_mem_kv[l], w_out[l],
                   ln_gain[l], ln_bias[l])
    return h
```

```python
import functools

import numpy as np
import jax
import jax.numpy as jnp
from jax import lax
from jax.experimental import pallas as pl
from jax.experimental.pallas import tpu as pltpu

F32 = jnp.float32
BF16 = jnp.bfloat16

D_MODEL = 1024
DEPTH = 4
CHUNK = 64
HEAD_DIM = 64
GROUP = 256
N_HEADS = 4
ROPE_THETA = 10000.0
NEG_INF = -1e30
A_PREV = 8
REL_CLIP = 128
MLA_NOPE = 64
MLA_ROPE = 32
MLA_Q_RANK = 192
MLA_KV_RANK = 128
SWA_PREV = 2
MEM_LEN = 256
ALPHA = (2.0 * DEPTH) ** 0.25

TQ = 128
A_WIN = TQ + A_PREV * CHUNK
C_WIN = TQ + SWA_PREV * CHUNK
B_TK = 512
TM = 512
VMEM_LIMIT = 56 * 1024 * 1024

P_AQ, P_AK, P_AV = 0, 256, 512
P_BQ0, P_BQ1, P_BK0, P_BK1, P_BV = 768, 1024, 1280, 1536, 1792
P_CQ, P_CK, P_CV = 2048, 2304, 2432
P_MQ = 2560
P_GA, P_GB, P_GC, P_GM = 2816, 3072, 3328, 3584
P_WIDTH = 3840

W_A, W_C, W_M, W_G, W_B, W_WIDTH = 0, 768, 1280, 1536, 2560, 3072

C_HEAD_ORDER = (0, 2, 1, 3)


def _inproj_cols():
    r = np.arange
    aq, ak, av, ag = 0, 256, 512, 768
    bcq, bckv, bkr, bg = 1024, 1216, 1344, 1376
    cq, ck, cv, cg = 1632, 1888, 2016, 2144
    mq, mg = 2400, 2656
    cperm = np.concatenate([r(64) + 64 * h for h in C_HEAD_ORDER])
    pad = lambda n: np.full(n, -1)
    cols = np.concatenate([
        aq + r(256), ak + r(256), av + r(256),
        cq + cperm, ck + r(128), cv + r(128),
        mq + r(256),
        ag + r(256), bg + r(256), cg + cperm, mg + r(256),
        bcq + r(192), pad(64), bckv + r(128), bkr + r(32), bkr + r(32), pad(64),
    ])
    assert cols.shape[0] == W_WIDTH
    return cols, cperm


def _take_cols(w, cols):
    n = w.shape[-1]
    wz = jnp.concatenate([w, jnp.zeros(w.shape[:-1] + (1,), w.dtype)], axis=-1)
    return jnp.take(wz, jnp.asarray(np.where(cols < 0, n, cols)), axis=-1)


def _uq_cols():
    r = np.arange
    per = MLA_NOPE + MLA_ROPE
    out = []
    for p in range(2):
        h0, h1 = 2 * p, 2 * p + 1
        out += [per * h0 + r(64), per * h1 + r(64),
                per * h0 + 64 + r(32), per * h1 + 64 + r(32), np.full(64, -1)]
    return np.concatenate(out)


def _ukv_cols():
    r = np.arange
    return np.concatenate([128 * h + r(64) for h in range(4)] + [128 * h + 64 + r(64) for h in range(4)])


def _rope_tables(positions):
    pos = positions.astype(F32).reshape(-1, 1)
    lane = np.arange(128)

    def tab(d):
        half = d // 2
        k = (lane % d) % half
        inv = ROPE_THETA ** (-jnp.asarray(2 * k, F32) / d)
        ang = pos * inv[None, :]
        sign = jnp.asarray(np.where((lane % d) < half, -1.0, 1.0), F32)
        return jnp.cos(ang), jnp.sin(ang) * sign[None, :]

    c64, s64 = tab(HEAD_DIM)
    c32, s32 = tab(MLA_ROPE)
    return jnp.concatenate([c64, s64, c32, s32], axis=-1)


def _bias_table_a(rel_bias):
    mb = np.arange(9)[:, None, None]
    i = np.arange(TQ)[None, :, None]
    c = np.arange(128)[None, None, :]
    m = mb * 128 + c
    rel = i + A_PREV * CHUNK - m
    dchunk = i // CHUNK + A_PREV - m // CHUNK
    valid = (dchunk >= 0) & (dchunk <= A_PREV)
    idx = np.clip(rel, -REL_CLIP, REL_CLIP) + REL_CLIP
    idx = np.broadcast_to(idx, (9, TQ, 128))
    valid = np.broadcast_to(valid, (9, TQ, 128))
    g = rel_bias[:, :, idx]
    g = jnp.where(jnp.asarray(valid)[None, None], g, NEG_INF)
    return jnp.transpose(g, (0, 2, 1, 3, 4)).reshape(DEPTH, 9, N_HEADS * TQ, 128)


def _mask_table_c():
    mb = np.arange(3)[:, None, None]
    i = np.arange(TQ)[None, :, None]
    c = np.arange(128)[None, None, :]
    m = mb * 128 + c
    dchunk = i // CHUNK + SWA_PREV - m // CHUNK
    valid = (dchunk >= 0) & (dchunk <= SWA_PREV)
    t = np.where(valid, 0.0, NEG_INF).astype(np.float32)
    return np.tile(t, (1, N_HEADS, 1))


def _mask_table_b():
    u = np.arange(4)[:, None, None]
    i = np.arange(TQ)[None, :, None]
    c = np.arange(B_TK)[None, None, :]
    valid = (c // CHUNK) <= 2 * u + i // CHUNK
    return np.where(valid, 0.0, NEG_INF).astype(np.float32)


def _dot(a, b):
    return jnp.dot(a, b, preferred_element_type=F32)


def _dot_nt(a, b):
    return lax.dot_general(a, b, (((1,), (1,)), ((), ())), preferred_element_type=F32)


def _rope(x, cos, sin_signed, half):
    lane = lax.broadcasted_iota(jnp.int32, x.shape, 1)
    first = (lane & (2 * half - 1)) < half
    swapped = jnp.where(first, pltpu.roll(x, 128 - half, 1), pltpu.roll(x, half, 1))
    return x * cos + swapped * sin_signed


def _stack_masked(q, lane_masks):
    qf = q.astype(F32)
    return jnp.concatenate([jnp.where(mk, qf, 0.0) for mk in lane_masks], axis=0).astype(BF16)


def _head_masks(rows, width):
    lane = lax.broadcasted_iota(jnp.int32, (rows, width), 1)
    return [(lane >= HEAD_DIM * h) & (lane < HEAD_DIM * (h + 1)) for h in range(width // HEAD_DIM)]


def _unstack(o, lane_masks):
    rows = o.shape[0] // len(lane_masks)
    out = o[(len(lane_masks) - 1) * rows:]
    for h in range(len(lane_masks) - 2, -1, -1):
        out = jnp.where(lane_masks[h], o[h * rows:(h + 1) * rows], out)
    return out


def _softmax_pv(s, v, sink=None):
    m = jnp.max(s, axis=-1, keepdims=True)
    if sink is not None:
        m = jnp.maximum(m, sink)
    p = jnp.exp(s - m)
    l = jnp.sum(p, axis=-1, keepdims=True)
    if sink is not None:
        l = l + jnp.exp(sink - m)
    o = _dot(p.astype(BF16), v)
    return o * (1.0 / l)


def _inproj_kernel(x_ref, tab_ref, w_ref, wuq_ref, wukv_ref, gq_ref, gkv_ref, p_ref):
    xb = x_ref[...].astype(BF16)
    cos64, sin64 = tab_ref[:, 0:128], tab_ref[:, 128:256]
    cos32, sin32 = tab_ref[:, 256:384], tab_ref[:, 384:512]

    def mm(lo, hi):
        return _dot(xb, w_ref[:, lo:hi])

    r = mm(W_A, W_A + 768)
    p_ref[:, P_AQ:P_AQ + 256] = (r[:, 0:256] * 0.125).astype(BF16)
    p_ref[:, P_AK:P_AK + 512] = r[:, 256:768].astype(BF16)

    r = mm(W_C, W_C + 512)
    for j in range(2):
        qj = _rope(r[:, 128 * j:128 * (j + 1)], cos64, sin64, 32)
        p_ref[:, P_CQ + 128 * j:P_CQ + 128 * (j + 1)] = (qj * 0.125).astype(BF16)
    p_ref[:, P_CK:P_CK + 128] = _rope(r[:, 256:384], cos64, sin64, 32).astype(BF16)
    p_ref[:, P_CV:P_CV + 128] = r[:, 384:512].astype(BF16)

    r = mm(W_M, W_M + 256)
    p_ref[:, P_MQ:P_MQ + 256] = (r * 0.125).astype(BF16)

    r = mm(W_G, W_G + 1024)
    p_ref[:, P_GA:P_GA + 1024] = (r * (1.0 / (1.0 + jnp.exp(-r)))).astype(BF16)

    r = mm(W_B, W_B + 512)
    cq = r[:, 0:256]
    ms = jnp.sum(cq * cq, axis=-1, keepdims=True) * (1.0 / MLA_Q_RANK)
    qn = (cq * lax.rsqrt(ms + 1e-6) * gq_ref[...]).astype(BF16)
    q = _dot(qn, wuq_ref[...]) * ((MLA_NOPE + MLA_ROPE) ** -0.5)
    for p in range(2):
        base = P_BQ0 + 256 * p
        p_ref[:, base:base + 128] = q[:, 256 * p:256 * p + 128].astype(BF16)
        p_ref[:, base + 128:base + 256] = _rope(q[:, 256 * p + 128:256 * p + 256], cos32, sin32, 16).astype(BF16)

    ckv = r[:, 256:384]
    ms = jnp.mean(ckv * ckv, axis=-1, keepdims=True)
    kvn = (ckv * lax.rsqrt(ms + 1e-6) * gkv_ref[...]).astype(BF16)
    kv = _dot(kvn, wukv_ref[...])
    krb = _rope(r[:, 384:512], cos32, sin32, 16).astype(BF16)
    for p in range(2):
        base = P_BK0 + 256 * p
        p_ref[:, base:base + 128] = kv[:, 128 * p:128 * (p + 1)].astype(BF16)
        p_ref[:, base + 128:base + 256] = krb
    p_ref[:, P_BV:P_BV + 256] = kv[:, 256:512].astype(BF16)


def _inproj(x2d, tab, w, wuq, wukv, gq, gkv):
    n = x2d.shape[0]
    const = lambda shape: pl.BlockSpec(shape, lambda i: (0,) * len(shape))
    return pl.pallas_call(
        _inproj_kernel,
        grid=(n // TM,),
        in_specs=[
            pl.BlockSpec((TM, D_MODEL), lambda i: (i, 0)),
            pl.BlockSpec((TM, 512), lambda i: (i, 0)),
            const((D_MODEL, W_WIDTH)),
            const((256, 512)),
            const((128, 512)),
            const((1, 256)),
            const((1, 128)),
        ],
        out_specs=pl.BlockSpec((TM, P_WIDTH), lambda i: (i, 0)),
        out_shape=jax.ShapeDtypeStruct((n, P_WIDTH), BF16),
        compiler_params=pltpu.CompilerParams(
            dimension_semantics=("parallel",), vmem_limit_bytes=VMEM_LIMIT),
        name="inproj",
    )(x2d, tab, w, wuq, wukv, gq, gkv)


def _memkv_kernel(mem_ref, w_ref, o_ref):
    o_ref[0] = _dot(mem_ref[0].astype(BF16), w_ref[...]).astype(BF16)


def _memkv(mem, w_all):
    b = mem.shape[0]
    n = w_all.shape[1]
    return pl.pallas_call(
        _memkv_kernel,
        grid=(b,),
        in_specs=[pl.BlockSpec((1, MEM_LEN, D_MODEL), lambda i: (i, 0, 0)),
                  pl.BlockSpec((D_MODEL, n), lambda i: (0, 0))],
        out_specs=pl.BlockSpec((1, MEM_LEN, n), lambda i: (i, 0, 0)),
        out_shape=jax.ShapeDtypeStruct((b, MEM_LEN, n), BF16),
        compiler_params=pltpu.CompilerParams(
            dimension_semantics=("parallel",), vmem_limit_bytes=VMEM_LIMIT),
        name="memkv",
    )(mem, w_all)


def _attn_a_kernel(q_ref, k_ref, v_ref, g_ref, e_ref, o_ref):
    t = pl.program_id(1)
    start = pl.multiple_of(jnp.maximum(t * TQ - A_PREV * CHUNK, 0), 128)
    mb0 = jnp.maximum(A_PREV * CHUNK // 128 - t, 0)
    masks = _head_masks(TQ, GROUP)
    qs = _stack_masked(q_ref[0], masks)
    kwin = k_ref[0, pl.ds(start, A_WIN), :]
    vwin = v_ref[0, pl.ds(start, A_WIN), :]
    s = _dot_nt(qs, kwin)
    s = s + jnp.concatenate([e_ref[mb0 + j] for j in range(A_WIN // 128)], axis=1)
    o = _unstack(_softmax_pv(s, vwin), masks)
    o_ref[0] = (o * g_ref[0].astype(F32)).astype(BF16)


def _attn_a(p3, e):
    b, s, _ = p3.shape
    return pl.pallas_call(
        _attn_a_kernel,
        grid=(b, s // TQ),
        in_specs=[
            pl.BlockSpec((1, TQ, 256), lambda i, t: (i, t, P_AQ // 256)),
            pl.BlockSpec((1, s, 256), lambda i, t: (i, 0, P_AK // 256)),
            pl.BlockSpec((1, s, 256), lambda i, t: (i, 0, P_AV // 256)),
            pl.BlockSpec((1, TQ, 256), lambda i, t: (i, t, P_GA // 256)),
            pl.BlockSpec((9, N_HEADS * TQ, 128), lambda i, t: (0, 0, 0)),
        ],
        out_specs=pl.BlockSpec((1, TQ, 256), lambda i, t: (i, t, 0)),
        out_shape=jax.ShapeDtypeStruct((b, s, GROUP), BF16),
        compiler_params=pltpu.CompilerParams(
            dimension_semantics=("parallel", "arbitrary"), vmem_limit_bytes=VMEM_LIMIT),
        name="attn_a",
    )(p3, p3, p3, p3, e)


def _attn_c_kernel(sink_ref, q_ref, k_ref, v_ref, g_ref, e_ref, o_ref):
    t = pl.program_id(1)
    start = pl.multiple_of(jnp.maximum(t * TQ - SWA_PREV * CHUNK, 0), 128)
    mb0 = jnp.maximum(1 - t, 0)
    lane = lax.broadcasted_iota(jnp.int32, (TQ, 128), 1)
    lo, hi = lane < HEAD_DIM, lane >= HEAD_DIM
    q = q_ref[0].astype(F32)
    qs = jnp.concatenate([jnp.where(lo, q[:, 0:128], 0.0), jnp.where(hi, q[:, 0:128], 0.0),
                          jnp.where(lo, q[:, 128:256], 0.0), jnp.where(hi, q[:, 128:256], 0.0)],
                         axis=0).astype(BF16)
    kwin = k_ref[0, pl.ds(start, C_WIN), :]
    vwin = v_ref[0, pl.ds(start, C_WIN), :]
    s = _dot_nt(qs, kwin)
    s = s + jnp.concatenate([e_ref[mb0 + j] for j in range(C_WIN // 128)], axis=1)
    row = lax.broadcasted_iota(jnp.int32, (N_HEADS * TQ, 1), 0)
    sink = jnp.where(row < TQ, sink_ref[C_HEAD_ORDER[0]],
                     jnp.where(row < 2 * TQ, sink_ref[C_HEAD_ORDER[1]],
                               jnp.where(row < 3 * TQ, sink_ref[C_HEAD_ORDER[2]], sink_ref[C_HEAD_ORDER[3]])))
    o = _softmax_pv(s, vwin, sink=sink)
    out = jnp.concatenate([jnp.where(lo, o[0:TQ], o[TQ:2 * TQ]),
                           jnp.where(lo, o[2 * TQ:3 * TQ], o[3 * TQ:4 * TQ])], axis=1)
    o_ref[0] = (out * g_ref[0].astype(F32)).astype(BF16)


def _attn_c(p3, sinks, e):
    b, s, _ = p3.shape
    return pl.pallas_call(
        _attn_c_kernel,
        grid=(b, s // TQ),
        in_specs=[
            pl.BlockSpec(memory_space=pltpu.SMEM),
            pl.BlockSpec((1, TQ, 256), lambda i, t: (i, t, P_CQ // 256)),
            pl.BlockSpec((1, s, 128), lambda i, t: (i, 0, P_CK // 128)),
            pl.BlockSpec((1, s, 128), lambda i, t: (i, 0, P_CV // 128)),
            pl.BlockSpec((1, TQ, 256), lambda i, t: (i, t, P_GC // 256)),
            pl.BlockSpec((3, N_HEADS * TQ, 128), lambda i, t: (0, 0, 0)),
        ],
        out_specs=pl.BlockSpec((1, TQ, 256), lambda i, t: (i, t, 0)),
        out_shape=jax.ShapeDtypeStruct((b, s, GROUP), BF16),
        compiler_params=pltpu.CompilerParams(
            dimension_semantics=("parallel", "arbitrary"), vmem_limit_bytes=VMEM_LIMIT),
        name="attn_c",
    )(sinks, p3, p3, p3, p3, e)


def _attn_m_kernel(q_ref, k_ref, v_ref, g_ref, o_ref):
    masks = _head_masks(TQ, GROUP)
    qs = _stack_masked(q_ref[0], masks)
    s = _dot_nt(qs, k_ref[0])
    o = _unstack(_softmax_pv(s, v_ref[0]), masks)
    o_ref[0] = (o * g_ref[0].astype(F32)).astype(BF16)


def _attn_m(p3, memkv, layer):
    b, s, _ = p3.shape
    return pl.pallas_call(
        _attn_m_kernel,
        grid=(b, s // TQ),
        in_specs=[
            pl.BlockSpec((1, TQ, 256), lambda i, t: (i, t, P_MQ // 256)),
            pl.BlockSpec((1, MEM_LEN, 256), lambda i, t: (i, 0, 2 * layer)),
            pl.BlockSpec((1, MEM_LEN, 256), lambda i, t: (i, 0, 2 * layer + 1)),
            pl.BlockSpec((1, TQ, 256), lambda i, t: (i, t, P_GM // 256)),
        ],
        out_specs=pl.BlockSpec((1, TQ, 256), lambda i, t: (i, t, 0)),
        out_shape=jax.ShapeDtypeStruct((b, s, GROUP), BF16),
        compiler_params=pltpu.CompilerParams(
            dimension_semantics=("parallel", "arbitrary"), vmem_limit_bytes=VMEM_LIMIT),
        name="attn_m",
    )(p3, memkv, memkv, p3)


def _attn_b_kernel(q0_ref, q1_ref, k0_ref, k1_ref, v_ref, g_ref, mask_ref, o_ref, m_sc, l_sc, acc_sc):
    t = pl.program_id(1)
    nfull = t >> 2
    lane = lax.broadcasted_iota(jnp.int32, (TQ, GROUP), 1)
    slot = [(lane < 64) | ((lane >= 128) & (lane < 160)),
            ((lane >= 64) & (lane < 128)) | ((lane >= 160) & (lane < 192))]
    qs0 = _stack_masked(q0_ref[0], slot)
    qs1 = _stack_masked(q1_ref[0], slot)

    m_sc[...] = jnp.full(m_sc.shape, NEG_INF, F32)
    l_sc[...] = jnp.zeros(l_sc.shape, F32)
    acc_sc[...] = jnp.zeros(acc_sc.shape, F32)

    def block(start, mask):
        k0 = k0_ref[0, pl.ds(start, B_TK), :]
        k1 = k1_ref[0, pl.ds(start, B_TK), :]
        v = v_ref[0, pl.ds(start, B_TK), :]
        s = jnp.concatenate([_dot_nt(qs0, k0), _dot_nt(qs1, k1)], axis=0)
        if mask is not None:
            s = s + mask
        m_prev = m_sc[...]
        m_new = jnp.maximum(m_prev, jnp.max(s, axis=-1, keepdims=True))
        alpha = jnp.exp(m_prev - m_new)
        p = jnp.exp(s - m_new)
        l_sc[...] = alpha * l_sc[...] + jnp.sum(p, axis=-1, keepdims=True)
        acc_sc[...] = alpha * acc_sc[...] + _dot(p.astype(BF16), v)
        m_sc[...] = m_new

    def body(kb, carry):
        block(pl.multiple_of(kb * B_TK, B_TK), None)
        return carry

    lax.fori_loop(0, nfull, body, 0)
    mk = mask_ref[t & 3]
    block(pl.multiple_of(nfull * B_TK, B_TK), jnp.concatenate([mk] * N_HEADS, axis=0))

    o = _unstack(acc_sc[...] * (1.0 / l_sc[...]), _head_masks(TQ, GROUP))
    o_ref[0] = (o * g_ref[0].astype(F32)).astype(BF16)


def _attn_b(p3, mask):
    b, s, _ = p3.shape
    return pl.pallas_call(
        _attn_b_kernel,
        grid=(b, s // TQ),
        in_specs=[
            pl.BlockSpec((1, TQ, 256), lambda i, t: (i, t, P_BQ0 // 256)),
            pl.BlockSpec((1, TQ, 256), lambda i, t: (i, t, P_BQ1 // 256)),
            pl.BlockSpec((1, s, 256), lambda i, t: (i, 0, P_BK0 // 256)),
            pl.BlockSpec((1, s, 256), lambda i, t: (i, 0, P_BK1 // 256)),
            pl.BlockSpec((1, s, 256), lambda i, t: (i, 0, P_BV // 256)),
            pl.BlockSpec((1, TQ, 256), lambda i, t: (i, t, P_GB // 256)),
            pl.BlockSpec((4, TQ, B_TK), lambda i, t: (0, 0, 0)),
        ],
        out_specs=pl.BlockSpec((1, TQ, 256), lambda i, t: (i, t, 0)),
        out_shape=jax.ShapeDtypeStruct((b, s, GROUP), BF16),
        scratch_shapes=[pltpu.VMEM((N_HEADS * TQ, 1), F32),
                        pltpu.VMEM((N_HEADS * TQ, 1), F32),
                        pltpu.VMEM((N_HEADS * TQ, GROUP), F32)],
        compiler_params=pltpu.CompilerParams(
            dimension_semantics=("parallel", "arbitrary"), vmem_limit_bytes=VMEM_LIMIT),
        name="attn_b",
    )(p3, p3, p3, p3, p3, p3, mask)


def _outproj_kernel(ya_ref, yb_ref, yc_ref, ym_ref, x_ref, w_ref, g_ref, b_ref, o_ref):
    y = (_dot(ya_ref[...], w_ref[0:256, :]) + _dot(yb_ref[...], w_ref[256:512, :])
         + _dot(yc_ref[...], w_ref[512:768, :]) + _dot(ym_ref[...], w_ref[768:1024, :]))
    z = ALPHA * x_ref[...] + y
    mu = jnp.mean(z, axis=-1, keepdims=True)
    zc = z - mu
    var = jnp.mean(zc * zc, axis=-1, keepdims=True)
    o_ref[...] = zc * lax.rsqrt(var + 1e-5) * g_ref[...] + b_ref[...]


def _outproj(ya, yb, yc, ym, x2d, w, g, bias):
    n = x2d.shape[0]
    ytile = pl.BlockSpec((TM, GROUP), lambda i: (i, 0))
    const = lambda shape: pl.BlockSpec(shape, lambda i: (0,) * len(shape))
    return pl.pallas_call(
        _outproj_kernel,
        grid=(n // TM,),
        in_specs=[ytile, ytile, ytile, ytile,
                  pl.BlockSpec((TM, D_MODEL), lambda i: (i, 0)),
                  const((D_MODEL, D_MODEL)), const((1, D_MODEL)), const((1, D_MODEL))],
        out_specs=pl.BlockSpec((TM, D_MODEL), lambda i: (i, 0)),
        out_shape=jax.ShapeDtypeStruct((n, D_MODEL), F32),
        compiler_params=pltpu.CompilerParams(
            dimension_semantics=("parallel",), vmem_limit_bytes=VMEM_LIMIT),
        name="outproj",
    )(ya, yb, yc, ym, x2d, w, g, bias)


def kernel(x, mem, positions, w_in, rel_bias, mla_q_norm, w_uq, mla_kv_norm, w_ukv,
           swa_sinks, w_mem_kv, w_out, ln_gain, ln_bias):
    b, s, d = x.shape
    depth = w_in.shape[0]
    assert d == D_MODEL and depth == DEPTH and s % B_TK == 0 and s >= A_WIN and (b * s) % TM == 0

    cols, cperm = _inproj_cols()
    w_in_p = _take_cols(w_in, cols).astype(BF16)
    wuq_p = jnp.pad(_take_cols(w_uq, _uq_cols()), ((0, 0), (0, 256 - MLA_Q_RANK), (0, 0))).astype(BF16)
    wukv_p = _take_cols(w_ukv, _ukv_cols()).astype(BF16)
    gq = jnp.pad(mla_q_norm, ((0, 0), (0, 256 - MLA_Q_RANK)))[:, None, :]
    gkv = mla_kv_norm[:, None, :]
    rows = np.concatenate([np.arange(512), 512 + cperm, np.arange(768, 1024)])
    w_out_p = jnp.take(w_out, jnp.asarray(rows), axis=1).astype(BF16)
    w_mem_all = jnp.transpose(w_mem_kv, (1, 0, 2)).reshape(D_MODEL, depth * 512).astype(BF16)
    e_a = _bias_table_a(rel_bias)
    e_c = jnp.asarray(_mask_table_c())
    e_b = jnp.asarray(_mask_table_b())
    tab = _rope_tables(positions)

    memkv = _memkv(mem, w_mem_all)
    h = x.reshape(b * s, d)
    for l in range(depth):
        p3 = _inproj(h, tab, w_in_p[l], wuq_p[l], wukv_p[l], gq[l], gkv[l]).reshape(b, s, P_WIDTH)
        ya = _attn_a(p3, e_a[l])
        yb = _attn_b(p3, e_b)
        yc = _attn_c(p3, swa_sinks[l], e_c)
        ym = _attn_m(p3, memkv, l)
        flat = lambda y: y.reshape(b * s, GROUP)
        h = _outproj(flat(ya), flat(yb), flat(yc), flat(ym), h, w_out_p[l],
                     ln_gain[l][None, :], ln_bias[l][None, :])
    return h.reshape(b, s, d)
```

```python
import functools

import numpy as np
import jax
import jax.numpy as jnp
from jax import lax
from jax.experimental import pallas as pl
from jax.experimental.pallas import tpu as pltpu

F32 = jnp.float32
BF16 = jnp.bfloat16

D_MODEL = 1024
DEPTH = 4
CHUNK = 64
HEAD_DIM = 64
GROUP = 256
N_HEADS = 4
ROPE_THETA = 10000.0
NEG_INF = -1e30
A_PREV = 8
REL_CLIP = 128
MLA_NOPE = 64
MLA_ROPE = 32
MLA_Q_RANK = 192
MLA_KV_RANK = 128
SWA_PREV = 2
MEM_LEN = 256
ALPHA = (2.0 * DEPTH) ** 0.25

TQ = 128
A_WIN = TQ + A_PREV * CHUNK
C_WIN = TQ + SWA_PREV * CHUNK
B_TK = 512
B_TQ = 512
TM = 512
VMEM_LIMIT = 56 * 1024 * 1024
B_QSCALE = (MLA_NOPE + MLA_ROPE) ** -0.5 * 1.4426950408889634

P_AQ, P_AK, P_AV = 0, 256, 512
P_BQ0, P_BQ1, P_BK0, P_BK1, P_BV = 768, 1024, 1280, 1536, 1792
P_CQ, P_CK, P_CV = 2048, 2304, 2432
P_MQ = 2560
P_GA, P_GB, P_GC, P_GM = 2816, 3072, 3328, 3584
P_WIDTH = 3840

W_A, W_C, W_M, W_G, W_B, W_WIDTH = 0, 768, 1280, 1536, 2560, 3072

C_HEAD_ORDER = (0, 2, 1, 3)


def _inproj_cols():
    r = np.arange
    aq, ak, av, ag = 0, 256, 512, 768
    bcq, bckv, bkr, bg = 1024, 1216, 1344, 1376
    cq, ck, cv, cg = 1632, 1888, 2016, 2144
    mq, mg = 2400, 2656
    cperm = np.concatenate([r(64) + 64 * h for h in C_HEAD_ORDER])
    pad = lambda n: np.full(n, -1)
    cols = np.concatenate([
        aq + r(256), ak + r(256), av + r(256),
        cq + cperm, ck + r(128), cv + r(128),
        mq + r(256),
        ag + r(256), bg + r(256), cg + cperm, mg + r(256),
        bcq + r(192), pad(64), bckv + r(128), bkr + r(32), bkr + r(32), pad(64),
    ])
    assert cols.shape[0] == W_WIDTH
    return cols, cperm


def _take_cols(w, cols):
    pieces, i = [], 0
    while i < len(cols):
        j = i + 1
        if cols[i] < 0:
            while j < len(cols) and cols[j] < 0:
                j += 1
            pieces.append(jnp.zeros(w.shape[:-1] + (j - i,), w.dtype))
        else:
            while j < len(cols) and cols[j] == cols[j - 1] + 1:
                j += 1
            pieces.append(w[..., int(cols[i]):int(cols[i]) + (j - i)])
        i = j
    return jnp.concatenate(pieces, axis=-1)


def _uq_cols():
    r = np.arange
    per = MLA_NOPE + MLA_ROPE
    out = []
    for p in range(2):
        h0, h1 = 2 * p, 2 * p + 1
        out += [per * h0 + r(64), per * h1 + r(64),
                per * h0 + 64 + r(32), per * h1 + 64 + r(32), np.full(64, -1)]
    return np.concatenate(out)


def _ukv_cols():
    r = np.arange
    return np.concatenate([128 * h + r(64) for h in range(4)] + [128 * h + 64 + r(64) for h in range(4)])


def _rope_tables(positions):
    pos = positions.astype(F32).reshape(-1, 1)
    lane = np.arange(128)

    def tab(d):
        half = d // 2
        k = (lane % d) % half
        inv = ROPE_THETA ** (-jnp.asarray(2 * k, F32) / d)
        ang = pos * inv[None, :]
        sign = jnp.asarray(np.where((lane % d) < half, -1.0, 1.0), F32)
        return jnp.cos(ang), jnp.sin(ang) * sign[None, :]

    c64, s64 = tab(HEAD_DIM)
    c32, s32 = tab(MLA_ROPE)
    return jnp.concatenate([c64, s64, c32, s32], axis=-1)


def _bias_table_a(rel_bias):
    mb = np.arange(9)[:, None, None]
    i = np.arange(TQ)[None, :, None]
    c = np.arange(128)[None, None, :]
    m = mb * 128 + c
    rel = i + A_PREV * CHUNK - m
    dchunk = i // CHUNK + A_PREV - m // CHUNK
    valid = (dchunk >= 0) & (dchunk <= A_PREV)
    idx = np.clip(rel, -REL_CLIP, REL_CLIP) + REL_CLIP
    idx = np.broadcast_to(idx, (9, TQ, 128))
    valid = np.broadcast_to(valid, (9, TQ, 128))
    g = rel_bias[:, :, idx]
    g = jnp.where(jnp.asarray(valid)[None, None], g, NEG_INF)
    return jnp.transpose(g, (0, 2, 1, 3, 4)).reshape(DEPTH, 9, N_HEADS * TQ, 128)


def _mask_table_c():
    mb = np.arange(3)[:, None, None]
    i = np.arange(TQ)[None, :, None]
    c = np.arange(128)[None, None, :]
    m = mb * 128 + c
    dchunk = i // CHUNK + SWA_PREV - m // CHUNK
    valid = (dchunk >= 0) & (dchunk <= SWA_PREV)
    t = np.where(valid, 0.0, NEG_INF).astype(np.float32)
    return np.tile(t, (1, N_HEADS, 1))


def _mask_table_b():
    assert B_TQ == B_TK
    tile = np.arange(B_TQ // TQ)[:, None, None]
    c = np.arange(B_TK)[None, :, None]
    i = np.arange(TQ)[None, None, :]
    valid = (c // CHUNK) <= (TQ // CHUNK) * tile + i // CHUNK
    diag = np.where(valid, 0.0, NEG_INF).astype(np.float32)
    return np.stack([np.zeros_like(diag), diag])


def _dot(a, b):
    return jnp.dot(a, b, preferred_element_type=F32)


def _dot_nt(a, b):
    return lax.dot_general(a, b, (((1,), (1,)), ((), ())), preferred_element_type=F32)


def _dot_tn(a, b):
    return lax.dot_general(a, b, (((0,), (0,)), ((), ())), preferred_element_type=F32)


def _rope(x, cos, sin_signed, half):
    lane = lax.broadcasted_iota(jnp.int32, x.shape, 1)
    first = (lane & (2 * half - 1)) < half
    swapped = jnp.where(first, pltpu.roll(x, 128 - half, 1), pltpu.roll(x, half, 1))
    return x * cos + swapped * sin_signed


def _stack_masked(q, lane_masks):
    qf = q.astype(F32)
    return jnp.concatenate([jnp.where(mk, qf, 0.0) for mk in lane_masks], axis=0).astype(BF16)


def _head_masks(rows, width):
    lane = lax.broadcasted_iota(jnp.int32, (rows, width), 1)
    return [(lane >= HEAD_DIM * h) & (lane < HEAD_DIM * (h + 1)) for h in range(width // HEAD_DIM)]


def _unstack(o, lane_masks):
    rows = o.shape[0] // len(lane_masks)
    out = o[(len(lane_masks) - 1) * rows:]
    for h in range(len(lane_masks) - 2, -1, -1):
        out = jnp.where(lane_masks[h], o[h * rows:(h + 1) * rows], out)
    return out


def _softmax_pv(s, v, sink=None):
    m = jnp.max(s, axis=-1, keepdims=True)
    if sink is not None:
        m = jnp.maximum(m, sink)
    p = jnp.exp(s - m)
    l = jnp.sum(p, axis=-1, keepdims=True)
    if sink is not None:
        l = l + jnp.exp(sink - m)
    o = _dot(p.astype(BF16), v)
    return o * (1.0 / l)


def _inproj_kernel(x_ref, tab_ref, w_ref, wuq_ref, wukv_ref, gq_ref, gkv_ref, p_ref):
    xb = x_ref[...].astype(BF16)
    cos64, sin64 = tab_ref[:, 0:128], tab_ref[:, 128:256]
    cos32, sin32 = tab_ref[:, 256:384], tab_ref[:, 384:512]

    def mm(lo, hi):
        return _dot(xb, w_ref[:, lo:hi])

    r = mm(W_A, W_A + 768)
    p_ref[:, P_AQ:P_AQ + 256] = (r[:, 0:256] * 0.125).astype(BF16)
    p_ref[:, P_AK:P_AK + 512] = r[:, 256:768].astype(BF16)

    r = mm(W_C, W_C + 512)
    for j in range(2):
        qj = _rope(r[:, 128 * j:128 * (j + 1)], cos64, sin64, 32)
        p_ref[:, P_CQ + 128 * j:P_CQ + 128 * (j + 1)] = (qj * 0.125).astype(BF16)
    p_ref[:, P_CK:P_CK + 128] = _rope(r[:, 256:384], cos64, sin64, 32).astype(BF16)
    p_ref[:, P_CV:P_CV + 128] = r[:, 384:512].astype(BF16)

    r = mm(W_M, W_M + 256)
    p_ref[:, P_MQ:P_MQ + 256] = (r * 0.125).astype(BF16)

    r = mm(W_G, W_G + 1024)
    p_ref[:, P_GA:P_GA + 1024] = (r * (1.0 / (1.0 + jnp.exp(-r)))).astype(BF16)

    r = mm(W_B, W_B + 512)
    cq = r[:, 0:256]
    ms = jnp.sum(cq * cq, axis=-1, keepdims=True) * (1.0 / MLA_Q_RANK)
    qn = (cq * lax.rsqrt(ms + 1e-6) * gq_ref[...]).astype(BF16)
    q = _dot(qn, wuq_ref[...]) * B_QSCALE
    for p in range(2):
        base = P_BQ0 + 256 * p
        p_ref[:, base:base + 128] = q[:, 256 * p:256 * p + 128].astype(BF16)
        p_ref[:, base + 128:base + 256] = _rope(q[:, 256 * p + 128:256 * p + 256], cos32, sin32, 16).astype(BF16)

    ckv = r[:, 256:384]
    ms = jnp.mean(ckv * ckv, axis=-1, keepdims=True)
    kvn = (ckv * lax.rsqrt(ms + 1e-6) * gkv_ref[...]).astype(BF16)
    kv = _dot(kvn, wukv_ref[...])
    krb = _rope(r[:, 384:512], cos32, sin32, 16).astype(BF16)
    for p in range(2):
        base = P_BK0 + 256 * p
        p_ref[:, base:base + 128] = kv[:, 128 * p:128 * (p + 1)].astype(BF16)
        p_ref[:, base + 128:base + 256] = krb
    p_ref[:, P_BV:P_BV + 256] = kv[:, 256:512].astype(BF16)


def _inproj(x2d, tab, w, wuq, wukv, gq, gkv):
    n = x2d.shape[0]
    const = lambda shape: pl.BlockSpec(shape, lambda i: (0,) * len(shape))
    return pl.pallas_call(
        _inproj_kernel,
        grid=(n // TM,),
        in_specs=[
            pl.BlockSpec((TM, D_MODEL), lambda i: (i, 0)),
            pl.BlockSpec((TM, 512), lambda i: (i, 0)),
            const((D_MODEL, W_WIDTH)),
            const((256, 512)),
            const((128, 512)),
            const((1, 256)),
            const((1, 128)),
        ],
        out_specs=pl.BlockSpec((TM, P_WIDTH), lambda i: (i, 0)),
        out_shape=jax.ShapeDtypeStruct((n, P_WIDTH), BF16),
        compiler_params=pltpu.CompilerParams(
            dimension_semantics=("parallel",), vmem_limit_bytes=VMEM_LIMIT),
        name="inproj",
    )(x2d, tab, w, wuq, wukv, gq, gkv)


def _memkv_kernel(mem_ref, w_ref, o_ref):
    o_ref[0] = _dot(mem_ref[0].astype(BF16), w_ref[...]).astype(BF16)


def _memkv(mem, w_all):
    b = mem.shape[0]
    n = w_all.shape[1]
    return pl.pallas_call(
        _memkv_kernel,
        grid=(b,),
        in_specs=[pl.BlockSpec((1, MEM_LEN, D_MODEL), lambda i: (i, 0, 0)),
                  pl.BlockSpec((D_MODEL, n), lambda i: (0, 0))],
        out_specs=pl.BlockSpec((1, MEM_LEN, n), lambda i: (i, 0, 0)),
        out_shape=jax.ShapeDtypeStruct((b, MEM_LEN, n), BF16),
        compiler_params=pltpu.CompilerParams(
            dimension_semantics=("parallel",), vmem_limit_bytes=VMEM_LIMIT),
        name="memkv",
    )(mem, w_all)


def _attn_a_kernel(q_ref, k_ref, v_ref, g_ref, e_ref, o_ref):
    t = pl.program_id(1)
    start = pl.multiple_of(jnp.maximum(t * TQ - A_PREV * CHUNK, 0), 128)
    mb0 = jnp.maximum(A_PREV * CHUNK // 128 - t, 0)
    masks = _head_masks(TQ, GROUP)
    qs = _stack_masked(q_ref[0], masks)
    kwin = k_ref[0, pl.ds(start, A_WIN), :]
    vwin = v_ref[0, pl.ds(start, A_WIN), :]
    s = _dot_nt(qs, kwin)
    s = s + jnp.concatenate([e_ref[mb0 + j] for j in range(A_WIN // 128)], axis=1)
    o = _unstack(_softmax_pv(s, vwin), masks)
    o_ref[0] = (o * g_ref[0].astype(F32)).astype(BF16)


def _attn_a(p3, e):
    b, s, _ = p3.shape
    return pl.pallas_call(
        _attn_a_kernel,
        grid=(b, s // TQ),
        in_specs=[
            pl.BlockSpec((1, TQ, 256), lambda i, t: (i, t, P_AQ // 256)),
            pl.BlockSpec((1, s, 256), lambda i, t: (i, 0, P_AK // 256)),
            pl.BlockSpec((1, s, 256), lambda i, t: (i, 0, P_AV // 256)),
            pl.BlockSpec((1, TQ, 256), lambda i, t: (i, t, P_GA // 256)),
            pl.BlockSpec((9, N_HEADS * TQ, 128), lambda i, t: (0, 0, 0)),
        ],
        out_specs=pl.BlockSpec((1, TQ, 256), lambda i, t: (i, t, 0)),
        out_shape=jax.ShapeDtypeStruct((b, s, GROUP), BF16),
        compiler_params=pltpu.CompilerParams(
            dimension_semantics=("parallel", "arbitrary"), vmem_limit_bytes=VMEM_LIMIT),
        name="attn_a",
    )(p3, p3, p3, p3, e)


def _attn_c_kernel(sink_ref, q_ref, k_ref, v_ref, g_ref, e_ref, o_ref):
    t = pl.program_id(1)
    start = pl.multiple_of(jnp.maximum(t * TQ - SWA_PREV * CHUNK, 0), 128)
    mb0 = jnp.maximum(1 - t, 0)
    lane = lax.broadcasted_iota(jnp.int32, (TQ, 128), 1)
    lo, hi = lane < HEAD_DIM, lane >= HEAD_DIM
    q = q_ref[0].astype(F32)
    qs = jnp.concatenate([jnp.where(lo, q[:, 0:128], 0.0), jnp.where(hi, q[:, 0:128], 0.0),
                          jnp.where(lo, q[:, 128:256], 0.0), jnp.where(hi, q[:, 128:256], 0.0)],
                         axis=0).astype(BF16)
    kwin = k_ref[0, pl.ds(start, C_WIN), :]
    vwin = v_ref[0, pl.ds(start, C_WIN), :]
    s = _dot_nt(qs, kwin)
    s = s + jnp.concatenate([e_ref[mb0 + j] for j in range(C_WIN // 128)], axis=1)
    row = lax.broadcasted_iota(jnp.int32, (N_HEADS * TQ, 1), 0)
    sink = jnp.where(row < TQ, sink_ref[C_HEAD_ORDER[0]],
                     jnp.where(row < 2 * TQ, sink_ref[C_HEAD_ORDER[1]],
                               jnp.where(row < 3 * TQ, sink_ref[C_HEAD_ORDER[2]], sink_ref[C_HEAD_ORDER[3]])))
    o = _softmax_pv(s, vwin, sink=sink)
    out = jnp.concatenate([jnp.where(lo, o[0:TQ], o[TQ:2 * TQ]),
                           jnp.where(lo, o[2 * TQ:3 * TQ], o[3 * TQ:4 * TQ])], axis=1)
    o_ref[0] = (out * g_ref[0].astype(F32)).astype(BF16)


def _attn_c(p3, sinks, e):
    b, s, _ = p3.shape
    return pl.pallas_call(
        _attn_c_kernel,
        grid=(b, s // TQ),
        in_specs=[
            pl.BlockSpec(memory_space=pltpu.SMEM),
            pl.BlockSpec((1, TQ, 256), lambda i, t: (i, t, P_CQ // 256)),
            pl.BlockSpec((1, s, 128), lambda i, t: (i, 0, P_CK // 128)),
            pl.BlockSpec((1, s, 128), lambda i, t: (i, 0, P_CV // 128)),
            pl.BlockSpec((1, TQ, 256), lambda i, t: (i, t, P_GC // 256)),
            pl.BlockSpec((3, N_HEADS * TQ, 128), lambda i, t: (0, 0, 0)),
        ],
        out_specs=pl.BlockSpec((1, TQ, 256), lambda i, t: (i, t, 0)),
        out_shape=jax.ShapeDtypeStruct((b, s, GROUP), BF16),
        compiler_params=pltpu.CompilerParams(
            dimension_semantics=("parallel", "arbitrary"), vmem_limit_bytes=VMEM_LIMIT),
        name="attn_c",
    )(sinks, p3, p3, p3, p3, e)


def _attn_m_kernel(q_ref, k_ref, v_ref, g_ref, o_ref):
    masks = _head_masks(TQ, GROUP)
    qs = _stack_masked(q_ref[0], masks)
    s = _dot_nt(qs, k_ref[0])
    o = _unstack(_softmax_pv(s, v_ref[0]), masks)
    o_ref[0] = (o * g_ref[0].astype(F32)).astype(BF16)


def _attn_m(p3, memkv, layer):
    b, s, _ = p3.shape
    return pl.pallas_call(
        _attn_m_kernel,
        grid=(b, s // TQ),
        in_specs=[
            pl.BlockSpec((1, TQ, 256), lambda i, t: (i, t, P_MQ // 256)),
            pl.BlockSpec((1, MEM_LEN, 256), lambda i, t: (i, 0, 2 * layer)),
            pl.BlockSpec((1, MEM_LEN, 256), lambda i, t: (i, 0, 2 * layer + 1)),
            pl.BlockSpec((1, TQ, 256), lambda i, t: (i, t, P_GM // 256)),
        ],
        out_specs=pl.BlockSpec((1, TQ, 256), lambda i, t: (i, t, 0)),
        out_shape=jax.ShapeDtypeStruct((b, s, GROUP), BF16),
        compiler_params=pltpu.CompilerParams(
            dimension_semantics=("parallel", "arbitrary"), vmem_limit_bytes=VMEM_LIMIT),
        name="attn_m",
    )(p3, memkv, memkv, p3)


def _attn_b_kernel(q0_ref, q1_ref, k0_ref, k1_ref, v_ref, g_ref, mask_ref, o_ref,
                   qs_sc, sa_sc, sb_sc, m_sc, l_sc, acc_sc):
    t = pl.program_id(1)
    n_tiles = B_TQ // TQ
    lane = lax.broadcasted_iota(jnp.int32, (TQ, GROUP), 1)
    slot = [(lane < 64) | ((lane >= 128) & (lane < 160)),
            ((lane >= 64) & (lane < 128)) | ((lane >= 160) & (lane < 192))]
    q_refs = (q0_ref, q1_ref)
    k_refs = (k0_ref, k1_ref)
    for c in range(n_tiles):
        for pr in range(2):
            qs_sc[2 * c + pr] = _stack_masked(q_refs[pr][0, TQ * c:TQ * (c + 1), :], slot)
    m_sc[...] = jnp.full(m_sc.shape, NEG_INF, F32)
    l_sc[...] = jnp.zeros(l_sc.shape, F32)
    acc_sc[...] = jnp.zeros(acc_sc.shape, F32)
    n_items = (t + 1) * n_tiles

    def scores_into(item, s_sc):
        kb, c = item // n_tiles, item % n_tiles
        start = pl.multiple_of(kb * B_TK, B_TK)
        for pr in range(2):
            s_sc[pr] = _dot_nt(k_refs[pr][0, pl.ds(start, B_TK), :], qs_sc[2 * c + pr])

    def consume(item, s_sc):
        kb, c = item // n_tiles, item % n_tiles
        start = pl.multiple_of(kb * B_TK, B_TK)
        diag = (kb == t).astype(jnp.int32)
        mask = jnp.concatenate([mask_ref[diag, c]] * 2, axis=1)
        v = v_ref[0, pl.ds(start, B_TK), :]
        for pr in range(2):
            u = 2 * c + pr
            s = s_sc[pr] + mask
            m_prev = m_sc[u]
            m_new = jnp.maximum(m_prev, jnp.max(s, axis=0, keepdims=True))
            alpha = jnp.exp2(m_prev - m_new)
            p = jnp.exp2(s - m_new)
            l_sc[u] = alpha * l_sc[u] + jnp.sum(p, axis=0, keepdims=True)
            m_sc[u] = m_new
            acc_sc[u] = alpha * acc_sc[u] + _dot_tn(v[:, 128 * pr:128 * (pr + 1)], p.astype(BF16))

    scores_into(0, sa_sc)

    def body(i, carry):
        scores_into(2 * i + 1, sb_sc)
        consume(2 * i, sa_sc)
        scores_into(jnp.minimum(2 * i + 2, n_items - 1), sa_sc)
        consume(2 * i + 1, sb_sc)
        return carry

    lax.fori_loop(0, n_items // 2, body, 0)

    for c in range(n_tiles):
        parts = []
        for pr in range(2):
            u = 2 * c + pr
            inv = 1.0 / l_sc[u]
            for e in range(2):
                parts.append(acc_sc[u, 64 * e:64 * (e + 1), 128 * e:128 * (e + 1)] * inv[:, 128 * e:128 * (e + 1)])
        o = jnp.concatenate(parts, axis=0).T
        o_ref[0, TQ * c:TQ * (c + 1), :] = (o * g_ref[0, TQ * c:TQ * (c + 1), :].astype(F32)).astype(BF16)


def _attn_b(p3, mask):
    b, s, _ = p3.shape
    n_units = 2 * (B_TQ // TQ)
    return pl.pallas_call(
        _attn_b_kernel,
        grid=(b, s // B_TQ),
        in_specs=[
            pl.BlockSpec((1, B_TQ, 256), lambda i, t: (i, t, P_BQ0 // 256)),
            pl.BlockSpec((1, B_TQ, 256), lambda i, t: (i, t, P_BQ1 // 256)),
            pl.BlockSpec((1, s, 256), lambda i, t: (i, 0, P_BK0 // 256)),
            pl.BlockSpec((1, s, 256), lambda i, t: (i, 0, P_BK1 // 256)),
            pl.BlockSpec((1, s, 256), lambda i, t: (i, 0, P_BV // 256)),
            pl.BlockSpec((1, B_TQ, 256), lambda i, t: (i, t, P_GB // 256)),
            pl.BlockSpec((2, B_TQ // TQ, B_TK, TQ), lambda i, t: (0, 0, 0, 0)),
        ],
        out_specs=pl.BlockSpec((1, B_TQ, 256), lambda i, t: (i, t, 0)),
        out_shape=jax.ShapeDtypeStruct((b, s, GROUP), BF16),
        scratch_shapes=[pltpu.VMEM((n_units, 2 * TQ, 256), BF16),
                        pltpu.VMEM((2, B_TK, 2 * TQ), F32),
                        pltpu.VMEM((2, B_TK, 2 * TQ), F32),
                        pltpu.VMEM((n_units, 1, 2 * TQ), F32),
                        pltpu.VMEM((n_units, 1, 2 * TQ), F32),
                        pltpu.VMEM((n_units, 128, 2 * TQ), F32)],
        compiler_params=pltpu.CompilerParams(
            dimension_semantics=("parallel", "arbitrary"), vmem_limit_bytes=VMEM_LIMIT),
        name="attn_b",
    )(p3, p3, p3, p3, p3, p3, mask)


def _outproj_kernel(ya_ref, yb_ref, yc_ref, ym_ref, x_ref, w_ref, g_ref, b_ref, o_ref):
    y = (_dot(ya_ref[...], w_ref[0:256, :]) + _dot(yb_ref[...], w_ref[256:512, :])
         + _dot(yc_ref[...], w_ref[512:768, :]) + _dot(ym_ref[...], w_ref[768:1024, :]))
    z = ALPHA * x_ref[...] + y
    mu = jnp.mean(z, axis=-1, keepdims=True)
    zc = z - mu
    var = jnp.mean(zc * zc, axis=-1, keepdims=True)
    o_ref[...] = zc * lax.rsqrt(var + 1e-5) * g_ref[...] + b_ref[...]


def _outproj(ya, yb, yc, ym, x2d, w, g, bias):
    n = x2d.shape[0]
    ytile = pl.BlockSpec((TM, GROUP), lambda i: (i, 0))
    const = lambda shape: pl.BlockSpec(shape, lambda i: (0,) * len(shape))
    return pl.pallas_call(
        _outproj_kernel,
        grid=(n // TM,),
        in_specs=[ytile, ytile, ytile, ytile,
                  pl.BlockSpec((TM, D_MODEL), lambda i: (i, 0)),
                  const((D_MODEL, D_MODEL)), const((1, D_MODEL)), const((1, D_MODEL))],
        out_specs=pl.BlockSpec((TM, D_MODEL), lambda i: (i, 0)),
        out_shape=jax.ShapeDtypeStruct((n, D_MODEL), F32),
        compiler_params=pltpu.CompilerParams(
            dimension_semantics=("parallel",), vmem_limit_bytes=VMEM_LIMIT),
        name="outproj",
    )(ya, yb, yc, ym, x2d, w, g, bias)


def kernel(x, mem, positions, w_in, rel_bias, mla_q_norm, w_uq, mla_kv_norm, w_ukv,
           swa_sinks, w_mem_kv, w_out, ln_gain, ln_bias):
    b, s, d = x.shape
    depth = w_in.shape[0]
    assert d == D_MODEL and depth == DEPTH and s % B_TK == 0 and s >= A_WIN and (b * s) % TM == 0

    cols, cperm = _inproj_cols()
    w_in_p = _take_cols(w_in, cols).astype(BF16)
    wuq_p = jnp.pad(_take_cols(w_uq, _uq_cols()), ((0, 0), (0, 256 - MLA_Q_RANK), (0, 0))).astype(BF16)
    wukv_p = _take_cols(w_ukv, _ukv_cols()).astype(BF16)
    gq = jnp.pad(mla_q_norm, ((0, 0), (0, 256 - MLA_Q_RANK)))[:, None, :]
    gkv = mla_kv_norm[:, None, :]
    rows = np.concatenate([np.arange(512), 512 + cperm, np.arange(768, 1024)])
    w_out_p = jnp.take(w_out, jnp.asarray(rows), axis=1).astype(BF16)
    w_mem_all = jnp.transpose(w_mem_kv, (1, 0, 2)).reshape(D_MODEL, depth * 512).astype(BF16)
    e_a = _bias_table_a(rel_bias)
    e_c = jnp.asarray(_mask_table_c())
    e_b = jnp.asarray(_mask_table_b())
    tab = _rope_tables(positions)

    memkv = _memkv(mem, w_mem_all)
    h = x.reshape(b * s, d)
    for l in range(depth):
        p3 = _inproj(h, tab, w_in_p[l], wuq_p[l], wukv_p[l], gq[l], gkv[l]).reshape(b, s, P_WIDTH)
        ya = _attn_a(p3, e_a[l])
        yb = _attn_b(p3, e_b)
        yc = _attn_c(p3, swa_sinks[l], e_c)
        ym = _attn_m(p3, memkv, l)
        flat = lambda y: y.reshape(b * s, GROUP)
        h = _outproj(flat(ya), flat(yb), flat(yc), flat(ym), h, w_out_p[l],
                     ln_gain[l][None, :], ln_bias[l][None, :])
    return h.reshape(b, s, d)
```

```python
import functools

import numpy as np
import jax
import jax.numpy as jnp
from jax import lax
from jax.experimental import pallas as pl
from jax.experimental.pallas import tpu as pltpu

F32 = jnp.float32
BF16 = jnp.bfloat16

D_MODEL = 1024
DEPTH = 4
CHUNK = 64
HEAD_DIM = 64
GROUP = 256
N_HEADS = 4
ROPE_THETA = 10000.0
NEG_INF = -1e30
A_PREV = 8
REL_CLIP = 128
MLA_NOPE = 64
MLA_ROPE = 32
MLA_Q_RANK = 192
MLA_KV_RANK = 128
SWA_PREV = 2
MEM_LEN = 256
ALPHA = (2.0 * DEPTH) ** 0.25

TQ = 128
A_WIN = TQ + A_PREV * CHUNK
C_WIN = TQ + SWA_PREV * CHUNK
B_TK = 512
B_TQ = 512
TM = 512
VMEM_LIMIT = 56 * 1024 * 1024
LOG2E = 1.4426950408889634
QSCALE = HEAD_DIM ** -0.5 * LOG2E
B_QSCALE = (MLA_NOPE + MLA_ROPE) ** -0.5 * LOG2E

P_AQ, P_AK, P_AV = 0, 256, 512
P_BQ0, P_BQ1, P_BK0, P_BK1, P_BV = 768, 1024, 1280, 1536, 1792
P_CQ, P_CK, P_CV = 2048, 2304, 2432
P_MQ = 2560
P_GA, P_GB, P_GC, P_GM = 2816, 3072, 3328, 3584
P_WIDTH = 3840

W_A, W_C, W_M, W_G, W_B, W_WIDTH = 0, 768, 1280, 1536, 2560, 3072

C_HEAD_ORDER = (0, 2, 1, 3)


def _inproj_cols():
    r = np.arange
    aq, ak, av, ag = 0, 256, 512, 768
    bcq, bckv, bkr, bg = 1024, 1216, 1344, 1376
    cq, ck, cv, cg = 1632, 1888, 2016, 2144
    mq, mg = 2400, 2656
    cperm = np.concatenate([r(64) + 64 * h for h in C_HEAD_ORDER])
    pad = lambda n: np.full(n, -1)
    cols = np.concatenate([
        aq + r(256), ak + r(256), av + r(256),
        cq + cperm, ck + r(128), cv + r(128),
        mq + r(256),
        ag + r(256), bg + r(256), cg + cperm, mg + r(256),
        bcq + r(192), pad(64), bckv + r(128), bkr + r(32), bkr + r(32), pad(64),
    ])
    assert cols.shape[0] == W_WIDTH
    return cols, cperm


def _take_cols(w, cols, axis=-1):
    axis = axis % w.ndim
    pieces, i = [], 0
    while i < len(cols):
        j = i + 1
        if cols[i] < 0:
            while j < len(cols) and cols[j] < 0:
                j += 1
            shape = w.shape[:axis] + (j - i,) + w.shape[axis + 1:]
            pieces.append(jnp.zeros(shape, w.dtype))
        else:
            while j < len(cols) and cols[j] == cols[j - 1] + 1:
                j += 1
            pieces.append(lax.slice_in_dim(w, int(cols[i]), int(cols[i]) + (j - i), axis=axis))
        i = j
    return jnp.concatenate(pieces, axis=axis)


def _uq_cols():
    r = np.arange
    per = MLA_NOPE + MLA_ROPE
    out = []
    for p in range(2):
        h0, h1 = 2 * p, 2 * p + 1
        out += [per * h0 + r(64), per * h1 + r(64),
                per * h0 + 64 + r(32), per * h1 + 64 + r(32), np.full(64, -1)]
    return np.concatenate(out)


def _ukv_cols():
    r = np.arange
    return np.concatenate([128 * h + r(64) for h in range(4)] + [128 * h + 64 + r(64) for h in range(4)])


def _rope_tables(positions):
    pos = positions.astype(F32).reshape(-1, 1)
    lane = np.arange(128)

    def tab(d):
        half = d // 2
        k = (lane % d) % half
        inv = ROPE_THETA ** (-jnp.asarray(2 * k, F32) / d)
        ang = pos * inv[None, :]
        sign = jnp.asarray(np.where((lane % d) < half, -1.0, 1.0), F32)
        return jnp.cos(ang), jnp.sin(ang) * sign[None, :]

    c64, s64 = tab(HEAD_DIM)
    c32, s32 = tab(MLA_ROPE)
    return jnp.concatenate([c64, s64, c32, s32], axis=-1)


def _bias_table_a(rel_bias):
    width, period = 9 * 128, 9 * 128 + TQ
    k = np.arange(period)
    d = np.where(k < width, A_PREV * CHUNK - k, A_PREV * CHUNK + period - k)
    idx = np.clip(d, -REL_CLIP, REL_CLIP) + REL_CLIP
    n_hi = A_PREV * CHUNK - REL_CLIP + 1
    n_lo = width - n_hi - (2 * REL_CLIP - 1)
    expect = np.concatenate([np.full(n_hi, 2 * REL_CLIP), np.arange(2 * REL_CLIP - 1, 0, -1),
                             np.zeros(n_lo, np.int64), np.full(period - width, 2 * REL_CLIP)])
    assert np.array_equal(idx, expect)
    rep = lambda col, n: jnp.broadcast_to(rel_bias[:, :, col:col + 1], rel_bias.shape[:2] + (n,))
    gp = jnp.concatenate([rep(2 * REL_CLIP, n_hi), jnp.flip(rel_bias[:, :, 1:2 * REL_CLIP], axis=-1),
                          rep(0, n_lo), rep(2 * REL_CLIP, period - width)], axis=-1) * LOG2E
    flat = jnp.tile(gp, (1, 1, TQ))[:, :, :TQ * (period - 1)]
    skew = flat.reshape(gp.shape[0], N_HEADS, TQ, period - 1)[..., :width]
    i = np.arange(TQ)[:, None]
    m = np.arange(width)[None, :]
    dchunk = i // CHUNK + A_PREV - m // CHUNK
    valid = (dchunk >= 0) & (dchunk <= A_PREV)
    t = jnp.where(jnp.asarray(valid)[None, None], skew, NEG_INF)
    return jnp.transpose(t, (0, 3, 1, 2)).reshape(gp.shape[0], 9, 128, N_HEADS * TQ)


def _mask_table_c():
    m = np.arange(3 * 128)[:, None]
    i = np.arange(TQ)[None, :]
    dchunk = i // CHUNK + SWA_PREV - m // CHUNK
    valid = (dchunk >= 0) & (dchunk <= SWA_PREV)
    t = np.where(valid, 0.0, NEG_INF).astype(np.float32)
    return np.tile(t, (1, N_HEADS)).reshape(3, 128, N_HEADS * TQ)


def _mask_table_b():
    assert B_TQ == B_TK
    tile = np.arange(B_TQ // TQ)[:, None, None]
    c = np.arange(B_TK)[None, :, None]
    i = np.arange(TQ)[None, None, :]
    valid = (c // CHUNK) <= (TQ // CHUNK) * tile + i // CHUNK
    diag = np.where(valid, 0.0, NEG_INF).astype(np.float32)
    return np.stack([np.zeros_like(diag), diag])


def _dot(a, b):
    return jnp.dot(a, b, preferred_element_type=F32)


def _dot_nt(a, b):
    return lax.dot_general(a, b, (((1,), (1,)), ((), ())), preferred_element_type=F32)


def _dot_tn(a, b):
    return lax.dot_general(a, b, (((0,), (0,)), ((), ())), preferred_element_type=F32)


def _rope(x, cos, sin_signed, half):
    lane = lax.broadcasted_iota(jnp.int32, x.shape, 1)
    first = (lane & (2 * half - 1)) < half
    swapped = jnp.where(first, pltpu.roll(x, 128 - half, 1), pltpu.roll(x, half, 1))
    return x * cos + swapped * sin_signed


def _stack_masked(q, lane_masks):
    qf = q.astype(F32)
    return jnp.concatenate([jnp.where(mk, qf, 0.0) for mk in lane_masks], axis=0).astype(BF16)


def _inproj_kernel(x_ref, tab_ref, w_ref, wuq_ref, wukv_ref, gq_ref, gkv_ref, p_ref):
    xb = x_ref[...].astype(BF16)
    cos64, sin64 = tab_ref[:, 0:128], tab_ref[:, 128:256]
    cos32, sin32 = tab_ref[:, 256:384], tab_ref[:, 384:512]

    def mm(lo, hi):
        return _dot(xb, w_ref[:, lo:hi])

    r = mm(W_A, W_A + 768)
    p_ref[:, P_AQ:P_AQ + 256] = (r[:, 0:256] * QSCALE).astype(BF16)
    p_ref[:, P_AK:P_AK + 512] = r[:, 256:768].astype(BF16)

    r = mm(W_C, W_C + 512)
    for j in range(2):
        qj = _rope(r[:, 128 * j:128 * (j + 1)], cos64, sin64, 32)
        p_ref[:, P_CQ + 128 * j:P_CQ + 128 * (j + 1)] = (qj * QSCALE).astype(BF16)
    p_ref[:, P_CK:P_CK + 128] = _rope(r[:, 256:384], cos64, sin64, 32).astype(BF16)
    p_ref[:, P_CV:P_CV + 128] = r[:, 384:512].astype(BF16)

    r = mm(W_M, W_M + 256)
    p_ref[:, P_MQ:P_MQ + 256] = (r * QSCALE).astype(BF16)

    r = mm(W_G, W_G + 1024)
    p_ref[:, P_GA:P_GA + 1024] = (r * (1.0 / (1.0 + jnp.exp(-r)))).astype(BF16)

    r = mm(W_B, W_B + 512)
    cq = r[:, 0:256]
    ms = jnp.sum(cq * cq, axis=-1, keepdims=True) * (1.0 / MLA_Q_RANK)
    qn = (cq * lax.rsqrt(ms + 1e-6) * gq_ref[...]).astype(BF16)
    q = _dot(qn, wuq_ref[...]) * B_QSCALE
    for p in range(2):
        base = P_BQ0 + 256 * p
        p_ref[:, base:base + 128] = q[:, 256 * p:256 * p + 128].astype(BF16)
        p_ref[:, base + 128:base + 256] = _rope(q[:, 256 * p + 128:256 * p + 256], cos32, sin32, 16).astype(BF16)

    ckv = r[:, 256:384]
    ms = jnp.mean(ckv * ckv, axis=-1, keepdims=True)
    kvn = (ckv * lax.rsqrt(ms + 1e-6) * gkv_ref[...]).astype(BF16)
    kv = _dot(kvn, wukv_ref[...])
    krb = _rope(r[:, 384:512], cos32, sin32, 16).astype(BF16)
    for p in range(2):
        base = P_BK0 + 256 * p
        p_ref[:, base:base + 128] = kv[:, 128 * p:128 * (p + 1)].astype(BF16)
        p_ref[:, base + 128:base + 256] = krb
    p_ref[:, P_BV:P_BV + 256] = kv[:, 256:512].astype(BF16)


def _inproj(x2d, tab, w, wuq, wukv, gq, gkv):
    n = x2d.shape[0]
    const = lambda shape: pl.BlockSpec(shape, lambda i: (0,) * len(shape))
    return pl.pallas_call(
        _inproj_kernel,
        grid=(n // TM,),
        in_specs=[
            pl.BlockSpec((TM, D_MODEL), lambda i: (i, 0)),
            pl.BlockSpec((TM, 512), lambda i: (i, 0)),
            const((D_MODEL, W_WIDTH)),
            const((256, 512)),
            const((128, 512)),
            const((1, 256)),
            const((1, 128)),
        ],
        out_specs=pl.BlockSpec((TM, P_WIDTH), lambda i: (i, 0)),
        out_shape=jax.ShapeDtypeStruct((n, P_WIDTH), BF16),
        compiler_params=pltpu.CompilerParams(
            dimension_semantics=("parallel",), vmem_limit_bytes=VMEM_LIMIT),
        name="inproj",
    )(x2d, tab, w, wuq, wukv, gq, gkv)


def _memkv_kernel(mem_ref, w_ref, o_ref):
    o_ref[0] = _dot(mem_ref[0].astype(BF16), w_ref[...]).astype(BF16)


def _memkv(mem, w_all):
    b = mem.shape[0]
    n = w_all.shape[1]
    return pl.pallas_call(
        _memkv_kernel,
        grid=(b,),
        in_specs=[pl.BlockSpec((1, MEM_LEN, D_MODEL), lambda i: (i, 0, 0)),
                  pl.BlockSpec((D_MODEL, n), lambda i: (0, 0))],
        out_specs=pl.BlockSpec((1, MEM_LEN, n), lambda i: (i, 0, 0)),
        out_shape=jax.ShapeDtypeStruct((b, MEM_LEN, n), BF16),
        compiler_params=pltpu.CompilerParams(
            dimension_semantics=("parallel",), vmem_limit_bytes=VMEM_LIMIT),
        name="memkv",
    )(mem, w_all)


def _window_attn_kernel(*refs, win, prev, dk, has_table, has_sink):
    refs = list(refs)
    sink_ref = refs.pop(0) if has_sink else None
    q_ref, k_ref, v_ref, g_ref = refs[:4]
    e_ref = refs[4] if has_table else None
    o_ref, sa_sc, sb_sc = refs[-3:]
    n_items = q_ref.shape[1] // TQ
    lanes = N_HEADS * TQ

    lane128 = lax.broadcasted_iota(jnp.int32, (TQ, 128), 1)
    lo, hi = lane128 < HEAD_DIM, lane128 >= HEAD_DIM
    if has_sink:
        col = lax.broadcasted_iota(jnp.int32, (1, lanes), 1)
        order = C_HEAD_ORDER if dk == 128 else tuple(range(N_HEADS))
        sink = jnp.where(col < TQ, sink_ref[order[0]],
                         jnp.where(col < 2 * TQ, sink_ref[order[1]],
                                   jnp.where(col < 3 * TQ, sink_ref[order[2]], sink_ref[order[3]]))) * LOG2E

    def window_start(item):
        if prev is None:
            return 0
        return pl.multiple_of(jnp.maximum(item * TQ - prev, 0), 128)

    def scores_into(item, s_sc):
        q = q_ref[0, pl.ds(pl.multiple_of(item * TQ, TQ), TQ), :].astype(F32)
        if dk == 256:
            halves = [q[:, 0:128], q[:, 0:128], q[:, 128:256], q[:, 128:256]]
            zero = jnp.zeros((TQ, 128), F32)
            blocks = [jnp.concatenate([jnp.where(lo, halves[0], 0.0), zero], axis=1),
                      jnp.concatenate([jnp.where(hi, halves[1], 0.0), zero], axis=1),
                      jnp.concatenate([zero, jnp.where(lo, halves[2], 0.0)], axis=1),
                      jnp.concatenate([zero, jnp.where(hi, halves[3], 0.0)], axis=1)]
        else:
            blocks = [jnp.where(lo, q[:, 0:128], 0.0), jnp.where(hi, q[:, 0:128], 0.0),
                      jnp.where(lo, q[:, 128:256], 0.0), jnp.where(hi, q[:, 128:256], 0.0)]
        qs = jnp.concatenate(blocks, axis=0).astype(BF16)
        s_sc[...] = _dot_nt(k_ref[0, pl.ds(window_start(item), win), :], qs)

    def consume(item, s_sc):
        start = window_start(item)
        s = s_sc[...]
        if has_table:
            mb0 = jnp.maximum(prev // 128 - item, 0)
            s = s + jnp.concatenate([e_ref[mb0 + jb] for jb in range(win // 128)], axis=0)
        m = jnp.max(s, axis=0, keepdims=True)
        if has_sink:
            m = jnp.maximum(m, sink)
        p = jnp.exp2(s - m)
        l = jnp.sum(p, axis=0, keepdims=True)
        if has_sink:
            l = l + jnp.exp2(sink - m)
        inv = 1.0 / l
        pb = p.astype(BF16)
        v = v_ref[0, pl.ds(start, win), :]
        parts = []
        for pr in range(2):
            vp = v[:, 128 * pr:128 * (pr + 1)] if dk == 256 else v
            ot = _dot_tn(vp, pb[:, 256 * pr:256 * (pr + 1)])
            for e in range(2):
                h = 2 * pr + e
                parts.append(ot[64 * e:64 * (e + 1), 128 * e:128 * (e + 1)] * inv[:, TQ * h:TQ * (h + 1)])
        o = jnp.concatenate(parts, axis=0).T
        rows = pl.ds(pl.multiple_of(item * TQ, TQ), TQ)
        o_ref[0, rows, :] = (o * g_ref[0, rows, :].astype(F32)).astype(BF16)

    scores_into(0, sa_sc)

    def body(i, carry):
        scores_into(2 * i + 1, sb_sc)
        consume(2 * i, sa_sc)
        scores_into(jnp.minimum(2 * i + 2, n_items - 1), sa_sc)
        consume(2 * i + 1, sb_sc)
        return carry

    lax.fori_loop(0, n_items // 2, body, 0)


def _window_attn(q_src, q_col, k_src, k_col, v_src, v_col, g_col, *, win, prev, dk, table=None, sinks=None):
    b, s, _ = q_src.shape
    skv = k_src.shape[1]
    assert s % (2 * TQ) == 0 and skv >= win
    kern = functools.partial(_window_attn_kernel, win=win, prev=prev, dk=dk,
                             has_table=table is not None, has_sink=sinks is not None)
    in_specs, args = [], []
    if sinks is not None:
        in_specs.append(pl.BlockSpec(memory_space=pltpu.SMEM)); args.append(sinks)
    in_specs += [pl.BlockSpec((1, s, 256), lambda i: (i, 0, q_col // 256)),
                 pl.BlockSpec((1, skv, dk), lambda i: (i, 0, k_col // dk)),
                 pl.BlockSpec((1, skv, dk), lambda i: (i, 0, v_col // dk)),
                 pl.BlockSpec((1, s, 256), lambda i: (i, 0, g_col // 256))]
    args += [q_src, k_src, v_src, q_src]
    if table is not None:
        in_specs.append(pl.BlockSpec(table.shape, lambda i: (0, 0, 0))); args.append(table)
    return pl.pallas_call(
        kern,
        grid=(b,),
        in_specs=in_specs,
        out_specs=pl.BlockSpec((1, s, 256), lambda i: (i, 0, 0)),
        out_shape=jax.ShapeDtypeStruct((b, s, GROUP), BF16),
        scratch_shapes=[pltpu.VMEM((win, N_HEADS * TQ), F32), pltpu.VMEM((win, N_HEADS * TQ), F32)],
        compiler_params=pltpu.CompilerParams(
            dimension_semantics=("parallel",), vmem_limit_bytes=VMEM_LIMIT),
        name="attn_win%d" % win,
    )(*args)


def _attn_b_kernel(q0_ref, q1_ref, k0_ref, k1_ref, v_ref, g_ref, mask_ref, o_ref,
                   qs_sc, sa_sc, sb_sc, m_sc, l_sc, acc_sc):
    t = pl.program_id(1)
    n_tiles = B_TQ // TQ
    lane = lax.broadcasted_iota(jnp.int32, (TQ, GROUP), 1)
    slot = [(lane < 64) | ((lane >= 128) & (lane < 160)),
            ((lane >= 64) & (lane < 128)) | ((lane >= 160) & (lane < 192))]
    q_refs = (q0_ref, q1_ref)
    k_refs = (k0_ref, k1_ref)
    for c in range(n_tiles):
        for pr in range(2):
            qs_sc[2 * c + pr] = _stack_masked(q_refs[pr][0, TQ * c:TQ * (c + 1), :], slot)
    m_sc[...] = jnp.full(m_sc.shape, NEG_INF, F32)
    l_sc[...] = jnp.zeros(l_sc.shape, F32)
    acc_sc[...] = jnp.zeros(acc_sc.shape, F32)
    n_items = (t + 1) * n_tiles

    def scores_into(item, s_sc):
        kb, c = item // n_tiles, item % n_tiles
        start = pl.multiple_of(kb * B_TK, B_TK)
        for pr in range(2):
            s_sc[pr] = _dot_nt(k_refs[pr][0, pl.ds(start, B_TK), :], qs_sc[2 * c + pr])

    def consume(item, s_sc):
        kb, c = item // n_tiles, item % n_tiles
        start = pl.multiple_of(kb * B_TK, B_TK)
        diag = (kb == t).astype(jnp.int32)
        mask = jnp.concatenate([mask_ref[diag, c]] * 2, axis=1)
        v = v_ref[0, pl.ds(start, B_TK), :]
        for pr in range(2):
            u = 2 * c + pr
            s = s_sc[pr] + mask
            m_prev = m_sc[u]
            m_new = jnp.maximum(m_prev, jnp.max(s, axis=0, keepdims=True))
            alpha = jnp.exp2(m_prev - m_new)
            p = jnp.exp2(s - m_new)
            l_sc[u] = alpha * l_sc[u] + jnp.sum(p, axis=0, keepdims=True)
            m_sc[u] = m_new
            acc_sc[u] = alpha * acc_sc[u] + _dot_tn(v[:, 128 * pr:128 * (pr + 1)], p.astype(BF16))

    scores_into(0, sa_sc)

    def body(i, carry):
        scores_into(2 * i + 1, sb_sc)
        consume(2 * i, sa_sc)
        scores_into(jnp.minimum(2 * i + 2, n_items - 1), sa_sc)
        consume(2 * i + 1, sb_sc)
        return carry

    lax.fori_loop(0, n_items // 2, body, 0)

    for c in range(n_tiles):
        parts = []
        for pr in range(2):
            u = 2 * c + pr
            inv = 1.0 / l_sc[u]
            for e in range(2):
                parts.append(acc_sc[u, 64 * e:64 * (e + 1), 128 * e:128 * (e + 1)] * inv[:, 128 * e:128 * (e + 1)])
        o = jnp.concatenate(parts, axis=0).T
        o_ref[0, TQ * c:TQ * (c + 1), :] = (o * g_ref[0, TQ * c:TQ * (c + 1), :].astype(F32)).astype(BF16)


def _attn_b(p3, mask):
    b, s, _ = p3.shape
    n_units = 2 * (B_TQ // TQ)
    return pl.pallas_call(
        _attn_b_kernel,
        grid=(b, s // B_TQ),
        in_specs=[
            pl.BlockSpec((1, B_TQ, 256), lambda i, t: (i, t, P_BQ0 // 256)),
            pl.BlockSpec((1, B_TQ, 256), lambda i, t: (i, t, P_BQ1 // 256)),
            pl.BlockSpec((1, s, 256), lambda i, t: (i, 0, P_BK0 // 256)),
            pl.BlockSpec((1, s, 256), lambda i, t: (i, 0, P_BK1 // 256)),
            pl.BlockSpec((1, s, 256), lambda i, t: (i, 0, P_BV // 256)),
            pl.BlockSpec((1, B_TQ, 256), lambda i, t: (i, t, P_GB // 256)),
            pl.BlockSpec((2, B_TQ // TQ, B_TK, TQ), lambda i, t: (0, 0, 0, 0)),
        ],
        out_specs=pl.BlockSpec((1, B_TQ, 256), lambda i, t: (i, t, 0)),
        out_shape=jax.ShapeDtypeStruct((b, s, GROUP), BF16),
        scratch_shapes=[pltpu.VMEM((n_units, 2 * TQ, 256), BF16),
                        pltpu.VMEM((2, B_TK, 2 * TQ), F32),
                        pltpu.VMEM((2, B_TK, 2 * TQ), F32),
                        pltpu.VMEM((n_units, 1, 2 * TQ), F32),
                        pltpu.VMEM((n_units, 1, 2 * TQ), F32),
                        pltpu.VMEM((n_units, 128, 2 * TQ), F32)],
        compiler_params=pltpu.CompilerParams(
            dimension_semantics=("parallel", "arbitrary"), vmem_limit_bytes=VMEM_LIMIT),
        name="attn_b",
    )(p3, p3, p3, p3, p3, p3, mask)


def _outproj_kernel(ya_ref, yb_ref, yc_ref, ym_ref, x_ref, w_ref, g_ref, b_ref, o_ref):
    y = (_dot(ya_ref[...], w_ref[0:256, :]) + _dot(yb_ref[...], w_ref[256:512, :])
         + _dot(yc_ref[...], w_ref[512:768, :]) + _dot(ym_ref[...], w_ref[768:1024, :]))
    z = ALPHA * x_ref[...] + y
    mu = jnp.mean(z, axis=-1, keepdims=True)
    zc = z - mu
    var = jnp.mean(zc * zc, axis=-1, keepdims=True)
    o_ref[...] = zc * lax.rsqrt(var + 1e-5) * g_ref[...] + b_ref[...]


def _outproj(ya, yb, yc, ym, x2d, w, g, bias):
    n = x2d.shape[0]
    ytile = pl.BlockSpec((TM, GROUP), lambda i: (i, 0))
    const = lambda shape: pl.BlockSpec(shape, lambda i: (0,) * len(shape))
    return pl.pallas_call(
        _outproj_kernel,
        grid=(n // TM,),
        in_specs=[ytile, ytile, ytile, ytile,
                  pl.BlockSpec((TM, D_MODEL), lambda i: (i, 0)),
                  const((D_MODEL, D_MODEL)), const((1, D_MODEL)), const((1, D_MODEL))],
        out_specs=pl.BlockSpec((TM, D_MODEL), lambda i: (i, 0)),
        out_shape=jax.ShapeDtypeStruct((n, D_MODEL), F32),
        compiler_params=pltpu.CompilerParams(
            dimension_semantics=("parallel",), vmem_limit_bytes=VMEM_LIMIT),
        name="outproj",
    )(ya, yb, yc, ym, x2d, w, g, bias)


def kernel(x, mem, positions, w_in, rel_bias, mla_q_norm, w_uq, mla_kv_norm, w_ukv,
           swa_sinks, w_mem_kv, w_out, ln_gain, ln_bias):
    b, s, d = x.shape
    depth = w_in.shape[0]
    assert d == D_MODEL and depth == DEPTH and s % B_TK == 0 and s >= A_WIN and (b * s) % TM == 0

    cols, cperm = _inproj_cols()
    w_in_p = _take_cols(w_in, cols).astype(BF16)
    wuq_p = jnp.pad(_take_cols(w_uq, _uq_cols()), ((0, 0), (0, 256 - MLA_Q_RANK), (0, 0))).astype(BF16)
    wukv_p = _take_cols(w_ukv, _ukv_cols()).astype(BF16)
    gq = jnp.pad(mla_q_norm, ((0, 0), (0, 256 - MLA_Q_RANK)))[:, None, :]
    gkv = mla_kv_norm[:, None, :]
    rows = np.concatenate([np.arange(512), 512 + cperm, np.arange(768, 1024)])
    w_out_p = _take_cols(w_out, rows, axis=1).astype(BF16)
    w_mem_all = jnp.transpose(w_mem_kv, (1, 0, 2)).reshape(D_MODEL, depth * 512).astype(BF16)
    e_a = _bias_table_a(rel_bias)
    e_c = jnp.asarray(_mask_table_c())
    e_b = jnp.asarray(_mask_table_b())
    tab = _rope_tables(positions)

    memkv = _memkv(mem, w_mem_all)
    h = x.reshape(b * s, d)
    for l in range(depth):
        p3 = _inproj(h, tab, w_in_p[l], wuq_p[l], wukv_p[l], gq[l], gkv[l]).reshape(b, s, P_WIDTH)
        ya = _window_attn(p3, P_AQ, p3, P_AK, p3, P_AV, P_GA, win=A_WIN, prev=A_PREV * CHUNK, dk=256, table=e_a[l])
        yb = _attn_b(p3, e_b)
        yc = _window_attn(p3, P_CQ, p3, P_CK, p3, P_CV, P_GC, win=C_WIN, prev=SWA_PREV * CHUNK, dk=128,
                          table=e_c, sinks=swa_sinks[l])
        ym = _window_attn(p3, P_MQ, memkv, 512 * l, memkv, 512 * l + 256, P_GM, win=MEM_LEN, prev=None, dk=256)
        flat = lambda y: y.reshape(b * s, GROUP)
        h = _outproj(flat(ya), flat(yb), flat(yc), flat(ym), h, w_out_p[l],
                     ln_gain[l][None, :], ln_bias[l][None, :])
    return h.reshape(b, s, d)
```

```python
import functools

import numpy as np
import jax
import jax.numpy as jnp
from jax import lax
from jax.experimental import pallas as pl
from jax.experimental.pallas import tpu as pltpu

F32 = jnp.float32
BF16 = jnp.bfloat16

D_MODEL = 1024
DEPTH = 4
CHUNK = 64
HEAD_DIM = 64
GROUP = 256
N_HEADS = 4
ROPE_THETA = 10000.0
NEG_INF = -1e30
A_PREV = 8
REL_CLIP = 128
MLA_NOPE = 64
MLA_ROPE = 32
MLA_Q_RANK = 192
MLA_KV_RANK = 128
SWA_PREV = 2
MEM_LEN = 256
ALPHA = (2.0 * DEPTH) ** 0.25

TQ = 128
A_WIN = TQ + A_PREV * CHUNK
C_WIN = TQ + SWA_PREV * CHUNK
WIN_UNROLL = 4
B_TK = 512
B_TQ = 512
TM = 512
VMEM_LIMIT = 56 * 1024 * 1024
LOG2E = 1.4426950408889634
QSCALE = HEAD_DIM ** -0.5 * LOG2E
B_QSCALE = (MLA_NOPE + MLA_ROPE) ** -0.5 * LOG2E

P_AQ, P_AK, P_AV = 0, 256, 512
P_BQ0, P_BQ1, P_BK0, P_BK1, P_BV = 768, 1024, 1280, 1536, 1792
P_CQ, P_CK, P_CV = 2048, 2304, 2432
P_MQ = 2560
P_GA, P_GB, P_GC, P_GM = 2816, 3072, 3328, 3584
P_WIDTH = 3840

W_A, W_C, W_M, W_G, W_B, W_WIDTH = 0, 768, 1280, 1536, 2560, 3072

C_HEAD_ORDER = (0, 2, 1, 3)


def _inproj_cols():
    r = np.arange
    aq, ak, av, ag = 0, 256, 512, 768
    bcq, bckv, bkr, bg = 1024, 1216, 1344, 1376
    cq, ck, cv, cg = 1632, 1888, 2016, 2144
    mq, mg = 2400, 2656
    cperm = np.concatenate([r(64) + 64 * h for h in C_HEAD_ORDER])
    pad = lambda n: np.full(n, -1)
    cols = np.concatenate([
        aq + r(256), ak + r(256), av + r(256),
        cq + cperm, ck + r(128), cv + r(128),
        mq + r(256),
        ag + r(256), bg + r(256), cg + cperm, mg + r(256),
        bcq + r(192), pad(64), bckv + r(128), bkr + r(32), bkr + r(32), pad(64),
    ])
    assert cols.shape[0] == W_WIDTH
    return cols, cperm


def _take_cols(w, cols, axis=-1):
    axis = axis % w.ndim
    pieces, i = [], 0
    while i < len(cols):
        j = i + 1
        if cols[i] < 0:
            while j < len(cols) and cols[j] < 0:
                j += 1
            shape = w.shape[:axis] + (j - i,) + w.shape[axis + 1:]
            pieces.append(jnp.zeros(shape, w.dtype))
        else:
            while j < len(cols) and cols[j] == cols[j - 1] + 1:
                j += 1
            pieces.append(lax.slice_in_dim(w, int(cols[i]), int(cols[i]) + (j - i), axis=axis))
        i = j
    return jnp.concatenate(pieces, axis=axis)


def _uq_cols():
    r = np.arange
    per = MLA_NOPE + MLA_ROPE
    out = []
    for p in range(2):
        h0, h1 = 2 * p, 2 * p + 1
        out += [per * h0 + r(64), per * h1 + r(64),
                per * h0 + 64 + r(32), per * h1 + 64 + r(32), np.full(64, -1)]
    return np.concatenate(out)


def _ukv_cols():
    r = np.arange
    return np.concatenate([128 * h + r(64) for h in range(4)] + [128 * h + 64 + r(64) for h in range(4)])


def _rope_tables(positions):
    pos = positions.astype(F32).reshape(-1, 1)

    def tab(d):
        inv = ROPE_THETA ** (-jnp.arange(0, d, 2, dtype=F32) / d)
        ang = pos * inv[None, :]
        c, s = jnp.cos(ang), jnp.sin(ang)
        return jnp.tile(c, (1, 256 // d)), jnp.tile(jnp.concatenate([-s, s], axis=1), (1, 128 // d))

    c64, s64 = tab(HEAD_DIM)
    c32, s32 = tab(MLA_ROPE)
    return jnp.concatenate([c64, s64, c32, s32], axis=-1)


def _bias_table_a(rel_bias):
    width, period = 9 * 128, 9 * 128 + TQ
    k = np.arange(period)
    d = np.where(k < width, A_PREV * CHUNK - k, A_PREV * CHUNK + period - k)
    idx = np.clip(d, -REL_CLIP, REL_CLIP) + REL_CLIP
    n_hi = A_PREV * CHUNK - REL_CLIP + 1
    n_lo = width - n_hi - (2 * REL_CLIP - 1)
    expect = np.concatenate([np.full(n_hi, 2 * REL_CLIP), np.arange(2 * REL_CLIP - 1, 0, -1),
                             np.zeros(n_lo, np.int64), np.full(period - width, 2 * REL_CLIP)])
    assert np.array_equal(idx, expect)
    rep = lambda col, n: jnp.broadcast_to(rel_bias[:, :, col:col + 1], rel_bias.shape[:2] + (n,))
    gp = jnp.concatenate([rep(2 * REL_CLIP, n_hi), jnp.flip(rel_bias[:, :, 1:2 * REL_CLIP], axis=-1),
                          rep(0, n_lo), rep(2 * REL_CLIP, period - width)], axis=-1) * LOG2E
    flat = jnp.tile(gp, (1, 1, TQ))[:, :, :TQ * (period - 1)]
    skew = flat.reshape(gp.shape[0], N_HEADS, TQ, period - 1)[..., :width]
    i = np.arange(TQ)[:, None]
    m = np.arange(width)[None, :]
    dchunk = i // CHUNK + A_PREV - m // CHUNK
    valid = (dchunk >= 0) & (dchunk <= A_PREV)
    t = jnp.where(jnp.asarray(valid)[None, None], skew, NEG_INF)
    return jnp.transpose(t, (0, 3, 1, 2)).reshape(gp.shape[0], 9, 128, N_HEADS * TQ)


def _mask_table_c():
    m = np.arange(3 * 128)[:, None]
    i = np.arange(TQ)[None, :]
    dchunk = i // CHUNK + SWA_PREV - m // CHUNK
    valid = (dchunk >= 0) & (dchunk <= SWA_PREV)
    t = np.where(valid, 0.0, NEG_INF).astype(np.float32)
    return np.tile(t, (1, N_HEADS)).reshape(3, 128, N_HEADS * TQ)


def _dot(a, b):
    return jnp.dot(a, b, preferred_element_type=F32)


def _dot_nt(a, b):
    return lax.dot_general(a, b, (((1,), (1,)), ((), ())), preferred_element_type=F32)


def _dot_tn(a, b):
    return lax.dot_general(a, b, (((0,), (0,)), ((), ())), preferred_element_type=F32)


def _rope(x, cos, sin_signed, half):
    lane = lax.broadcasted_iota(jnp.int32, x.shape, 1)
    first = (lane & (2 * half - 1)) < half
    swapped = jnp.where(first, pltpu.roll(x, 128 - half, 1), pltpu.roll(x, half, 1))
    return x * cos + swapped * sin_signed


def _inproj_kernel(x_ref, tab_ref, w_ref, wuq_ref, wukv_ref, gq_ref, gkv_ref, p_ref):
    xb = x_ref[...].astype(BF16)
    cos64, sin64 = tab_ref[:, 0:128], tab_ref[:, 128:256]
    cos32, sin32 = tab_ref[:, 256:384], tab_ref[:, 384:512]

    def mm(lo, hi):
        return _dot(xb, w_ref[:, lo:hi])

    r = mm(W_A, W_A + 768)
    p_ref[:, P_AQ:P_AQ + 256] = (r[:, 0:256] * QSCALE).astype(BF16)
    p_ref[:, P_AK:P_AK + 512] = r[:, 256:768].astype(BF16)

    r = mm(W_C, W_C + 512)
    for j in range(2):
        qj = _rope(r[:, 128 * j:128 * (j + 1)], cos64, sin64, 32)
        p_ref[:, P_CQ + 128 * j:P_CQ + 128 * (j + 1)] = (qj * QSCALE).astype(BF16)
    p_ref[:, P_CK:P_CK + 128] = _rope(r[:, 256:384], cos64, sin64, 32).astype(BF16)
    p_ref[:, P_CV:P_CV + 128] = r[:, 384:512].astype(BF16)

    r = mm(W_M, W_M + 256)
    p_ref[:, P_MQ:P_MQ + 256] = (r * QSCALE).astype(BF16)

    r = mm(W_G, W_G + 1024)
    p_ref[:, P_GA:P_GA + 1024] = (r * (1.0 / (1.0 + jnp.exp(-r)))).astype(BF16)

    r = mm(W_B, W_B + 512)
    cq = r[:, 0:256]
    ms = jnp.sum(cq * cq, axis=-1, keepdims=True) * (1.0 / MLA_Q_RANK)
    qn = (cq * lax.rsqrt(ms + 1e-6) * gq_ref[...]).astype(BF16)
    q = _dot(qn, wuq_ref[...]) * B_QSCALE
    for p in range(2):
        base = P_BQ0 + 256 * p
        p_ref[:, base:base + 128] = q[:, 256 * p:256 * p + 128].astype(BF16)
        p_ref[:, base + 128:base + 256] = _rope(q[:, 256 * p + 128:256 * p + 256], cos32, sin32, 16).astype(BF16)

    ckv = r[:, 256:384]
    ms = jnp.mean(ckv * ckv, axis=-1, keepdims=True)
    kvn = (ckv * lax.rsqrt(ms + 1e-6) * gkv_ref[...]).astype(BF16)
    kv = _dot(kvn, wukv_ref[...])
    krb = _rope(r[:, 384:512], cos32, sin32, 16).astype(BF16)
    for p in range(2):
        base = P_BK0 + 256 * p
        p_ref[:, base:base + 128] = kv[:, 128 * p:128 * (p + 1)].astype(BF16)
        p_ref[:, base + 128:base + 256] = krb
    p_ref[:, P_BV:P_BV + 256] = kv[:, 256:512].astype(BF16)


def _inproj(x2d, tab, w, wuq, wukv, gq, gkv):
    n = x2d.shape[0]
    const = lambda shape: pl.BlockSpec(shape, lambda i: (0,) * len(shape))
    return pl.pallas_call(
        _inproj_kernel,
        grid=(n // TM,),
        in_specs=[
            pl.BlockSpec((TM, D_MODEL), lambda i: (i, 0)),
            pl.BlockSpec((TM, 512), lambda i: (i, 0)),
            const((D_MODEL, W_WIDTH)),
            const((256, 512)),
            const((128, 512)),
            const((1, 256)),
            const((1, 128)),
        ],
        out_specs=pl.BlockSpec((TM, P_WIDTH), lambda i: (i, 0)),
        out_shape=jax.ShapeDtypeStruct((n, P_WIDTH), BF16),
        compiler_params=pltpu.CompilerParams(
            dimension_semantics=("parallel",), vmem_limit_bytes=VMEM_LIMIT),
        name="inproj",
    )(x2d, tab, w, wuq, wukv, gq, gkv)


def _memkv_kernel(mem_ref, w_ref, o_ref):
    o_ref[0] = _dot(mem_ref[0].astype(BF16), w_ref[...]).astype(BF16)


def _memkv(mem, w_all):
    b = mem.shape[0]
    n = w_all.shape[1]
    return pl.pallas_call(
        _memkv_kernel,
        grid=(b,),
        in_specs=[pl.BlockSpec((1, MEM_LEN, D_MODEL), lambda i: (i, 0, 0)),
                  pl.BlockSpec((D_MODEL, n), lambda i: (0, 0))],
        out_specs=pl.BlockSpec((1, MEM_LEN, n), lambda i: (i, 0, 0)),
        out_shape=jax.ShapeDtypeStruct((b, MEM_LEN, n), BF16),
        compiler_params=pltpu.CompilerParams(
            dimension_semantics=("parallel",), vmem_limit_bytes=VMEM_LIMIT),
        name="memkv",
    )(mem, w_all)


def _window_attn_kernel(*refs, win, prev, dk, has_table, has_sink):
    refs = list(refs)
    sink_ref = refs.pop(0) if has_sink else None
    q_ref, k_ref, v_ref, g_ref = refs[:4]
    e_ref = refs[4] if has_table else None
    o_ref, sa_sc, sb_sc = refs[-3:]
    n_items = q_ref.shape[1] // TQ
    lanes = N_HEADS * TQ

    lane128 = lax.broadcasted_iota(jnp.int32, (TQ, 128), 1)
    lo, hi = lane128 < HEAD_DIM, lane128 >= HEAD_DIM
    if has_sink:
        col = lax.broadcasted_iota(jnp.int32, (1, lanes), 1)
        order = C_HEAD_ORDER if dk == 128 else tuple(range(N_HEADS))
        sink = jnp.where(col < TQ, sink_ref[order[0]],
                         jnp.where(col < 2 * TQ, sink_ref[order[1]],
                                   jnp.where(col < 3 * TQ, sink_ref[order[2]], sink_ref[order[3]]))) * LOG2E

    def window_start(item):
        if prev is None:
            return 0
        return pl.multiple_of(jnp.maximum(item * TQ - prev, 0), 128)

    def scores_into(item, s_sc):
        q = q_ref[0, pl.ds(pl.multiple_of(item * TQ, TQ), TQ), :].astype(F32)
        if dk == 256:
            halves = [q[:, 0:128], q[:, 0:128], q[:, 128:256], q[:, 128:256]]
            zero = jnp.zeros((TQ, 128), F32)
            blocks = [jnp.concatenate([jnp.where(lo, halves[0], 0.0), zero], axis=1),
                      jnp.concatenate([jnp.where(hi, halves[1], 0.0), zero], axis=1),
                      jnp.concatenate([zero, jnp.where(lo, halves[2], 0.0)], axis=1),
                      jnp.concatenate([zero, jnp.where(hi, halves[3], 0.0)], axis=1)]
        else:
            blocks = [jnp.where(lo, q[:, 0:128], 0.0), jnp.where(hi, q[:, 0:128], 0.0),
                      jnp.where(lo, q[:, 128:256], 0.0), jnp.where(hi, q[:, 128:256], 0.0)]
        qs = jnp.concatenate(blocks, axis=0).astype(BF16)
        s_sc[...] = _dot_nt(k_ref[0, pl.ds(window_start(item), win), :], qs)

    def consume(item, s_sc):
        start = window_start(item)
        s = s_sc[...]
        if has_table:
            mb0 = jnp.maximum(prev // 128 - item, 0)
            s = s + jnp.concatenate([e_ref[mb0 + jb] for jb in range(win // 128)], axis=0)
        m = jnp.max(s, axis=0, keepdims=True)
        if has_sink:
            m = jnp.maximum(m, sink)
        p = jnp.exp2(s - m)
        l = jnp.sum(p, axis=0, keepdims=True)
        if has_sink:
            l = l + jnp.exp2(sink - m)
        inv = 1.0 / l
        pb = p.astype(BF16)
        v = v_ref[0, pl.ds(start, win), :]
        parts = []
        for pr in range(2):
            vp = v[:, 128 * pr:128 * (pr + 1)] if dk == 256 else v
            ot = _dot_tn(vp, pb[:, 256 * pr:256 * (pr + 1)])
            for e in range(2):
                h = 2 * pr + e
                parts.append(ot[64 * e:64 * (e + 1), 128 * e:128 * (e + 1)] * inv[:, TQ * h:TQ * (h + 1)])
        o = jnp.concatenate(parts, axis=0).T
        rows = pl.ds(pl.multiple_of(item * TQ, TQ), TQ)
        o_ref[0, rows, :] = (o * g_ref[0, rows, :].astype(F32)).astype(BF16)

    bufs = (sa_sc, sb_sc)
    scores_into(0, bufs[0])

    def body(i, carry):
        for j in range(WIN_UNROLL):
            item = WIN_UNROLL * i + j
            scores_into(jnp.minimum(item + 1, n_items - 1), bufs[(j + 1) % 2])
            consume(item, bufs[j % 2])
        return carry

    lax.fori_loop(0, n_items // WIN_UNROLL, body, 0)


def _window_attn(q_src, q_col, k_src, k_col, v_src, v_col, g_col, *, win, prev, dk, table=None, sinks=None):
    b, s, _ = q_src.shape
    skv = k_src.shape[1]
    assert s % (WIN_UNROLL * TQ) == 0 and skv >= win
    kern = functools.partial(_window_attn_kernel, win=win, prev=prev, dk=dk,
                             has_table=table is not None, has_sink=sinks is not None)
    in_specs, args = [], []
    if sinks is not None:
        in_specs.append(pl.BlockSpec(memory_space=pltpu.SMEM)); args.append(sinks)
    in_specs += [pl.BlockSpec((1, s, 256), lambda i: (i, 0, q_col // 256)),
                 pl.BlockSpec((1, skv, dk), lambda i: (i, 0, k_col // dk)),
                 pl.BlockSpec((1, skv, dk), lambda i: (i, 0, v_col // dk)),
                 pl.BlockSpec((1, s, 256), lambda i: (i, 0, g_col // 256))]
    args += [q_src, k_src, v_src, q_src]
    if table is not None:
        in_specs.append(pl.BlockSpec(table.shape, lambda i: (0, 0, 0))); args.append(table)
    return pl.pallas_call(
        kern,
        grid=(b,),
        in_specs=in_specs,
        out_specs=pl.BlockSpec((1, s, 256), lambda i: (i, 0, 0)),
        out_shape=jax.ShapeDtypeStruct((b, s, GROUP), BF16),
        scratch_shapes=[pltpu.VMEM((win, N_HEADS * TQ), F32), pltpu.VMEM((win, N_HEADS * TQ), F32)],
        compiler_params=pltpu.CompilerParams(
            dimension_semantics=("parallel",), vmem_limit_bytes=VMEM_LIMIT),
        name="attn_win%d" % win,
    )(*args)


def _attn_b_kernel(q0_ref, q1_ref, k0_ref, k1_ref, v_ref, g_ref, o_ref,
                   qs_sc, sa_sc, sb_sc, m_sc, l_sc, acc_sc):
    t = pl.program_id(1)
    n_tiles = B_TQ // TQ
    lane = lax.broadcasted_iota(jnp.int32, (TQ, 128), 1)
    q_refs = (q0_ref, q1_ref)
    k_refs = (k0_ref, k1_ref)

    def stack_queries(c):
        stacked = []
        for pr in range(2):
            q = q_refs[pr][0, TQ * c:TQ * (c + 1), :].astype(F32)
            nope, rope = q[:, 0:128], q[:, 128:256]
            head_a = jnp.concatenate([jnp.where(lane < 64, nope, 0.0), jnp.where(lane < 32, rope, 0.0)], axis=1)
            head_b = jnp.concatenate([jnp.where(lane >= 64, nope, 0.0), jnp.where(lane >= 32, rope, 0.0)], axis=1)
            qs = jnp.concatenate([head_a, head_b], axis=0).astype(BF16)
            qs_sc[2 * c + pr] = qs
            stacked.append(qs)
        return stacked

    lane2 = lax.broadcasted_iota(jnp.int32, (1, 2 * TQ), 1)
    hide_first_chunk = jnp.where((lane2 & (TQ - 1)) < CHUNK, NEG_INF, 0.0)

    def scores_into(kb, c, s_sc, nk=B_TK, qs=None):
        start = pl.multiple_of(kb * B_TK, B_TK)
        for pr in range(2):
            q = qs_sc[2 * c + pr] if qs is None else qs[pr]
            s_sc[pr, 0:nk, :] = _dot_nt(k_refs[pr][0, pl.ds(start, nk), :], q)

    def consume(kb, c, s_sc, nk=B_TK, diagonal=False):
        start = pl.multiple_of(kb * B_TK, B_TK)
        v = v_ref[0, pl.ds(start, nk), :]
        for pr in range(2):
            u = 2 * c + pr
            s = s_sc[pr, 0:nk, :]
            if diagonal:
                s = jnp.concatenate([s[:nk - CHUNK], s[nk - CHUNK:] + hide_first_chunk], axis=0)
            m_prev = m_sc[u]
            m_new = jnp.maximum(m_prev, jnp.max(s, axis=0, keepdims=True))
            alpha = jnp.exp2(m_prev - m_new)
            p = jnp.exp2(s - m_new)
            l_sc[u] = alpha * l_sc[u] + jnp.sum(p, axis=0, keepdims=True)
            m_sc[u] = m_new
            acc_sc[u] = alpha * acc_sc[u] + _dot_tn(v[:, 128 * pr:128 * (pr + 1)], p.astype(BF16))

    bufs = (sa_sc, sb_sc)
    scores_into(0, 0, bufs[0], qs=stack_queries(0))
    for c in range(1, n_tiles):
        stack_queries(c)
    m_sc[...] = jnp.full(m_sc.shape, NEG_INF, F32)
    l_sc[...] = jnp.zeros(l_sc.shape, F32)
    acc_sc[...] = jnp.zeros(acc_sc.shape, F32)

    def body(kb, carry):
        for c in range(n_tiles):
            if c + 1 < n_tiles:
                scores_into(kb, c + 1, bufs[(c + 1) % 2])
            else:
                scores_into(kb + 1, 0, bufs[0])
            consume(kb, c, bufs[c % 2])
        return carry

    lax.fori_loop(0, t, body, 0)

    for c in range(n_tiles):
        if c + 1 < n_tiles:
            scores_into(t, c + 1, bufs[(c + 1) % 2], nk=TQ * (c + 2))
        consume(t, c, bufs[c % 2], nk=TQ * (c + 1), diagonal=True)

    for c in range(n_tiles):
        parts = []
        for pr in range(2):
            u = 2 * c + pr
            inv = 1.0 / l_sc[u]
            for e in range(2):
                parts.append(acc_sc[u, 64 * e:64 * (e + 1), 128 * e:128 * (e + 1)] * inv[:, 128 * e:128 * (e + 1)])
        o = jnp.concatenate(parts, axis=0).T
        o_ref[0, TQ * c:TQ * (c + 1), :] = (o * g_ref[0, TQ * c:TQ * (c + 1), :].astype(F32)).astype(BF16)


def _attn_b(p3):
    b, s, _ = p3.shape
    n_units = 2 * (B_TQ // TQ)
    return pl.pallas_call(
        _attn_b_kernel,
        grid=(b, s // B_TQ),
        in_specs=[
            pl.BlockSpec((1, B_TQ, 256), lambda i, t: (i, t, P_BQ0 // 256)),
            pl.BlockSpec((1, B_TQ, 256), lambda i, t: (i, t, P_BQ1 // 256)),
            pl.BlockSpec((1, s, 256), lambda i, t: (i, 0, P_BK0 // 256)),
            pl.BlockSpec((1, s, 256), lambda i, t: (i, 0, P_BK1 // 256)),
            pl.BlockSpec((1, s, 256), lambda i, t: (i, 0, P_BV // 256)),
            pl.BlockSpec((1, B_TQ, 256), lambda i, t: (i, t, P_GB // 256)),
        ],
        out_specs=pl.BlockSpec((1, B_TQ, 256), lambda i, t: (i, t, 0)),
        out_shape=jax.ShapeDtypeStruct((b, s, GROUP), BF16),
        scratch_shapes=[pltpu.VMEM((n_units, 2 * TQ, 256), BF16),
                        pltpu.VMEM((2, B_TK, 2 * TQ), F32),
                        pltpu.VMEM((2, B_TK, 2 * TQ), F32),
                        pltpu.VMEM((n_units, 1, 2 * TQ), F32),
                        pltpu.VMEM((n_units, 1, 2 * TQ), F32),
                        pltpu.VMEM((n_units, 128, 2 * TQ), F32)],
        compiler_params=pltpu.CompilerParams(
            dimension_semantics=("parallel", "arbitrary"), vmem_limit_bytes=VMEM_LIMIT),
        name="attn_b",
    )(p3, p3, p3, p3, p3, p3)


def _outproj_kernel(ya_ref, yb_ref, yc_ref, ym_ref, x_ref, w_ref, g_ref, b_ref, o_ref):
    y = (_dot(ya_ref[...], w_ref[0:256, :]) + _dot(yb_ref[...], w_ref[256:512, :])
         + _dot(yc_ref[...], w_ref[512:768, :]) + _dot(ym_ref[...], w_ref[768:1024, :]))
    z = ALPHA * x_ref[...] + y
    mu = jnp.mean(z, axis=-1, keepdims=True)
    zc = z - mu
    var = jnp.mean(zc * zc, axis=-1, keepdims=True)
    o_ref[...] = zc * lax.rsqrt(var + 1e-5) * g_ref[...] + b_ref[...]


def _outproj(ya, yb, yc, ym, x2d, w, g, bias):
    n = x2d.shape[0]
    ytile = pl.BlockSpec((TM, GROUP), lambda i: (i, 0))
    const = lambda shape: pl.BlockSpec(shape, lambda i: (0,) * len(shape))
    return pl.pallas_call(
        _outproj_kernel,
        grid=(n // TM,),
        in_specs=[ytile, ytile, ytile, ytile,
                  pl.BlockSpec((TM, D_MODEL), lambda i: (i, 0)),
                  const((D_MODEL, D_MODEL)), const((1, D_MODEL)), const((1, D_MODEL))],
        out_specs=pl.BlockSpec((TM, D_MODEL), lambda i: (i, 0)),
        out_shape=jax.ShapeDtypeStruct((n, D_MODEL), F32),
        compiler_params=pltpu.CompilerParams(
            dimension_semantics=("parallel",), vmem_limit_bytes=VMEM_LIMIT),
        name="outproj",
    )(ya, yb, yc, ym, x2d, w, g, bias)


def kernel(x, mem, positions, w_in, rel_bias, mla_q_norm, w_uq, mla_kv_norm, w_ukv,
           swa_sinks, w_mem_kv, w_out, ln_gain, ln_bias):
    b, s, d = x.shape
    depth = w_in.shape[0]
    assert d == D_MODEL and depth == DEPTH and s % B_TK == 0 and s >= A_WIN and (b * s) % TM == 0

    cols, cperm = _inproj_cols()
    w_in_p = _take_cols(w_in, cols).astype(BF16)
    wuq_p = jnp.pad(_take_cols(w_uq, _uq_cols()), ((0, 0), (0, 256 - MLA_Q_RANK), (0, 0))).astype(BF16)
    wukv_p = _take_cols(w_ukv, _ukv_cols()).astype(BF16)
    gq = jnp.pad(mla_q_norm, ((0, 0), (0, 256 - MLA_Q_RANK)))[:, None, :]
    gkv = mla_kv_norm[:, None, :]
    rows = np.concatenate([np.arange(512), 512 + cperm, np.arange(768, 1024)])
    w_out_p = _take_cols(w_out, rows, axis=1).astype(BF16)
    w_mem_all = jnp.transpose(w_mem_kv, (1, 0, 2)).reshape(D_MODEL, depth * 512).astype(BF16)
    e_a = _bias_table_a(rel_bias)
    e_c = jnp.asarray(_mask_table_c())
    tab = _rope_tables(positions)

    memkv = _memkv(mem, w_mem_all)
    h = x.reshape(b * s, d)
    for l in range(depth):
        p3 = _inproj(h, tab, w_in_p[l], wuq_p[l], wukv_p[l], gq[l], gkv[l]).reshape(b, s, P_WIDTH)
        ya = _window_attn(p3, P_AQ, p3, P_AK, p3, P_AV, P_GA, win=A_WIN, prev=A_PREV * CHUNK, dk=256, table=e_a[l])
        yb = _attn_b(p3)
        yc = _window_attn(p3, P_CQ, p3, P_CK, p3, P_CV, P_GC, win=C_WIN, prev=SWA_PREV * CHUNK, dk=128,
                          table=e_c, sinks=swa_sinks[l])
        ym = _window_attn(p3, P_MQ, memkv, 512 * l, memkv, 512 * l + 256, P_GM, win=MEM_LEN, prev=None, dk=256)
        flat = lambda y: y.reshape(b * s, GROUP)
        h = _outproj(flat(ya), flat(yb), flat(yc), flat(ym), h, w_out_p[l],
                     ln_gain[l][None, :], ln_bias[l][None, :])
    return h.reshape(b, s, d)
```

```python
import functools

import numpy as np
import jax
import jax.numpy as jnp
from jax import lax
from jax.experimental import pallas as pl
from jax.experimental.pallas import tpu as pltpu

F32 = jnp.float32
BF16 = jnp.bfloat16

D_MODEL = 1024
DEPTH = 4
CHUNK = 64
HEAD_DIM = 64
GROUP = 256
N_HEADS = 4
ROPE_THETA = 10000.0
NEG_INF = -1e30
A_PREV = 8
REL_CLIP = 128
MLA_NOPE = 64
MLA_ROPE = 32
MLA_Q_RANK = 192
MLA_KV_RANK = 128
SWA_PREV = 2
MEM_LEN = 256
ALPHA = (2.0 * DEPTH) ** 0.25

TQ = 128
A_WIN = TQ + A_PREV * CHUNK
C_WIN = TQ + SWA_PREV * CHUNK
WIN_UNROLL = 4
B_TK = 512
B_TQ = 512
TM = 512
VMEM_LIMIT = 56 * 1024 * 1024
LOG2E = 1.4426950408889634
QSCALE = HEAD_DIM ** -0.5 * LOG2E
B_QSCALE = (MLA_NOPE + MLA_ROPE) ** -0.5 * LOG2E

P_AQ, P_AK, P_AV = 0, 256, 512
P_BQ0, P_BQ1, P_BK0, P_BK1, P_BV = 768, 1024, 1280, 1536, 1792
P_CQ, P_CK, P_CV = 2048, 2304, 2432
P_MQ = 2560
P_GA, P_GB, P_GC, P_GM = 2816, 3072, 3328, 3584
P_WIDTH = 3840

W_A, W_C, W_M, W_G, W_B, W_WIDTH = 0, 768, 1280, 1536, 2560, 3072

C_HEAD_ORDER = (0, 2, 1, 3)


def _inproj_cols():
    r = np.arange
    aq, ak, av, ag = 0, 256, 512, 768
    bcq, bckv, bkr, bg = 1024, 1216, 1344, 1376
    cq, ck, cv, cg = 1632, 1888, 2016, 2144
    mq, mg = 2400, 2656
    cperm = np.concatenate([r(64) + 64 * h for h in C_HEAD_ORDER])
    pad = lambda n: np.full(n, -1)
    cols = np.concatenate([
        aq + r(256), ak + r(256), av + r(256),
        cq + cperm, ck + r(128), cv + r(128),
        mq + r(256),
        ag + r(256), bg + r(256), cg + cperm, mg + r(256),
        bcq + r(192), pad(64), bckv + r(128), bkr + r(32), bkr + r(32), pad(64),
    ])
    assert cols.shape[0] == W_WIDTH
    return cols, cperm


def _take_cols(w, cols, axis=-1):
    axis = axis % w.ndim
    pieces, i = [], 0
    while i < len(cols):
        j = i + 1
        if cols[i] < 0:
            while j < len(cols) and cols[j] < 0:
                j += 1
            shape = w.shape[:axis] + (j - i,) + w.shape[axis + 1:]
            pieces.append(jnp.zeros(shape, w.dtype))
        else:
            while j < len(cols) and cols[j] == cols[j - 1] + 1:
                j += 1
            pieces.append(lax.slice_in_dim(w, int(cols[i]), int(cols[i]) + (j - i), axis=axis))
        i = j
    return jnp.concatenate(pieces, axis=axis)


def _uq_cols():
    r = np.arange
    per = MLA_NOPE + MLA_ROPE
    out = []
    for p in range(2):
        h0, h1 = 2 * p, 2 * p + 1
        out += [per * h0 + r(64), per * h1 + r(64),
                per * h0 + 64 + r(32), per * h1 + 64 + r(32), np.full(64, -1)]
    return np.concatenate(out)


def _ukv_cols():
    r = np.arange
    return np.concatenate([128 * h + r(64) for h in range(4)] + [128 * h + 64 + r(64) for h in range(4)])


def _rope_tables(positions):
    pos = positions.astype(F32).reshape(-1, 1)

    def tab(d):
        inv = ROPE_THETA ** (-jnp.arange(0, d, 2, dtype=F32) / d)
        ang = pos * inv[None, :]
        c, s = jnp.cos(ang), jnp.sin(ang)
        return jnp.tile(c, (1, 256 // d)), jnp.tile(jnp.concatenate([-s, s], axis=1), (1, 128 // d))

    c64, s64 = tab(HEAD_DIM)
    c32, s32 = tab(MLA_ROPE)
    return jnp.concatenate([c64, s64, c32, s32], axis=-1)


def _bias_table_a(rel_bias):
    width, period = 9 * 128, 9 * 128 + TQ
    k = np.arange(period)
    d = np.where(k < width, A_PREV * CHUNK - k, A_PREV * CHUNK + period - k)
    idx = np.clip(d, -REL_CLIP, REL_CLIP) + REL_CLIP
    n_hi = A_PREV * CHUNK - REL_CLIP + 1
    n_lo = width - n_hi - (2 * REL_CLIP - 1)
    expect = np.concatenate([np.full(n_hi, 2 * REL_CLIP), np.arange(2 * REL_CLIP - 1, 0, -1),
                             np.zeros(n_lo, np.int64), np.full(period - width, 2 * REL_CLIP)])
    assert np.array_equal(idx, expect)
    rep = lambda col, n: jnp.broadcast_to(rel_bias[:, :, col:col + 1], rel_bias.shape[:2] + (n,))
    gp = jnp.concatenate([rep(2 * REL_CLIP, n_hi), jnp.flip(rel_bias[:, :, 1:2 * REL_CLIP], axis=-1),
                          rep(0, n_lo), rep(2 * REL_CLIP, period - width)], axis=-1) * LOG2E
    flat = jnp.tile(gp, (1, 1, TQ))[:, :, :TQ * (period - 1)]
    skew = flat.reshape(gp.shape[0], N_HEADS, TQ, period - 1)[..., :width]
    i = np.arange(TQ)[:, None]
    m = np.arange(width)[None, :]
    dchunk = i // CHUNK + A_PREV - m // CHUNK
    valid = (dchunk >= 0) & (dchunk <= A_PREV)
    t = jnp.where(jnp.asarray(valid)[None, None], skew, NEG_INF)
    return jnp.transpose(t, (0, 3, 1, 2)).reshape(gp.shape[0], 9, 128, N_HEADS * TQ)


def _mask_table_c():
    m = np.arange(3 * 128)[:, None]
    i = np.arange(TQ)[None, :]
    dchunk = i // CHUNK + SWA_PREV - m // CHUNK
    valid = (dchunk >= 0) & (dchunk <= SWA_PREV)
    t = np.where(valid, 0.0, NEG_INF).astype(np.float32)
    return np.tile(t, (1, N_HEADS)).reshape(3, 128, N_HEADS * TQ)


def _dot(a, b):
    return jnp.dot(a, b, preferred_element_type=F32)


def _dot_nt(a, b):
    return lax.dot_general(a, b, (((1,), (1,)), ((), ())), preferred_element_type=F32)


def _dot_tn(a, b):
    return lax.dot_general(a, b, (((0,), (0,)), ((), ())), preferred_element_type=F32)


def _rope(x, cos, sin_signed, half):
    lane = lax.broadcasted_iota(jnp.int32, x.shape, 1)
    first = (lane & (2 * half - 1)) < half
    swapped = jnp.where(first, pltpu.roll(x, 128 - half, 1), pltpu.roll(x, half, 1))
    return x * cos + swapped * sin_signed


def _project(xb, tab_ref, w_ref, wuq_ref, wukv_ref, gq_ref, gkv_ref, p_ref, rows=slice(None)):
    tab_ref, p_ref = tab_ref.at[rows], p_ref.at[rows]
    cos64, sin64 = tab_ref[:, 0:128], tab_ref[:, 128:256]
    cos32, sin32 = tab_ref[:, 256:384], tab_ref[:, 384:512]

    def mm(lo, hi):
        return _dot(xb, w_ref[:, lo:hi])

    rb = mm(W_B, W_B + 512)
    cq = rb[:, 0:256]
    ms = jnp.sum(cq * cq, axis=-1, keepdims=True) * (1.0 / MLA_Q_RANK)
    qn = (cq * lax.rsqrt(ms + 1e-6) * gq_ref[...]).astype(BF16)
    ckv = rb[:, 256:384]
    ms = jnp.mean(ckv * ckv, axis=-1, keepdims=True)
    kvn = (ckv * lax.rsqrt(ms + 1e-6) * gkv_ref[...]).astype(BF16)
    krb = _rope(rb[:, 384:512], cos32, sin32, 16).astype(BF16)

    r = mm(W_A, W_A + 768)
    p_ref[:, P_AQ:P_AQ + 256] = (r[:, 0:256] * QSCALE).astype(BF16)
    p_ref[:, P_AK:P_AK + 512] = r[:, 256:768].astype(BF16)

    q = _dot(qn, wuq_ref[...]) * B_QSCALE
    for p in range(2):
        base = P_BQ0 + 256 * p
        p_ref[:, base:base + 128] = q[:, 256 * p:256 * p + 128].astype(BF16)
        p_ref[:, base + 128:base + 256] = _rope(q[:, 256 * p + 128:256 * p + 256], cos32, sin32, 16).astype(BF16)
    kv = _dot(kvn, wukv_ref[...])
    for p in range(2):
        base = P_BK0 + 256 * p
        p_ref[:, base:base + 128] = kv[:, 128 * p:128 * (p + 1)].astype(BF16)
        p_ref[:, base + 128:base + 256] = krb
    p_ref[:, P_BV:P_BV + 256] = kv[:, 256:512].astype(BF16)

    r = mm(W_G, W_G + 1024)
    p_ref[:, P_GA:P_GA + 1024] = (r * (1.0 / (1.0 + jnp.exp(-r)))).astype(BF16)

    r = mm(W_C, W_C + 512)
    for j in range(2):
        qj = _rope(r[:, 128 * j:128 * (j + 1)], cos64, sin64, 32)
        p_ref[:, P_CQ + 128 * j:P_CQ + 128 * (j + 1)] = (qj * QSCALE).astype(BF16)
    p_ref[:, P_CK:P_CK + 128] = _rope(r[:, 256:384], cos64, sin64, 32).astype(BF16)
    p_ref[:, P_CV:P_CV + 128] = r[:, 384:512].astype(BF16)

    r = mm(W_M, W_M + 256)
    p_ref[:, P_MQ:P_MQ + 256] = (r * QSCALE).astype(BF16)


def _inproj_kernel(x_ref, tab_ref, w_ref, wuq_ref, wukv_ref, gq_ref, gkv_ref, p_ref):
    _project(x_ref[...].astype(BF16), tab_ref, w_ref, wuq_ref, wukv_ref, gq_ref, gkv_ref, p_ref)


def _inproj(x2d, tab, w, wuq, wukv, gq, gkv):
    n = x2d.shape[0]
    const = lambda shape: pl.BlockSpec(shape, lambda i: (0,) * len(shape))
    return pl.pallas_call(
        _inproj_kernel,
        grid=(n // TM,),
        in_specs=[
            pl.BlockSpec((TM, D_MODEL), lambda i: (i, 0)),
            pl.BlockSpec((TM, 512), lambda i: (i, 0)),
            const((D_MODEL, W_WIDTH)),
            const((256, 512)),
            const((128, 512)),
            const((1, 256)),
            const((1, 128)),
        ],
        out_specs=pl.BlockSpec((TM, P_WIDTH), lambda i: (i, 0)),
        out_shape=jax.ShapeDtypeStruct((n, P_WIDTH), BF16),
        compiler_params=pltpu.CompilerParams(
            dimension_semantics=("parallel",), vmem_limit_bytes=VMEM_LIMIT),
        name="inproj",
    )(x2d, tab, w, wuq, wukv, gq, gkv)


def _memkv_kernel(mem_ref, w_ref, o_ref):
    o_ref[0] = _dot(mem_ref[0].astype(BF16), w_ref[...]).astype(BF16)


def _memkv(mem, w_all):
    b = mem.shape[0]
    n = w_all.shape[1]
    return pl.pallas_call(
        _memkv_kernel,
        grid=(b,),
        in_specs=[pl.BlockSpec((1, MEM_LEN, D_MODEL), lambda i: (i, 0, 0)),
                  pl.BlockSpec((D_MODEL, n), lambda i: (0, 0))],
        out_specs=pl.BlockSpec((1, MEM_LEN, n), lambda i: (i, 0, 0)),
        out_shape=jax.ShapeDtypeStruct((b, MEM_LEN, n), BF16),
        compiler_params=pltpu.CompilerParams(
            dimension_semantics=("parallel",), vmem_limit_bytes=VMEM_LIMIT),
        name="memkv",
    )(mem, w_all)


def _window_attn_kernel(*refs, win, prev, dk, has_table, has_sink):
    refs = list(refs)
    sink_ref = refs.pop(0) if has_sink else None
    q_ref, k_ref, v_ref, g_ref = refs[:4]
    e_ref = refs[4] if has_table else None
    o_ref, sa_sc, sb_sc = refs[-3:]
    n_items = q_ref.shape[1] // TQ
    lanes = N_HEADS * TQ

    lane128 = lax.broadcasted_iota(jnp.int32, (TQ, 128), 1)
    lo, hi = lane128 < HEAD_DIM, lane128 >= HEAD_DIM
    if has_sink:
        col = lax.broadcasted_iota(jnp.int32, (1, lanes), 1)
        order = C_HEAD_ORDER if dk == 128 else tuple(range(N_HEADS))
        sink = jnp.where(col < TQ, sink_ref[order[0]],
                         jnp.where(col < 2 * TQ, sink_ref[order[1]],
                                   jnp.where(col < 3 * TQ, sink_ref[order[2]], sink_ref[order[3]]))) * LOG2E

    def window_start(item):
        if prev is None:
            return 0
        return pl.multiple_of(jnp.maximum(item * TQ - prev, 0), 128)

    def scores_into(item, s_sc):
        q = q_ref[0, pl.ds(pl.multiple_of(item * TQ, TQ), TQ), :].astype(F32)
        if dk == 256:
            halves = [q[:, 0:128], q[:, 0:128], q[:, 128:256], q[:, 128:256]]
            zero = jnp.zeros((TQ, 128), F32)
            blocks = [jnp.concatenate([jnp.where(lo, halves[0], 0.0), zero], axis=1),
                      jnp.concatenate([jnp.where(hi, halves[1], 0.0), zero], axis=1),
                      jnp.concatenate([zero, jnp.where(lo, halves[2], 0.0)], axis=1),
                      jnp.concatenate([zero, jnp.where(hi, halves[3], 0.0)], axis=1)]
        else:
            blocks = [jnp.where(lo, q[:, 0:128], 0.0), jnp.where(hi, q[:, 0:128], 0.0),
                      jnp.where(lo, q[:, 128:256], 0.0), jnp.where(hi, q[:, 128:256], 0.0)]
        qs = jnp.concatenate(blocks, axis=0).astype(BF16)
        s_sc[...] = _dot_nt(k_ref[0, pl.ds(window_start(item), win), :], qs)

    def consume(item, s_sc):
        start = window_start(item)
        s = s_sc[...]
        if has_table:
            mb0 = jnp.maximum(prev // 128 - item, 0)
            s = s + jnp.concatenate([e_ref[mb0 + jb] for jb in range(win // 128)], axis=0)
        m = jnp.max(s, axis=0, keepdims=True)
        if has_sink:
            m = jnp.maximum(m, sink)
        p = jnp.exp2(s - m)
        l = jnp.sum(p, axis=0, keepdims=True)
        if has_sink:
            l = l + jnp.exp2(sink - m)
        inv = 1.0 / l
        pb = p.astype(BF16)
        v = v_ref[0, pl.ds(start, win), :]
        parts = []
        for pr in range(2):
            vp = v[:, 128 * pr:128 * (pr + 1)] if dk == 256 else v
            ot = _dot_tn(vp, pb[:, 256 * pr:256 * (pr + 1)])
            for e in range(2):
                h = 2 * pr + e
                parts.append(ot[64 * e:64 * (e + 1), 128 * e:128 * (e + 1)] * inv[:, TQ * h:TQ * (h + 1)])
        o = jnp.concatenate(parts, axis=0).T
        rows = pl.ds(pl.multiple_of(item * TQ, TQ), TQ)
        o_ref[0, rows, :] = (o * g_ref[0, rows, :].astype(F32)).astype(BF16)

    bufs = (sa_sc, sb_sc)
    scores_into(0, bufs[0])

    def body(i, carry):
        for j in range(WIN_UNROLL):
            item = WIN_UNROLL * i + j
            scores_into(jnp.minimum(item + 1, n_items - 1), bufs[(j + 1) % 2])
            consume(item, bufs[j % 2])
        return carry

    lax.fori_loop(0, n_items // WIN_UNROLL, body, 0)


def _window_attn(q_src, q_col, k_src, k_col, v_src, v_col, g_col, *, win, prev, dk, table=None, sinks=None):
    b, s, _ = q_src.shape
    skv = k_src.shape[1]
    assert s % (WIN_UNROLL * TQ) == 0 and skv >= win
    kern = functools.partial(_window_attn_kernel, win=win, prev=prev, dk=dk,
                             has_table=table is not None, has_sink=sinks is not None)
    in_specs, args = [], []
    if sinks is not None:
        in_specs.append(pl.BlockSpec(memory_space=pltpu.SMEM)); args.append(sinks)
    in_specs += [pl.BlockSpec((1, s, 256), lambda i: (i, 0, q_col // 256)),
                 pl.BlockSpec((1, skv, dk), lambda i: (i, 0, k_col // dk)),
                 pl.BlockSpec((1, skv, dk), lambda i: (i, 0, v_col // dk)),
                 pl.BlockSpec((1, s, 256), lambda i: (i, 0, g_col // 256))]
    args += [q_src, k_src, v_src, q_src]
    if table is not None:
        in_specs.append(pl.BlockSpec(table.shape, lambda i: (0, 0, 0))); args.append(table)
    return pl.pallas_call(
        kern,
        grid=(b,),
        in_specs=in_specs,
        out_specs=pl.BlockSpec((1, s, 256), lambda i: (i, 0, 0)),
        out_shape=jax.ShapeDtypeStruct((b, s, GROUP), BF16),
        scratch_shapes=[pltpu.VMEM((win, N_HEADS * TQ), F32), pltpu.VMEM((win, N_HEADS * TQ), F32)],
        compiler_params=pltpu.CompilerParams(
            dimension_semantics=("parallel",), vmem_limit_bytes=VMEM_LIMIT),
        name="attn_win%d" % win,
    )(*args)


def _attn_b_kernel(q0_ref, q1_ref, k0_ref, k1_ref, v_ref, g_ref, o_ref,
                   qs_sc, sa_sc, sb_sc, m_sc, l_sc, acc_sc):
    t = pl.program_id(1)
    n_tiles = B_TQ // TQ
    lane = lax.broadcasted_iota(jnp.int32, (TQ, 128), 1)
    q_refs = (q0_ref, q1_ref)
    k_refs = (k0_ref, k1_ref)

    def stack_queries(c):
        stacked = []
        for pr in range(2):
            q = q_refs[pr][0, TQ * c:TQ * (c + 1), :].astype(F32)
            nope, rope = q[:, 0:128], q[:, 128:256]
            head_a = jnp.concatenate([jnp.where(lane < 64, nope, 0.0), jnp.where(lane < 32, rope, 0.0)], axis=1)
            head_b = jnp.concatenate([jnp.where(lane >= 64, nope, 0.0), jnp.where(lane >= 32, rope, 0.0)], axis=1)
            qs = jnp.concatenate([head_a, head_b], axis=0).astype(BF16)
            qs_sc[2 * c + pr] = qs
            stacked.append(qs)
        return stacked

    lane2 = lax.broadcasted_iota(jnp.int32, (1, 2 * TQ), 1)
    hide_first_chunk = jnp.where((lane2 & (TQ - 1)) < CHUNK, NEG_INF, 0.0)

    def scores_into(kb, c, s_sc, nk=B_TK, qs=None):
        start = pl.multiple_of(kb * B_TK, B_TK)
        for pr in range(2):
            q = qs_sc[2 * c + pr] if qs is None else qs[pr]
            s_sc[pr, 0:nk, :] = _dot_nt(k_refs[pr][0, pl.ds(start, nk), :], q)

    def consume(kb, c, s_sc, nk=B_TK, diagonal=False):
        start = pl.multiple_of(kb * B_TK, B_TK)
        v = v_ref[0, pl.ds(start, nk), :]
        for pr in range(2):
            u = 2 * c + pr
            s = s_sc[pr, 0:nk, :]
            if diagonal:
                s = jnp.concatenate([s[:nk - CHUNK], s[nk - CHUNK:] + hide_first_chunk], axis=0)
            m_prev = m_sc[u]
            m_new = jnp.maximum(m_prev, jnp.max(s, axis=0, keepdims=True))
            alpha = jnp.exp2(m_prev - m_new)
            p = jnp.exp2(s - m_new)
            l_sc[u] = alpha * l_sc[u] + jnp.sum(p, axis=0, keepdims=True)
            m_sc[u] = m_new
            acc_sc[u] = alpha * acc_sc[u] + _dot_tn(v[:, 128 * pr:128 * (pr + 1)], p.astype(BF16))

    bufs = (sa_sc, sb_sc)
    scores_into(0, 0, bufs[0], qs=stack_queries(0))
    for c in range(1, n_tiles):
        stack_queries(c)
    m_sc[...] = jnp.full(m_sc.shape, NEG_INF, F32)
    l_sc[...] = jnp.zeros(l_sc.shape, F32)
    acc_sc[...] = jnp.zeros(acc_sc.shape, F32)

    def body(kb, carry):
        for c in range(n_tiles):
            if c + 1 < n_tiles:
                scores_into(kb, c + 1, bufs[(c + 1) % 2])
            else:
                scores_into(kb + 1, 0, bufs[0])
            consume(kb, c, bufs[c % 2])
        return carry

    lax.fori_loop(0, t, body, 0)

    for c in range(n_tiles):
        if c + 1 < n_tiles:
            scores_into(t, c + 1, bufs[(c + 1) % 2], nk=TQ * (c + 2))
        consume(t, c, bufs[c % 2], nk=TQ * (c + 1), diagonal=True)

    for c in range(n_tiles):
        parts = []
        for pr in range(2):
            u = 2 * c + pr
            inv = 1.0 / l_sc[u]
            for e in range(2):
                parts.append(acc_sc[u, 64 * e:64 * (e + 1), 128 * e:128 * (e + 1)] * inv[:, 128 * e:128 * (e + 1)])
        o = jnp.concatenate(parts, axis=0).T
        o_ref[0, TQ * c:TQ * (c + 1), :] = (o * g_ref[0, TQ * c:TQ * (c + 1), :].astype(F32)).astype(BF16)


def _attn_b(p3):
    b, s, _ = p3.shape
    n_units = 2 * (B_TQ // TQ)
    return pl.pallas_call(
        _attn_b_kernel,
        grid=(b, s // B_TQ),
        in_specs=[
            pl.BlockSpec((1, B_TQ, 256), lambda i, t: (i, t, P_BQ0 // 256)),
            pl.BlockSpec((1, B_TQ, 256), lambda i, t: (i, t, P_BQ1 // 256)),
            pl.BlockSpec((1, s, 256), lambda i, t: (i, 0, P_BK0 // 256)),
            pl.BlockSpec((1, s, 256), lambda i, t: (i, 0, P_BK1 // 256)),
            pl.BlockSpec((1, s, 256), lambda i, t: (i, 0, P_BV // 256)),
            pl.BlockSpec((1, B_TQ, 256), lambda i, t: (i, t, P_GB // 256)),
        ],
        out_specs=pl.BlockSpec((1, B_TQ, 256), lambda i, t: (i, t, 0)),
        out_shape=jax.ShapeDtypeStruct((b, s, GROUP), BF16),
        scratch_shapes=[pltpu.VMEM((n_units, 2 * TQ, 256), BF16),
                        pltpu.VMEM((2, B_TK, 2 * TQ), F32),
                        pltpu.VMEM((2, B_TK, 2 * TQ), F32),
                        pltpu.VMEM((n_units, 1, 2 * TQ), F32),
                        pltpu.VMEM((n_units, 1, 2 * TQ), F32),
                        pltpu.VMEM((n_units, 128, 2 * TQ), F32)],
        compiler_params=pltpu.CompilerParams(
            dimension_semantics=("parallel", "arbitrary"), vmem_limit_bytes=VMEM_LIMIT),
        name="attn_b",
    )(p3, p3, p3, p3, p3, p3)


def _residual_norm(ya_ref, yb_ref, yc_ref, ym_ref, x_ref, w_ref, g_ref, b_ref, rows=slice(None)):
    y = (_dot(ya_ref[rows, :], w_ref[0:256, :]) + _dot(yb_ref[rows, :], w_ref[256:512, :])
         + _dot(yc_ref[rows, :], w_ref[512:768, :]) + _dot(ym_ref[rows, :], w_ref[768:1024, :]))
    z = ALPHA * x_ref[rows, :] + y
    mu = jnp.mean(z, axis=-1, keepdims=True)
    zc = z - mu
    var = jnp.mean(zc * zc, axis=-1, keepdims=True)
    return zc * lax.rsqrt(var + 1e-5) * g_ref[...] + b_ref[...]


def _outproj_kernel(ya_ref, yb_ref, yc_ref, ym_ref, x_ref, w_ref, g_ref, b_ref, o_ref):
    for h in range(2):
        rows = pl.ds(h * (TM // 2), TM // 2)
        o_ref[rows, :] = _residual_norm(ya_ref, yb_ref, yc_ref, ym_ref, x_ref, w_ref, g_ref, b_ref, rows)


def _out_in_proj_kernel(ya_ref, yb_ref, yc_ref, ym_ref, x_ref, wo_ref, g_ref, b_ref,
                        tab_ref, w_ref, wuq_ref, wukv_ref, gq_ref, gkv_ref, o_ref, p_ref):
    halves = [pl.ds(h * (TM // 2), TM // 2) for h in range(2)]
    xb = []
    for rows in halves:
        xn = _residual_norm(ya_ref, yb_ref, yc_ref, ym_ref, x_ref, wo_ref, g_ref, b_ref, rows)
        o_ref[rows, :] = xn
        xb.append(xn.astype(BF16))
    for rows, x in zip(halves, xb):
        _project(x, tab_ref, w_ref, wuq_ref, wukv_ref, gq_ref, gkv_ref, p_ref, rows)


def _outproj(ya, yb, yc, ym, x2d, w, g, bias):
    n = x2d.shape[0]
    ytile = pl.BlockSpec((TM, GROUP), lambda i: (i, 0))
    const = lambda shape: pl.BlockSpec(shape, lambda i: (0,) * len(shape))
    return pl.pallas_call(
        _outproj_kernel,
        grid=(n // TM,),
        in_specs=[ytile, ytile, ytile, ytile,
                  pl.BlockSpec((TM, D_MODEL), lambda i: (i, 0)),
                  const((D_MODEL, D_MODEL)), const((1, D_MODEL)), const((1, D_MODEL))],
        out_specs=pl.BlockSpec((TM, D_MODEL), lambda i: (i, 0)),
        out_shape=jax.ShapeDtypeStruct((n, D_MODEL), F32),
        compiler_params=pltpu.CompilerParams(
            dimension_semantics=("parallel",), vmem_limit_bytes=VMEM_LIMIT),
        name="outproj",
    )(ya, yb, yc, ym, x2d, w, g, bias)


def _out_in_proj(ya, yb, yc, ym, x2d, wo, g, bias, tab, w, wuq, wukv, gq, gkv):
    n = x2d.shape[0]
    ytile = pl.BlockSpec((TM, GROUP), lambda i: (i, 0))
    const = lambda shape: pl.BlockSpec(shape, lambda i: (0,) * len(shape))
    return pl.pallas_call(
        _out_in_proj_kernel,
        grid=(n // TM,),
        in_specs=[ytile, ytile, ytile, ytile,
                  pl.BlockSpec((TM, D_MODEL), lambda i: (i, 0)),
                  const((D_MODEL, D_MODEL)), const((1, D_MODEL)), const((1, D_MODEL)),
                  pl.BlockSpec((TM, 512), lambda i: (i, 0)),
                  const((D_MODEL, W_WIDTH)), const((256, 512)), const((128, 512)),
                  const((1, 256)), const((1, 128))],
        out_specs=[pl.BlockSpec((TM, D_MODEL), lambda i: (i, 0)),
                   pl.BlockSpec((TM, P_WIDTH), lambda i: (i, 0))],
        out_shape=[jax.ShapeDtypeStruct((n, D_MODEL), F32),
                   jax.ShapeDtypeStruct((n, P_WIDTH), BF16)],
        compiler_params=pltpu.CompilerParams(
            dimension_semantics=("parallel",), vmem_limit_bytes=VMEM_LIMIT),
        name="out_in_proj",
    )(ya, yb, yc, ym, x2d, wo, g, bias, tab, w, wuq, wukv, gq, gkv)


def kernel(x, mem, positions, w_in, rel_bias, mla_q_norm, w_uq, mla_kv_norm, w_ukv,
           swa_sinks, w_mem_kv, w_out, ln_gain, ln_bias):
    b, s, d = x.shape
    depth = w_in.shape[0]
    assert d == D_MODEL and depth == DEPTH and s % B_TK == 0 and s >= A_WIN and (b * s) % TM == 0

    cols, cperm = _inproj_cols()
    w_in_p = _take_cols(w_in, cols).astype(BF16)
    wuq_p = jnp.pad(_take_cols(w_uq, _uq_cols()), ((0, 0), (0, 256 - MLA_Q_RANK), (0, 0))).astype(BF16)
    wukv_p = _take_cols(w_ukv, _ukv_cols()).astype(BF16)
    gq = jnp.pad(mla_q_norm, ((0, 0), (0, 256 - MLA_Q_RANK)))[:, None, :]
    gkv = mla_kv_norm[:, None, :]
    rows = np.concatenate([np.arange(512), 512 + cperm, np.arange(768, 1024)])
    w_out_p = _take_cols(w_out, rows, axis=1).astype(BF16)
    w_mem_all = jnp.transpose(w_mem_kv, (1, 0, 2)).reshape(D_MODEL, depth * 512).astype(BF16)
    e_a = _bias_table_a(rel_bias)
    e_c = jnp.asarray(_mask_table_c())
    tab = _rope_tables(positions)

    memkv = _memkv(mem, w_mem_all)
    h = x.reshape(b * s, d)
    p2 = _inproj(h, tab, w_in_p[0], wuq_p[0], wukv_p[0], gq[0], gkv[0])
    for l in range(depth):
        p3 = p2.reshape(b, s, P_WIDTH)
        ya = _window_attn(p3, P_AQ, p3, P_AK, p3, P_AV, P_GA, win=A_WIN, prev=A_PREV * CHUNK, dk=256, table=e_a[l])
        yb = _attn_b(p3)
        yc = _window_attn(p3, P_CQ, p3, P_CK, p3, P_CV, P_GC, win=C_WIN, prev=SWA_PREV * CHUNK, dk=128,
                          table=e_c, sinks=swa_sinks[l])
        ym = _window_attn(p3, P_MQ, memkv, 512 * l, memkv, 512 * l + 256, P_GM, win=MEM_LEN, prev=None, dk=256)
        ys = [y.reshape(b * s, GROUP) for y in (ya, yb, yc, ym)]
        ln = (ln_gain[l][None, :], ln_bias[l][None, :])
        if l + 1 < depth:
            h, p2 = _out_in_proj(*ys, h, w_out_p[l], *ln, tab, w_in_p[l + 1], wuq_p[l + 1], wukv_p[l + 1],
                                 gq[l + 1], gkv[l + 1])
        else:
            h = _outproj(*ys, h, w_out_p[l], *ln)
    return h.reshape(b, s, d)
```

```python
import functools

import numpy as np
import jax
import jax.numpy as jnp
from jax import lax
from jax.experimental import pallas as pl
from jax.experimental.pallas import tpu as pltpu

F32 = jnp.float32
BF16 = jnp.bfloat16

D_MODEL = 1024
DEPTH = 4
CHUNK = 64
HEAD_DIM = 64
GROUP = 256
N_HEADS = 4
ROPE_THETA = 10000.0
NEG_INF = -1e30
A_PREV = 8
REL_CLIP = 128
MLA_NOPE = 64
MLA_ROPE = 32
MLA_Q_RANK = 192
MLA_KV_RANK = 128
SWA_PREV = 2
MEM_LEN = 256
ALPHA = (2.0 * DEPTH) ** 0.25

TQ = 128
A_WIN = TQ + A_PREV * CHUNK
C_WIN = TQ + SWA_PREV * CHUNK
WIN_UNROLL = 4
B_TK = 512
B_TQ = 512
TM = 512
VMEM_LIMIT = 56 * 1024 * 1024
LOG2E = 1.4426950408889634
QSCALE = HEAD_DIM ** -0.5 * LOG2E
B_QSCALE = (MLA_NOPE + MLA_ROPE) ** -0.5 * LOG2E

P_AQ, P_AK, P_AV = 0, 256, 512
P_BQ0, P_BQ1, P_BK0, P_BK1, P_BV = 768, 1024, 1280, 1536, 1792
P_CQ, P_CK, P_CV = 2048, 2304, 2432
P_MQ = 2560
P_GA, P_GB, P_GC, P_GM = 2816, 3072, 3328, 3584
P_WIDTH = 3840

W_A, W_C, W_M, W_G, W_B, W_WIDTH = 0, 768, 1280, 1536, 2560, 3072

C_HEAD_ORDER = (0, 2, 1, 3)


def _inproj_cols():
    r = np.arange
    aq, ak, av, ag = 0, 256, 512, 768
    bcq, bckv, bkr, bg = 1024, 1216, 1344, 1376
    cq, ck, cv, cg = 1632, 1888, 2016, 2144
    mq, mg = 2400, 2656
    cperm = np.concatenate([r(64) + 64 * h for h in C_HEAD_ORDER])
    pad = lambda n: np.full(n, -1)
    cols = np.concatenate([
        aq + r(256), ak + r(256), av + r(256),
        cq + cperm, ck + r(128), cv + r(128),
        mq + r(256),
        ag + r(256), bg + r(256), cg + cperm, mg + r(256),
        bcq + r(192), pad(64), bckv + r(128), bkr + r(32), bkr + r(32), pad(64),
    ])
    assert cols.shape[0] == W_WIDTH
    return cols, cperm


def _take_cols(w, cols, axis=-1):
    axis = axis % w.ndim
    pieces, i = [], 0
    while i < len(cols):
        j = i + 1
        if cols[i] < 0:
            while j < len(cols) and cols[j] < 0:
                j += 1
            shape = w.shape[:axis] + (j - i,) + w.shape[axis + 1:]
            pieces.append(jnp.zeros(shape, w.dtype))
        else:
            while j < len(cols) and cols[j] == cols[j - 1] + 1:
                j += 1
            pieces.append(lax.slice_in_dim(w, int(cols[i]), int(cols[i]) + (j - i), axis=axis))
        i = j
    return jnp.concatenate(pieces, axis=axis)


def _uq_cols():
    r = np.arange
    per = MLA_NOPE + MLA_ROPE
    out = []
    for p in range(2):
        h0, h1 = 2 * p, 2 * p + 1
        out += [per * h0 + r(64), per * h1 + r(64),
                per * h0 + 64 + r(32), per * h1 + 64 + r(32), np.full(64, -1)]
    return np.concatenate(out)


def _ukv_cols():
    r = np.arange
    return np.concatenate([128 * h + r(64) for h in range(4)] + [128 * h + 64 + r(64) for h in range(4)])


def _rope_tables(positions):
    pos = positions.astype(F32).reshape(-1, 1)
    narrow, col0, expand = [], 0, np.zeros((2 * (HEAD_DIM + MLA_ROPE) // 2, 512), np.float32)
    for t, d in enumerate((HEAD_DIM, MLA_ROPE)):
        half = d // 2
        inv = ROPE_THETA ** (-jnp.arange(0, d, 2, dtype=F32) / d)
        ang = pos * inv[None, :]
        narrow += [jnp.cos(ang), jnp.sin(ang)]
        lane = np.arange(128)
        k = (lane % d) % half
        expand[col0 + k, 256 * t + lane] = 1.0
        expand[col0 + half + k, 256 * t + 128 + lane] = np.where(lane % d < half, -1.0, 1.0)
        col0 += d
    return jnp.dot(jnp.concatenate(narrow, axis=1), jnp.asarray(expand), precision=lax.Precision.HIGHEST)


def _bias_table_a(rel_bias):
    width, period = 9 * 128, 9 * 128 + TQ
    k = np.arange(period)
    d = np.where(k < width, A_PREV * CHUNK - k, A_PREV * CHUNK + period - k)
    idx = np.clip(d, -REL_CLIP, REL_CLIP) + REL_CLIP
    n_hi = A_PREV * CHUNK - REL_CLIP + 1
    n_lo = width - n_hi - (2 * REL_CLIP - 1)
    expect = np.concatenate([np.full(n_hi, 2 * REL_CLIP), np.arange(2 * REL_CLIP - 1, 0, -1),
                             np.zeros(n_lo, np.int64), np.full(period - width, 2 * REL_CLIP)])
    assert np.array_equal(idx, expect)
    rep = lambda col, n: jnp.broadcast_to(rel_bias[:, :, col:col + 1], rel_bias.shape[:2] + (n,))
    gp = jnp.concatenate([rep(2 * REL_CLIP, n_hi), jnp.flip(rel_bias[:, :, 1:2 * REL_CLIP], axis=-1),
                          rep(0, n_lo), rep(2 * REL_CLIP, period - width)], axis=-1) * LOG2E
    flat = jnp.tile(gp, (1, 1, TQ))[:, :, :TQ * (period - 1)]
    skew = flat.reshape(gp.shape[0], N_HEADS, TQ, period - 1)[..., :width]
    i = np.arange(TQ)[:, None]
    m = np.arange(width)[None, :]
    dchunk = i // CHUNK + A_PREV - m // CHUNK
    valid = (dchunk >= 0) & (dchunk <= A_PREV)
    t = jnp.where(jnp.asarray(valid)[None, None], skew, NEG_INF)
    return jnp.transpose(t, (0, 3, 1, 2)).reshape(gp.shape[0], 9, 128, N_HEADS * TQ)


def _mask_table_c():
    m = np.arange(3 * 128)[:, None]
    i = np.arange(TQ)[None, :]
    dchunk = i // CHUNK + SWA_PREV - m // CHUNK
    valid = (dchunk >= 0) & (dchunk <= SWA_PREV)
    t = np.where(valid, 0.0, NEG_INF).astype(np.float32)
    return np.tile(t, (1, N_HEADS)).reshape(3, 128, N_HEADS * TQ)


def _dot(a, b):
    return jnp.dot(a, b, preferred_element_type=F32)


def _dot_nt(a, b):
    return lax.dot_general(a, b, (((1,), (1,)), ((), ())), preferred_element_type=F32)


def _dot_tn(a, b):
    return lax.dot_general(a, b, (((0,), (0,)), ((), ())), preferred_element_type=F32)


def _rope(x, cos, sin_signed, half):
    lane = lax.broadcasted_iota(jnp.int32, x.shape, 1)
    first = (lane & (2 * half - 1)) < half
    swapped = jnp.where(first, pltpu.roll(x, 128 - half, 1), pltpu.roll(x, half, 1))
    return x * cos + swapped * sin_signed


def _project(xb, tab_ref, w_ref, wuq_ref, wukv_ref, gq_ref, gkv_ref, p_ref, rows=slice(None)):
    tab_ref, p_ref = tab_ref.at[rows], p_ref.at[rows]
    cos64, sin64 = tab_ref[:, 0:128], tab_ref[:, 128:256]
    cos32, sin32 = tab_ref[:, 256:384], tab_ref[:, 384:512]

    def mm(lo, hi):
        return _dot(xb, w_ref[:, lo:hi])

    rb = mm(W_B, W_B + 512)
    cq = rb[:, 0:256]
    ms = jnp.sum(cq * cq, axis=-1, keepdims=True) * (1.0 / MLA_Q_RANK)
    qn = (cq * lax.rsqrt(ms + 1e-6) * gq_ref[...]).astype(BF16)
    ckv = rb[:, 256:384]
    ms = jnp.mean(ckv * ckv, axis=-1, keepdims=True)
    kvn = (ckv * lax.rsqrt(ms + 1e-6) * gkv_ref[...]).astype(BF16)
    krb = _rope(rb[:, 384:512], cos32, sin32, 16).astype(BF16)

    r = mm(W_A, W_A + 768)
    p_ref[:, P_AQ:P_AQ + 256] = (r[:, 0:256] * QSCALE).astype(BF16)
    p_ref[:, P_AK:P_AK + 512] = r[:, 256:768].astype(BF16)

    q = _dot(qn, wuq_ref[...]) * B_QSCALE
    for p in range(2):
        base = P_BQ0 + 256 * p
        p_ref[:, base:base + 128] = q[:, 256 * p:256 * p + 128].astype(BF16)
        p_ref[:, base + 128:base + 256] = _rope(q[:, 256 * p + 128:256 * p + 256], cos32, sin32, 16).astype(BF16)
    kv = _dot(kvn, wukv_ref[...])
    for p in range(2):
        base = P_BK0 + 256 * p
        p_ref[:, base:base + 128] = kv[:, 128 * p:128 * (p + 1)].astype(BF16)
        p_ref[:, base + 128:base + 256] = krb
    p_ref[:, P_BV:P_BV + 256] = kv[:, 256:512].astype(BF16)

    r = mm(W_G, W_G + 1024)
    p_ref[:, P_GA:P_GA + 1024] = (r * (1.0 / (1.0 + jnp.exp(-r)))).astype(BF16)

    r = mm(W_C, W_C + 512)
    for j in range(2):
        qj = _rope(r[:, 128 * j:128 * (j + 1)], cos64, sin64, 32)
        p_ref[:, P_CQ + 128 * j:P_CQ + 128 * (j + 1)] = (qj * QSCALE).astype(BF16)
    p_ref[:, P_CK:P_CK + 128] = _rope(r[:, 256:384], cos64, sin64, 32).astype(BF16)
    p_ref[:, P_CV:P_CV + 128] = r[:, 384:512].astype(BF16)

    r = mm(W_M, W_M + 256)
    p_ref[:, P_MQ:P_MQ + 256] = (r * QSCALE).astype(BF16)


def _inproj_kernel(x_ref, tab_ref, w_ref, wuq_ref, wukv_ref, gq_ref, gkv_ref, p_ref):
    _project(x_ref[...].astype(BF16), tab_ref, w_ref, wuq_ref, wukv_ref, gq_ref, gkv_ref, p_ref)


def _inproj(x2d, tab, w, wuq, wukv, gq, gkv):
    n = x2d.shape[0]
    const = lambda shape: pl.BlockSpec(shape, lambda i: (0,) * len(shape))
    return pl.pallas_call(
        _inproj_kernel,
        grid=(n // TM,),
        in_specs=[
            pl.BlockSpec((TM, D_MODEL), lambda i: (i, 0)),
            pl.BlockSpec((TM, 512), lambda i: (i, 0)),
            const((D_MODEL, W_WIDTH)),
            const((256, 512)),
            const((128, 512)),
            const((1, 256)),
            const((1, 128)),
        ],
        out_specs=pl.BlockSpec((TM, P_WIDTH), lambda i: (i, 0)),
        out_shape=jax.ShapeDtypeStruct((n, P_WIDTH), BF16),
        compiler_params=pltpu.CompilerParams(
            dimension_semantics=("parallel",), vmem_limit_bytes=VMEM_LIMIT),
        name="inproj",
    )(x2d, tab, w, wuq, wukv, gq, gkv)


def _memkv_kernel(mem_ref, w_ref, o_ref):
    o_ref[0] = _dot(mem_ref[0].astype(BF16), w_ref[...]).astype(BF16)


def _memkv(mem, w_all):
    b = mem.shape[0]
    n = w_all.shape[1]
    return pl.pallas_call(
        _memkv_kernel,
        grid=(b,),
        in_specs=[pl.BlockSpec((1, MEM_LEN, D_MODEL), lambda i: (i, 0, 0)),
                  pl.BlockSpec((D_MODEL, n), lambda i: (0, 0))],
        out_specs=pl.BlockSpec((1, MEM_LEN, n), lambda i: (i, 0, 0)),
        out_shape=jax.ShapeDtypeStruct((b, MEM_LEN, n), BF16),
        compiler_params=pltpu.CompilerParams(
            dimension_semantics=("parallel",), vmem_limit_bytes=VMEM_LIMIT),
        name="memkv",
    )(mem, w_all)


def _window_attn_kernel(*refs, win, prev, dk, has_table, has_sink):
    refs = list(refs)
    sink_ref = refs.pop(0) if has_sink else None
    q_ref, k_ref, v_ref, g_ref = refs[:4]
    e_ref = refs[4] if has_table else None
    o_ref, sa_sc, sb_sc = refs[-3:]
    n_items = q_ref.shape[1] // TQ
    lanes = N_HEADS * TQ

    lane128 = lax.broadcasted_iota(jnp.int32, (TQ, 128), 1)
    lo, hi = lane128 < HEAD_DIM, lane128 >= HEAD_DIM
    if has_sink:
        col = lax.broadcasted_iota(jnp.int32, (1, lanes), 1)
        order = C_HEAD_ORDER if dk == 128 else tuple(range(N_HEADS))
        sink = jnp.where(col < TQ, sink_ref[order[0]],
                         jnp.where(col < 2 * TQ, sink_ref[order[1]],
                                   jnp.where(col < 3 * TQ, sink_ref[order[2]], sink_ref[order[3]]))) * LOG2E

    def window_start(item):
        if prev is None:
            return 0
        return pl.multiple_of(jnp.maximum(item * TQ - prev, 0), 128)

    def scores_into(item, s_sc):
        q = q_ref[0, pl.ds(pl.multiple_of(item * TQ, TQ), TQ), :].astype(F32)
        if dk == 256:
            zero = jnp.zeros((TQ, 128), F32)
            blocks = [jnp.concatenate([jnp.where(lo, q[:, 0:128], 0.0), zero], axis=1),
                      jnp.concatenate([jnp.where(hi, q[:, 0:128], 0.0), zero], axis=1),
                      jnp.concatenate([zero, jnp.where(lo, q[:, 128:256], 0.0)], axis=1),
                      jnp.concatenate([zero, jnp.where(hi, q[:, 128:256], 0.0)], axis=1)]
        else:
            blocks = [jnp.where(lo, q[:, 0:128], 0.0), jnp.where(hi, q[:, 0:128], 0.0),
                      jnp.where(lo, q[:, 128:256], 0.0), jnp.where(hi, q[:, 128:256], 0.0)]
        qs = jnp.concatenate(blocks, axis=0).astype(BF16)
        s_sc[...] = _dot_nt(k_ref[0, pl.ds(window_start(item), win), :], qs)

    def consume(item, s_sc):
        start = window_start(item)
        s = s_sc[...]
        if has_table:
            mb0 = jnp.maximum(prev // 128 - item, 0)
            s = s + jnp.concatenate([e_ref[mb0 + jb] for jb in range(win // 128)], axis=0)
        m = jnp.max(s, axis=0, keepdims=True)
        if has_sink:
            m = jnp.maximum(m, sink)
        p = jnp.exp2(s - m)
        l = jnp.sum(p, axis=0, keepdims=True)
        if has_sink:
            l = l + jnp.exp2(sink - m)
        inv = 1.0 / l
        pb = p.astype(BF16)
        v = v_ref[0, pl.ds(start, win), :]
        parts = []
        for pr in range(2):
            vp = v[:, 128 * pr:128 * (pr + 1)] if dk == 256 else v
            ot = _dot_tn(vp, pb[:, 256 * pr:256 * (pr + 1)])
            for e in range(2):
                h = 2 * pr + e
                parts.append(ot[64 * e:64 * (e + 1), 128 * e:128 * (e + 1)] * inv[:, TQ * h:TQ * (h + 1)])
        o = jnp.concatenate(parts, axis=0).T
        rows = pl.ds(pl.multiple_of(item * TQ, TQ), TQ)
        o_ref[0, rows, :] = (o * g_ref[0, rows, :].astype(F32)).astype(BF16)

    bufs = (sa_sc, sb_sc)
    scores_into(0, bufs[0])

    def body(i, carry):
        for j in range(WIN_UNROLL):
            item = WIN_UNROLL * i + j
            scores_into(jnp.minimum(item + 1, n_items - 1), bufs[(j + 1) % 2])
            consume(item, bufs[j % 2])
        return carry

    lax.fori_loop(0, n_items // WIN_UNROLL, body, 0)


def _window_attn(q_src, q_col, k_src, k_col, v_src, v_col, g_col, *, win, prev, dk, table=None, sinks=None):
    b, s, _ = q_src.shape
    skv = k_src.shape[1]
    assert s % (WIN_UNROLL * TQ) == 0 and skv >= win
    kern = functools.partial(_window_attn_kernel, win=win, prev=prev, dk=dk,
                             has_table=table is not None, has_sink=sinks is not None)
    in_specs, args = [], []
    if sinks is not None:
        in_specs.append(pl.BlockSpec(memory_space=pltpu.SMEM)); args.append(sinks)
    in_specs += [pl.BlockSpec((1, s, 256), lambda i: (i, 0, q_col // 256)),
                 pl.BlockSpec((1, skv, dk), lambda i: (i, 0, k_col // dk)),
                 pl.BlockSpec((1, skv, dk), lambda i: (i, 0, v_col // dk)),
                 pl.BlockSpec((1, s, 256), lambda i: (i, 0, g_col // 256))]
    args += [q_src, k_src, v_src, q_src]
    if table is not None:
        in_specs.append(pl.BlockSpec(table.shape, lambda i: (0, 0, 0))); args.append(table)
    return pl.pallas_call(
        kern,
        grid=(b,),
        in_specs=in_specs,
        out_specs=pl.BlockSpec((1, s, 256), lambda i: (i, 0, 0)),
        out_shape=jax.ShapeDtypeStruct((b, s, GROUP), BF16),
        scratch_shapes=[pltpu.VMEM((win, N_HEADS * TQ), F32), pltpu.VMEM((win, N_HEADS * TQ), F32)],
        compiler_params=pltpu.CompilerParams(
            dimension_semantics=("parallel",), vmem_limit_bytes=VMEM_LIMIT),
        name="attn_win%d" % win,
    )(*args)


def _attn_b_kernel(q0_ref, q1_ref, k0_ref, k1_ref, v_ref, g_ref, o_ref,
                   qs_sc, vt_sc, sa_sc, sb_sc, m_sc, l_sc, acc_sc):
    t = pl.program_id(1)
    n_tiles = B_TQ // TQ
    lane = lax.broadcasted_iota(jnp.int32, (TQ, 128), 1)
    q_refs = (q0_ref, q1_ref)
    k_refs = (k0_ref, k1_ref)

    def stack_queries(c):
        stacked = []
        for pr in range(2):
            q = q_refs[pr][0, TQ * c:TQ * (c + 1), :].astype(F32)
            nope, rope = q[:, 0:128], q[:, 128:256]
            head_a = jnp.concatenate([jnp.where(lane < 64, nope, 0.0), jnp.where(lane < 32, rope, 0.0)], axis=1)
            head_b = jnp.concatenate([jnp.where(lane >= 64, nope, 0.0), jnp.where(lane >= 32, rope, 0.0)], axis=1)
            qs = jnp.concatenate([head_a, head_b], axis=0).astype(BF16)
            qs_sc[2 * c + pr] = qs
            stacked.append(qs)
        return stacked

    lane2 = lax.broadcasted_iota(jnp.int32, (1, 2 * TQ), 1)
    hide_first_chunk = jnp.where((lane2 & (TQ - 1)) < CHUNK, NEG_INF, 0.0)

    def scores_into(kb, c, s_sc, nk=B_TK, qs=None):
        start = pl.multiple_of(kb * B_TK, B_TK)
        for pr in range(2):
            q = qs_sc[2 * c + pr] if qs is None else qs[pr]
            s_sc[pr, 0:nk, :] = _dot_nt(k_refs[pr][0, pl.ds(start, nk), :], q)

    def transpose_values(kb):
        vt_sc[...] = v_ref[0, pl.ds(pl.multiple_of(kb * B_TK, B_TK), B_TK), :].T

    def consume(kb, c, s_sc, nk=B_TK, diagonal=False):
        for pr in range(2):
            u = 2 * c + pr
            s = s_sc[pr, 0:nk, :]
            if diagonal:
                s = jnp.concatenate([s[:nk - CHUNK], s[nk - CHUNK:] + hide_first_chunk], axis=0)
            m_prev = m_sc[u]
            m_new = jnp.maximum(m_prev, jnp.max(s, axis=0, keepdims=True))
            alpha = jnp.exp2(m_prev - m_new)
            p = jnp.exp2(s - m_new)
            l_sc[u] = alpha * l_sc[u] + jnp.sum(p, axis=0, keepdims=True)
            m_sc[u] = m_new
            acc_sc[u] = alpha * acc_sc[u] + _dot(vt_sc[128 * pr:128 * (pr + 1), 0:nk], p.astype(BF16))

    bufs = (sa_sc, sb_sc)
    scores_into(0, 0, bufs[0], qs=stack_queries(0))
    for c in range(1, n_tiles):
        stack_queries(c)
    m_sc[...] = jnp.full(m_sc.shape, NEG_INF, F32)
    l_sc[...] = jnp.zeros(l_sc.shape, F32)
    acc_sc[...] = jnp.zeros(acc_sc.shape, F32)

    def body(kb, carry):
        transpose_values(kb)
        for c in range(n_tiles):
            if c + 1 < n_tiles:
                scores_into(kb, c + 1, bufs[(c + 1) % 2])
            else:
                scores_into(kb + 1, 0, bufs[0])
            consume(kb, c, bufs[c % 2])
        return carry

    lax.fori_loop(0, t, body, 0)

    transpose_values(t)
    for c in range(n_tiles):
        if c + 1 < n_tiles:
            scores_into(t, c + 1, bufs[(c + 1) % 2], nk=TQ * (c + 2))
        consume(t, c, bufs[c % 2], nk=TQ * (c + 1), diagonal=True)

    for c in range(n_tiles):
        parts = []
        for pr in range(2):
            u = 2 * c + pr
            inv = 1.0 / l_sc[u]
            for e in range(2):
                parts.append(acc_sc[u, 64 * e:64 * (e + 1), 128 * e:128 * (e + 1)] * inv[:, 128 * e:128 * (e + 1)])
        o = jnp.concatenate(parts, axis=0).T
        o_ref[0, TQ * c:TQ * (c + 1), :] = (o * g_ref[0, TQ * c:TQ * (c + 1), :].astype(F32)).astype(BF16)


def _attn_b(p3):
    b, s, _ = p3.shape
    n_units = 2 * (B_TQ // TQ)
    return pl.pallas_call(
        _attn_b_kernel,
        grid=(b, s // B_TQ),
        in_specs=[
            pl.BlockSpec((1, B_TQ, 256), lambda i, t: (i, t, P_BQ0 // 256)),
            pl.BlockSpec((1, B_TQ, 256), lambda i, t: (i, t, P_BQ1 // 256)),
            pl.BlockSpec((1, s, 256), lambda i, t: (i, 0, P_BK0 // 256)),
            pl.BlockSpec((1, s, 256), lambda i, t: (i, 0, P_BK1 // 256)),
            pl.BlockSpec((1, s, 256), lambda i, t: (i, 0, P_BV // 256)),
            pl.BlockSpec((1, B_TQ, 256), lambda i, t: (i, t, P_GB // 256)),
        ],
        out_specs=pl.BlockSpec((1, B_TQ, 256), lambda i, t: (i, t, 0)),
        out_shape=jax.ShapeDtypeStruct((b, s, GROUP), BF16),
        scratch_shapes=[pltpu.VMEM((n_units, 2 * TQ, 256), BF16),
                        pltpu.VMEM((GROUP, B_TK), BF16),
                        pltpu.VMEM((2, B_TK, 2 * TQ), F32),
                        pltpu.VMEM((2, B_TK, 2 * TQ), F32),
                        pltpu.VMEM((n_units, 1, 2 * TQ), F32),
                        pltpu.VMEM((n_units, 1, 2 * TQ), F32),
                        pltpu.VMEM((n_units, 128, 2 * TQ), F32)],
        compiler_params=pltpu.CompilerParams(
            dimension_semantics=("parallel", "arbitrary"), vmem_limit_bytes=VMEM_LIMIT),
        name="attn_b",
    )(p3, p3, p3, p3, p3, p3)


def _residual_norm(ya_ref, yb_ref, yc_ref, ym_ref, x_ref, w_ref, g_ref, b_ref, rows=slice(None)):
    y = (_dot(ya_ref[rows, :], w_ref[0:256, :]) + _dot(yb_ref[rows, :], w_ref[256:512, :])
         + _dot(yc_ref[rows, :], w_ref[512:768, :]) + _dot(ym_ref[rows, :], w_ref[768:1024, :]))
    z = ALPHA * x_ref[rows, :] + y
    mu = jnp.mean(z, axis=-1, keepdims=True)
    zc = z - mu
    var = jnp.mean(zc * zc, axis=-1, keepdims=True)
    return zc * lax.rsqrt(var + 1e-5) * g_ref[...] + b_ref[...]


def _outproj_kernel(ya_ref, yb_ref, yc_ref, ym_ref, x_ref, w_ref, g_ref, b_ref, o_ref):
    for h in range(2):
        rows = pl.ds(h * (TM // 2), TM // 2)
        o_ref[rows, :] = _residual_norm(ya_ref, yb_ref, yc_ref, ym_ref, x_ref, w_ref, g_ref, b_ref, rows)


def _out_in_proj_kernel(ya_ref, yb_ref, yc_ref, ym_ref, x_ref, wo_ref, g_ref, b_ref,
                        tab_ref, w_ref, wuq_ref, wukv_ref, gq_ref, gkv_ref, o_ref, p_ref):
    halves = [pl.ds(h * (TM // 2), TM // 2) for h in range(2)]
    xb = []
    for rows in halves:
        xn = _residual_norm(ya_ref, yb_ref, yc_ref, ym_ref, x_ref, wo_ref, g_ref, b_ref, rows)
        o_ref[rows, :] = xn
        xb.append(xn.astype(BF16))
    for rows, x in zip(halves, xb):
        _project(x, tab_ref, w_ref, wuq_ref, wukv_ref, gq_ref, gkv_ref, p_ref, rows)


def _outproj(ya, yb, yc, ym, x2d, w, g, bias):
    n = x2d.shape[0]
    ytile = pl.BlockSpec((TM, GROUP), lambda i: (i, 0))
    const = lambda shape: pl.BlockSpec(shape, lambda i: (0,) * len(shape))
    return pl.pallas_call(
        _outproj_kernel,
        grid=(n // TM,),
        in_specs=[ytile, ytile, ytile, ytile,
                  pl.BlockSpec((TM, D_MODEL), lambda i: (i, 0)),
                  const((D_MODEL, D_MODEL)), const((1, D_MODEL)), const((1, D_MODEL))],
        out_specs=pl.BlockSpec((TM, D_MODEL), lambda i: (i, 0)),
        out_shape=jax.ShapeDtypeStruct((n, D_MODEL), F32),
        compiler_params=pltpu.CompilerParams(
            dimension_semantics=("parallel",), vmem_limit_bytes=VMEM_LIMIT),
        name="outproj",
    )(ya, yb, yc, ym, x2d, w, g, bias)


def _out_in_proj(ya, yb, yc, ym, x2d, wo, g, bias, tab, w, wuq, wukv, gq, gkv):
    n = x2d.shape[0]
    ytile = pl.BlockSpec((TM, GROUP), lambda i: (i, 0))
    const = lambda shape: pl.BlockSpec(shape, lambda i: (0,) * len(shape))
    return pl.pallas_call(
        _out_in_proj_kernel,
        grid=(n // TM,),
        in_specs=[ytile, ytile, ytile, ytile,
                  pl.BlockSpec((TM, D_MODEL), lambda i: (i, 0)),
                  const((D_MODEL, D_MODEL)), const((1, D_MODEL)), const((1, D_MODEL)),
                  pl.BlockSpec((TM, 512), lambda i: (i, 0)),
                  const((D_MODEL, W_WIDTH)), const((256, 512)), const((128, 512)),
                  const((1, 256)), const((1, 128))],
        out_specs=[pl.BlockSpec((TM, D_MODEL), lambda i: (i, 0)),
                   pl.BlockSpec((TM, P_WIDTH), lambda i: (i, 0))],
        out_shape=[jax.ShapeDtypeStruct((n, D_MODEL), F32),
                   jax.ShapeDtypeStruct((n, P_WIDTH), BF16)],
        compiler_params=pltpu.CompilerParams(
            dimension_semantics=("parallel",), vmem_limit_bytes=VMEM_LIMIT),
        name="out_in_proj",
    )(ya, yb, yc, ym, x2d, wo, g, bias, tab, w, wuq, wukv, gq, gkv)


def kernel(x, mem, positions, w_in, rel_bias, mla_q_norm, w_uq, mla_kv_norm, w_ukv,
           swa_sinks, w_mem_kv, w_out, ln_gain, ln_bias):
    b, s, d = x.shape
    depth = w_in.shape[0]
    assert d == D_MODEL and depth == DEPTH and s % B_TK == 0 and s >= A_WIN and (b * s) % TM == 0

    cols, cperm = _inproj_cols()
    w_in_p = _take_cols(w_in, cols).astype(BF16)
    wuq_p = jnp.pad(_take_cols(w_uq, _uq_cols()), ((0, 0), (0, 256 - MLA_Q_RANK), (0, 0))).astype(BF16)
    wukv_p = _take_cols(w_ukv, _ukv_cols()).astype(BF16)
    gq = jnp.pad(mla_q_norm, ((0, 0), (0, 256 - MLA_Q_RANK)))[:, None, :]
    gkv = mla_kv_norm[:, None, :]
    rows = np.concatenate([np.arange(512), 512 + cperm, np.arange(768, 1024)])
    w_out_p = _take_cols(w_out, rows, axis=1).astype(BF16)
    w_mem_all = jnp.transpose(w_mem_kv, (1, 0, 2)).reshape(D_MODEL, depth * 512).astype(BF16)
    e_a = _bias_table_a(rel_bias)
    e_c = jnp.asarray(_mask_table_c())
    tab = _rope_tables(positions)

    memkv = _memkv(mem, w_mem_all)
    h = x.reshape(b * s, d)
    p2 = _inproj(h, tab, w_in_p[0], wuq_p[0], wukv_p[0], gq[0], gkv[0])
    for l in range(depth):
        p3 = p2.reshape(b, s, P_WIDTH)
        ya = _window_attn(p3, P_AQ, p3, P_AK, p3, P_AV, P_GA, win=A_WIN, prev=A_PREV * CHUNK, dk=256, table=e_a[l])
        yb = _attn_b(p3)
        yc = _window_attn(p3, P_CQ, p3, P_CK, p3, P_CV, P_GC, win=C_WIN, prev=SWA_PREV * CHUNK, dk=128,
                          table=e_c, sinks=swa_sinks[l])
        ym = _window_attn(p3, P_MQ, memkv, 512 * l, memkv, 512 * l + 256, P_GM, win=MEM_LEN, prev=None, dk=256)
        ys = [y.reshape(b * s, GROUP) for y in (ya, yb, yc, ym)]
        ln = (ln_gain[l][None, :], ln_bias[l][None, :])
        if l + 1 < depth:
            h, p2 = _out_in_proj(*ys, h, w_out_p[l], *ln, tab, w_in_p[l + 1], wuq_p[l + 1], wukv_p[l + 1],
                                 gq[l + 1], gkv[l + 1])
        else:
            h = _outproj(*ys, h, w_out_p[l], *ln)
    return h.reshape(b, s, d)
```

```python
import functools

import numpy as np
import jax
import jax.numpy as jnp
from jax import lax
from jax.experimental import pallas as pl
from jax.experimental.pallas import tpu as pltpu

F32 = jnp.float32
BF16 = jnp.bfloat16

D_MODEL = 1024
DEPTH = 4
CHUNK = 64
HEAD_DIM = 64
GROUP = 256
N_HEADS = 4
ROPE_THETA = 10000.0
NEG_INF = -1e30
A_PREV = 8
REL_CLIP = 128
MLA_NOPE = 64
MLA_ROPE = 32
MLA_Q_RANK = 192
MLA_KV_RANK = 128
SWA_PREV = 2
MEM_LEN = 256
ALPHA = (2.0 * DEPTH) ** 0.25

TQ = 128
A_WIN = TQ + A_PREV * CHUNK
C_WIN = TQ + SWA_PREV * CHUNK
WIN_UNROLL = 8
B_TK = 512
B_TQ = 512
TM = 512
VMEM_LIMIT = 56 * 1024 * 1024
LOG2E = 1.4426950408889634
QSCALE = HEAD_DIM ** -0.5 * LOG2E
B_QSCALE = (MLA_NOPE + MLA_ROPE) ** -0.5 * LOG2E

P_AQ, P_AK, P_AV = 0, 256, 512
P_BQ0, P_BQ1, P_BK0, P_BK1, P_BV = 768, 1024, 1280, 1536, 1792
P_CQ, P_CK, P_CV = 2048, 2304, 2432
P_MQ = 2560
P_GA, P_GB, P_GC, P_GM = 2816, 3072, 3328, 3584
P_WIDTH = 3840

W_A, W_C, W_M, W_G, W_B, W_WIDTH = 0, 768, 1280, 1536, 2560, 3072

C_HEAD_ORDER = (0, 2, 1, 3)


def _inproj_cols():
    r = np.arange
    aq, ak, av, ag = 0, 256, 512, 768
    bcq, bckv, bkr, bg = 1024, 1216, 1344, 1376
    cq, ck, cv, cg = 1632, 1888, 2016, 2144
    mq, mg = 2400, 2656
    cperm = np.concatenate([r(64) + 64 * h for h in C_HEAD_ORDER])
    pad = lambda n: np.full(n, -1)
    cols = np.concatenate([
        aq + r(256), ak + r(256), av + r(256),
        cq + cperm, ck + r(128), cv + r(128),
        mq + r(256),
        ag + r(256), bg + r(256), cg + cperm, mg + r(256),
        bcq + r(192), pad(64), bckv + r(128), bkr + r(32), bkr + r(32), pad(64),
    ])
    assert cols.shape[0] == W_WIDTH
    return cols, cperm


def _take_cols(w, cols, axis=-1):
    axis = axis % w.ndim
    pieces, i = [], 0
    while i < len(cols):
        j = i + 1
        if cols[i] < 0:
            while j < len(cols) and cols[j] < 0:
                j += 1
            shape = w.shape[:axis] + (j - i,) + w.shape[axis + 1:]
            pieces.append(jnp.zeros(shape, w.dtype))
        else:
            while j < len(cols) and cols[j] == cols[j - 1] + 1:
                j += 1
            pieces.append(lax.slice_in_dim(w, int(cols[i]), int(cols[i]) + (j - i), axis=axis))
        i = j
    return jnp.concatenate(pieces, axis=axis)


def _uq_cols():
    r = np.arange
    per = MLA_NOPE + MLA_ROPE
    out = []
    for p in range(2):
        h0, h1 = 2 * p, 2 * p + 1
        out += [per * h0 + r(64), per * h1 + r(64),
                per * h0 + 64 + r(32), per * h1 + 64 + r(32), np.full(64, -1)]
    return np.concatenate(out)


def _ukv_cols():
    r = np.arange
    return np.concatenate([128 * h + r(64) for h in range(4)] + [128 * h + 64 + r(64) for h in range(4)])


def _rope_tables(positions):
    pos = positions.astype(F32).reshape(-1, 1)
    narrow, col0, expand = [], 0, np.zeros((2 * (HEAD_DIM + MLA_ROPE) // 2, 512), np.float32)
    for t, d in enumerate((HEAD_DIM, MLA_ROPE)):
        half = d // 2
        inv = ROPE_THETA ** (-jnp.arange(0, d, 2, dtype=F32) / d)
        ang = pos * inv[None, :]
        narrow += [jnp.cos(ang), jnp.sin(ang)]
        lane = np.arange(128)
        k = (lane % d) % half
        expand[col0 + k, 256 * t + lane] = 1.0
        expand[col0 + half + k, 256 * t + 128 + lane] = np.where(lane % d < half, -1.0, 1.0)
        col0 += d
    x = jnp.concatenate(narrow, axis=1)
    hi = x.astype(BF16)
    rest = x - hi.astype(F32)
    mid = rest.astype(BF16)
    lo = (rest - mid.astype(F32)).astype(BF16)
    return jnp.dot(jnp.concatenate([hi, mid, lo], axis=1), jnp.asarray(np.concatenate([expand] * 3, axis=0), BF16),
                   preferred_element_type=F32)


def _bias_table_a(rel_bias):
    width, period = 9 * 128, 9 * 128 + TQ
    k = np.arange(period)
    d = np.where(k < width, A_PREV * CHUNK - k, A_PREV * CHUNK + period - k)
    idx = np.clip(d, -REL_CLIP, REL_CLIP) + REL_CLIP
    n_hi = A_PREV * CHUNK - REL_CLIP + 1
    n_lo = width - n_hi - (2 * REL_CLIP - 1)
    expect = np.concatenate([np.full(n_hi, 2 * REL_CLIP), np.arange(2 * REL_CLIP - 1, 0, -1),
                             np.zeros(n_lo, np.int64), np.full(period - width, 2 * REL_CLIP)])
    assert np.array_equal(idx, expect)
    rep = lambda col, n: jnp.broadcast_to(rel_bias[:, :, col:col + 1], rel_bias.shape[:2] + (n,))
    gp = jnp.concatenate([rep(2 * REL_CLIP, n_hi), jnp.flip(rel_bias[:, :, 1:2 * REL_CLIP], axis=-1),
                          rep(0, n_lo), rep(2 * REL_CLIP, period - width)], axis=-1) * LOG2E
    flat = jnp.tile(gp, (1, 1, TQ))[:, :, :TQ * (period - 1)]
    skew = flat.reshape(gp.shape[0], N_HEADS, TQ, period - 1)[..., :width]
    i = np.arange(TQ)[:, None]
    m = np.arange(width)[None, :]
    dchunk = i // CHUNK + A_PREV - m // CHUNK
    valid = (dchunk >= 0) & (dchunk <= A_PREV)
    t = jnp.where(jnp.asarray(valid)[None, None], skew, NEG_INF)
    return jnp.transpose(t, (0, 3, 1, 2)).reshape(gp.shape[0], 9, 128, N_HEADS * TQ)


def _mask_table_c():
    m = np.arange(3 * 128)[:, None]
    i = np.arange(TQ)[None, :]
    dchunk = i // CHUNK + SWA_PREV - m // CHUNK
    valid = (dchunk >= 0) & (dchunk <= SWA_PREV)
    t = np.where(valid, 0.0, NEG_INF).astype(np.float32)
    return np.tile(t, (1, N_HEADS)).reshape(3, 128, N_HEADS * TQ)


def _dot(a, b):
    return jnp.dot(a, b, preferred_element_type=F32)


def _dot_nt(a, b):
    return lax.dot_general(a, b, (((1,), (1,)), ((), ())), preferred_element_type=F32)


def _dot_tn(a, b):
    return lax.dot_general(a, b, (((0,), (0,)), ((), ())), preferred_element_type=F32)


def _rope(x, cos, sin_signed, half):
    lane = lax.broadcasted_iota(jnp.int32, x.shape, 1)
    first = (lane & (2 * half - 1)) < half
    swapped = jnp.where(first, pltpu.roll(x, 128 - half, 1), pltpu.roll(x, half, 1))
    return x * cos + swapped * sin_signed


def _project(xb, tab_ref, w_ref, wuq_ref, wukv_ref, gq_ref, gkv_ref, p_ref, rows=slice(None)):
    tab_ref, p_ref = tab_ref.at[rows], p_ref.at[rows]
    cos64, sin64 = tab_ref[:, 0:128], tab_ref[:, 128:256]
    cos32, sin32 = tab_ref[:, 256:384], tab_ref[:, 384:512]

    def mm(lo, hi):
        return _dot(xb, w_ref[:, lo:hi])

    rb = mm(W_B, W_B + 512)
    cq = rb[:, 0:256]
    ms = jnp.sum(cq * cq, axis=-1, keepdims=True) * (1.0 / MLA_Q_RANK)
    qn = (cq * lax.rsqrt(ms + 1e-6) * gq_ref[...]).astype(BF16)
    ckv = rb[:, 256:384]
    ms = jnp.mean(ckv * ckv, axis=-1, keepdims=True)
    kvn = (ckv * lax.rsqrt(ms + 1e-6) * gkv_ref[...]).astype(BF16)
    krb = _rope(rb[:, 384:512], cos32, sin32, 16).astype(BF16)

    r = mm(W_A, W_A + 768)
    p_ref[:, P_AQ:P_AQ + 256] = (r[:, 0:256] * QSCALE).astype(BF16)
    p_ref[:, P_AK:P_AK + 512] = r[:, 256:768].astype(BF16)

    q = _dot(qn, wuq_ref[...]) * B_QSCALE
    for p in range(2):
        base = P_BQ0 + 256 * p
        p_ref[:, base:base + 128] = q[:, 256 * p:256 * p + 128].astype(BF16)
        p_ref[:, base + 128:base + 256] = _rope(q[:, 256 * p + 128:256 * p + 256], cos32, sin32, 16).astype(BF16)
    kv = _dot(kvn, wukv_ref[...])
    for p in range(2):
        base = P_BK0 + 256 * p
        p_ref[:, base:base + 128] = kv[:, 128 * p:128 * (p + 1)].astype(BF16)
        p_ref[:, base + 128:base + 256] = krb
    p_ref[:, P_BV:P_BV + 256] = kv[:, 256:512].astype(BF16)

    r = mm(W_G, W_G + 1024)
    p_ref[:, P_GA:P_GA + 1024] = (r * (1.0 / (1.0 + jnp.exp(-r)))).astype(BF16)

    r = mm(W_C, W_C + 512)
    for j in range(2):
        qj = _rope(r[:, 128 * j:128 * (j + 1)], cos64, sin64, 32)
        p_ref[:, P_CQ + 128 * j:P_CQ + 128 * (j + 1)] = (qj * QSCALE).astype(BF16)
    p_ref[:, P_CK:P_CK + 128] = _rope(r[:, 256:384], cos64, sin64, 32).astype(BF16)
    p_ref[:, P_CV:P_CV + 128] = r[:, 384:512].astype(BF16)

    r = mm(W_M, W_M + 256)
    p_ref[:, P_MQ:P_MQ + 256] = (r * QSCALE).astype(BF16)


def _inproj_kernel(x_ref, tab_ref, w_ref, wuq_ref, wukv_ref, gq_ref, gkv_ref, p_ref):
    _project(x_ref[...].astype(BF16), tab_ref, w_ref, wuq_ref, wukv_ref, gq_ref, gkv_ref, p_ref)


def _inproj(x2d, tab, w, wuq, wukv, gq, gkv):
    n = x2d.shape[0]
    const = lambda shape: pl.BlockSpec(shape, lambda i: (0,) * len(shape))
    return pl.pallas_call(
        _inproj_kernel,
        grid=(n // TM,),
        in_specs=[
            pl.BlockSpec((TM, D_MODEL), lambda i: (i, 0)),
            pl.BlockSpec((TM, 512), lambda i: (i, 0)),
            const((D_MODEL, W_WIDTH)),
            const((256, 512)),
            const((128, 512)),
            const((1, 256)),
            const((1, 128)),
        ],
        out_specs=pl.BlockSpec((TM, P_WIDTH), lambda i: (i, 0)),
        out_shape=jax.ShapeDtypeStruct((n, P_WIDTH), BF16),
        compiler_params=pltpu.CompilerParams(
            dimension_semantics=("parallel",), vmem_limit_bytes=VMEM_LIMIT),
        name="inproj",
    )(x2d, tab, w, wuq, wukv, gq, gkv)


def _memkv_kernel(mem_ref, w_ref, o_ref):
    o_ref[0] = _dot(mem_ref[0].astype(BF16), w_ref[...]).astype(BF16)


def _memkv(mem, w_all):
    b = mem.shape[0]
    n = w_all.shape[1]
    return pl.pallas_call(
        _memkv_kernel,
        grid=(b,),
        in_specs=[pl.BlockSpec((1, MEM_LEN, D_MODEL), lambda i: (i, 0, 0)),
                  pl.BlockSpec((D_MODEL, n), lambda i: (0, 0))],
        out_specs=pl.BlockSpec((1, MEM_LEN, n), lambda i: (i, 0, 0)),
        out_shape=jax.ShapeDtypeStruct((b, MEM_LEN, n), BF16),
        compiler_params=pltpu.CompilerParams(
            dimension_semantics=("parallel",), vmem_limit_bytes=VMEM_LIMIT),
        name="memkv",
    )(mem, w_all)


def _window_attn_kernel(*refs, win, prev, dk, has_table, has_sink):
    refs = list(refs)
    sink_ref = refs.pop(0) if has_sink else None
    q_ref, k_ref, v_ref, g_ref = refs[:4]
    e_ref = refs[4] if has_table else None
    o_ref, sa_sc, sb_sc = refs[-3:]
    n_items = q_ref.shape[1] // TQ
    lanes = N_HEADS * TQ

    lane128 = lax.broadcasted_iota(jnp.int32, (TQ, 128), 1)
    lo, hi = lane128 < HEAD_DIM, lane128 >= HEAD_DIM
    if has_sink:
        col = lax.broadcasted_iota(jnp.int32, (1, lanes), 1)
        order = C_HEAD_ORDER if dk == 128 else tuple(range(N_HEADS))
        sink = jnp.where(col < TQ, sink_ref[order[0]],
                         jnp.where(col < 2 * TQ, sink_ref[order[1]],
                                   jnp.where(col < 3 * TQ, sink_ref[order[2]], sink_ref[order[3]]))) * LOG2E

    def window_start(item):
        if prev is None:
            return 0
        return pl.multiple_of(jnp.maximum(item * TQ - prev, 0), 128)

    def scores_into(item, s_sc):
        q = q_ref[0, pl.ds(pl.multiple_of(item * TQ, TQ), TQ), :].astype(F32)
        if dk == 256:
            zero = jnp.zeros((TQ, 128), F32)
            blocks = [jnp.concatenate([jnp.where(lo, q[:, 0:128], 0.0), zero], axis=1),
                      jnp.concatenate([jnp.where(hi, q[:, 0:128], 0.0), zero], axis=1),
                      jnp.concatenate([zero, jnp.where(lo, q[:, 128:256], 0.0)], axis=1),
                      jnp.concatenate([zero, jnp.where(hi, q[:, 128:256], 0.0)], axis=1)]
        else:
            blocks = [jnp.where(lo, q[:, 0:128], 0.0), jnp.where(hi, q[:, 0:128], 0.0),
                      jnp.where(lo, q[:, 128:256], 0.0), jnp.where(hi, q[:, 128:256], 0.0)]
        qs = jnp.concatenate(blocks, axis=0).astype(BF16)
        s = _dot_nt(k_ref[0, pl.ds(window_start(item), win), :], qs)
        if has_table:
            mb0 = jnp.maximum(prev // 128 - item, 0)
            s = s + jnp.concatenate([e_ref[mb0 + jb] for jb in range(win // 128)], axis=0)
        s_sc[...] = s

    def consume(item, s_sc):
        start = window_start(item)
        m = jnp.max(s_sc[...], axis=0, keepdims=True)
        if has_sink:
            m = jnp.maximum(m, sink)
        p = jnp.exp2(s_sc[...] - m)
        l = jnp.sum(p, axis=0, keepdims=True)
        if has_sink:
            l = l + jnp.exp2(sink - m)
        inv = 1.0 / l
        pb = p.astype(BF16)
        v = v_ref[0, pl.ds(start, win), :]
        parts = []
        for pr in range(2):
            vp = v[:, 128 * pr:128 * (pr + 1)] if dk == 256 else v
            ot = _dot_tn(vp, pb[:, 256 * pr:256 * (pr + 1)])
            for e in range(2):
                h = 2 * pr + e
                parts.append(ot[64 * e:64 * (e + 1), 128 * e:128 * (e + 1)] * inv[:, TQ * h:TQ * (h + 1)])
        o = jnp.concatenate(parts, axis=0).T
        rows = pl.ds(pl.multiple_of(item * TQ, TQ), TQ)
        o_ref[0, rows, :] = (o * g_ref[0, rows, :].astype(F32)).astype(BF16)

    bufs = (sa_sc, sb_sc)
    scores_into(0, bufs[0])

    def body(i, carry):
        for j in range(WIN_UNROLL):
            item = WIN_UNROLL * i + j
            scores_into(jnp.minimum(item + 1, n_items - 1), bufs[(j + 1) % 2])
            consume(item, bufs[j % 2])
        return carry

    lax.fori_loop(0, n_items // WIN_UNROLL, body, 0)


def _window_attn(q_src, q_col, k_src, k_col, v_src, v_col, g_col, *, win, prev, dk, table=None, sinks=None):
    b, s, _ = q_src.shape
    skv = k_src.shape[1]
    assert s % (WIN_UNROLL * TQ) == 0 and skv >= win
    kern = functools.partial(_window_attn_kernel, win=win, prev=prev, dk=dk,
                             has_table=table is not None, has_sink=sinks is not None)
    in_specs, args = [], []
    if sinks is not None:
        in_specs.append(pl.BlockSpec(memory_space=pltpu.SMEM)); args.append(sinks)
    in_specs += [pl.BlockSpec((1, s, 256), lambda i: (i, 0, q_col // 256)),
                 pl.BlockSpec((1, skv, dk), lambda i: (i, 0, k_col // dk)),
                 pl.BlockSpec((1, skv, dk), lambda i: (i, 0, v_col // dk)),
                 pl.BlockSpec((1, s, 256), lambda i: (i, 0, g_col // 256))]
    args += [q_src, k_src, v_src, q_src]
    if table is not None:
        in_specs.append(pl.BlockSpec(table.shape, lambda i: (0, 0, 0))); args.append(table)
    return pl.pallas_call(
        kern,
        grid=(b,),
        in_specs=in_specs,
        out_specs=pl.BlockSpec((1, s, 256), lambda i: (i, 0, 0)),
        out_shape=jax.ShapeDtypeStruct((b, s, GROUP), BF16),
        scratch_shapes=[pltpu.VMEM((win, N_HEADS * TQ), F32), pltpu.VMEM((win, N_HEADS * TQ), F32)],
        compiler_params=pltpu.CompilerParams(
            dimension_semantics=("parallel",), vmem_limit_bytes=VMEM_LIMIT),
        name="attn_win%d" % win,
    )(*args)


def _attn_b_kernel(q0_ref, q1_ref, k0_ref, k1_ref, v_ref, g_ref, o_ref,
                   qs_sc, vt_sc, sa_sc, sb_sc, m_sc, l_sc, acc_sc):
    t = pl.program_id(1)
    n_tiles = B_TQ // TQ
    lane = lax.broadcasted_iota(jnp.int32, (TQ, 128), 1)
    q_refs = (q0_ref, q1_ref)
    k_refs = (k0_ref, k1_ref)

    def stack_queries(c):
        stacked = []
        for pr in range(2):
            q = q_refs[pr][0, TQ * c:TQ * (c + 1), :].astype(F32)
            nope, rope = q[:, 0:128], q[:, 128:256]
            head_a = jnp.concatenate([jnp.where(lane < 64, nope, 0.0), jnp.where(lane < 32, rope, 0.0)], axis=1)
            head_b = jnp.concatenate([jnp.where(lane >= 64, nope, 0.0), jnp.where(lane >= 32, rope, 0.0)], axis=1)
            qs = jnp.concatenate([head_a, head_b], axis=0).astype(BF16)
            qs_sc[2 * c + pr] = qs
            stacked.append(qs)
        return stacked

    lane2 = lax.broadcasted_iota(jnp.int32, (1, 2 * TQ), 1)
    hide_first_chunk = jnp.where((lane2 & (TQ - 1)) < CHUNK, NEG_INF, 0.0)

    def scores_into(kb, c, s_sc, nk=B_TK, qs=None):
        start = pl.multiple_of(kb * B_TK, B_TK)
        for pr in range(2):
            q = qs_sc[2 * c + pr] if qs is None else qs[pr]
            s_sc[pr, 0:nk, :] = _dot_nt(k_refs[pr][0, pl.ds(start, nk), :], q)

    def transpose_values(kb):
        vt_sc[...] = v_ref[0, pl.ds(pl.multiple_of(kb * B_TK, B_TK), B_TK), :].T

    def consume(kb, c, s_sc, nk=B_TK, diagonal=False):
        for pr in range(2):
            u = 2 * c + pr
            s = s_sc[pr, 0:nk, :]
            if diagonal:
                s = jnp.concatenate([s[:nk - CHUNK], s[nk - CHUNK:] + hide_first_chunk], axis=0)
            m_prev = m_sc[u]
            m_new = jnp.maximum(m_prev, jnp.max(s, axis=0, keepdims=True))
            alpha = jnp.exp2(m_prev - m_new)
            p = jnp.exp2(s - m_new)
            l_sc[u] = alpha * l_sc[u] + jnp.sum(p, axis=0, keepdims=True)
            m_sc[u] = m_new
            acc_sc[u] = alpha * acc_sc[u] + _dot(vt_sc[128 * pr:128 * (pr + 1), 0:nk], p.astype(BF16))

    bufs = (sa_sc, sb_sc)
    scores_into(0, 0, bufs[0], qs=stack_queries(0))
    for c in range(1, n_tiles):
        stack_queries(c)
    m_sc[...] = jnp.full(m_sc.shape, NEG_INF, F32)
    l_sc[...] = jnp.zeros(l_sc.shape, F32)
    acc_sc[...] = jnp.zeros(acc_sc.shape, F32)

    def body(kb, carry):
        transpose_values(kb)
        for c in range(n_tiles):
            if c + 1 < n_tiles:
                scores_into(kb, c + 1, bufs[(c + 1) % 2])
            else:
                scores_into(kb + 1, 0, bufs[0])
            consume(kb, c, bufs[c % 2])
        return carry

    lax.fori_loop(0, t, body, 0)

    transpose_values(t)
    for c in range(n_tiles):
        if c + 1 < n_tiles:
            scores_into(t, c + 1, bufs[(c + 1) % 2], nk=TQ * (c + 2))
        consume(t, c, bufs[c % 2], nk=TQ * (c + 1), diagonal=True)

    for c in range(n_tiles):
        parts = []
        for pr in range(2):
            u = 2 * c + pr
            inv = 1.0 / l_sc[u]
            for e in range(2):
                parts.append(acc_sc[u, 64 * e:64 * (e + 1), 128 * e:128 * (e + 1)] * inv[:, 128 * e:128 * (e + 1)])
        o = jnp.concatenate(parts, axis=0).T
        o_ref[0, TQ * c:TQ * (c + 1), :] = (o * g_ref[0, TQ * c:TQ * (c + 1), :].astype(F32)).astype(BF16)


def _attn_b(p3):
    b, s, _ = p3.shape
    n_units = 2 * (B_TQ // TQ)
    return pl.pallas_call(
        _attn_b_kernel,
        grid=(b, s // B_TQ),
        in_specs=[
            pl.BlockSpec((1, B_TQ, 256), lambda i, t: (i, t, P_BQ0 // 256)),
            pl.BlockSpec((1, B_TQ, 256), lambda i, t: (i, t, P_BQ1 // 256)),
            pl.BlockSpec((1, s, 256), lambda i, t: (i, 0, P_BK0 // 256)),
            pl.BlockSpec((1, s, 256), lambda i, t: (i, 0, P_BK1 // 256)),
            pl.BlockSpec((1, s, 256), lambda i, t: (i, 0, P_BV // 256)),
            pl.BlockSpec((1, B_TQ, 256), lambda i, t: (i, t, P_GB // 256)),
        ],
        out_specs=pl.BlockSpec((1, B_TQ, 256), lambda i, t: (i, t, 0)),
        out_shape=jax.ShapeDtypeStruct((b, s, GROUP), BF16),
        scratch_shapes=[pltpu.VMEM((n_units, 2 * TQ, 256), BF16),
                        pltpu.VMEM((GROUP, B_TK), BF16),
                        pltpu.VMEM((2, B_TK, 2 * TQ), F32),
                        pltpu.VMEM((2, B_TK, 2 * TQ), F32),
                        pltpu.VMEM((n_units, 1, 2 * TQ), F32),
                        pltpu.VMEM((n_units, 1, 2 * TQ), F32),
                        pltpu.VMEM((n_units, 128, 2 * TQ), F32)],
        compiler_params=pltpu.CompilerParams(
            dimension_semantics=("parallel", "arbitrary"), vmem_limit_bytes=VMEM_LIMIT),
        name="attn_b",
    )(p3, p3, p3, p3, p3, p3)


def _residual_norm(ya_ref, yb_ref, yc_ref, ym_ref, x_ref, w_ref, g_ref, b_ref, rows=slice(None)):
    y = (_dot(ya_ref[rows, :], w_ref[0:256, :]) + _dot(yb_ref[rows, :], w_ref[256:512, :])
         + _dot(yc_ref[rows, :], w_ref[512:768, :]) + _dot(ym_ref[rows, :], w_ref[768:1024, :]))
    z = ALPHA * x_ref[rows, :] + y
    mu = jnp.mean(z, axis=-1, keepdims=True)
    zc = z - mu
    var = jnp.mean(zc * zc, axis=-1, keepdims=True)
    return zc * lax.rsqrt(var + 1e-5) * g_ref[...] + b_ref[...]


def _outproj_kernel(ya_ref, yb_ref, yc_ref, ym_ref, x_ref, w_ref, g_ref, b_ref, o_ref):
    for h in range(2):
        rows = pl.ds(h * (TM // 2), TM // 2)
        o_ref[rows, :] = _residual_norm(ya_ref, yb_ref, yc_ref, ym_ref, x_ref, w_ref, g_ref, b_ref, rows)


def _out_in_proj_kernel(ya_ref, yb_ref, yc_ref, ym_ref, x_ref, wo_ref, g_ref, b_ref,
                        tab_ref, w_ref, wuq_ref, wukv_ref, gq_ref, gkv_ref, o_ref, p_ref):
    halves = [pl.ds(h * (TM // 2), TM // 2) for h in range(2)]
    xb = []
    for rows in halves:
        xn = _residual_norm(ya_ref, yb_ref, yc_ref, ym_ref, x_ref, wo_ref, g_ref, b_ref, rows)
        o_ref[rows, :] = xn
        xb.append(xn.astype(BF16))
    for rows, x in zip(halves, xb):
        _project(x, tab_ref, w_ref, wuq_ref, wukv_ref, gq_ref, gkv_ref, p_ref, rows)


def _outproj(ya, yb, yc, ym, x2d, w, g, bias):
    n = x2d.shape[0]
    ytile = pl.BlockSpec((TM, GROUP), lambda i: (i, 0))
    const = lambda shape: pl.BlockSpec(shape, lambda i: (0,) * len(shape))
    return pl.pallas_call(
        _outproj_kernel,
        grid=(n // TM,),
        in_specs=[ytile, ytile, ytile, ytile,
                  pl.BlockSpec((TM, D_MODEL), lambda i: (i, 0)),
                  const((D_MODEL, D_MODEL)), const((1, D_MODEL)), const((1, D_MODEL))],
        out_specs=pl.BlockSpec((TM, D_MODEL), lambda i: (i, 0)),
        out_shape=jax.ShapeDtypeStruct((n, D_MODEL), F32),
        compiler_params=pltpu.CompilerParams(
            dimension_semantics=("parallel",), vmem_limit_bytes=VMEM_LIMIT),
        name="outproj",
    )(ya, yb, yc, ym, x2d, w, g, bias)


def _out_in_proj(ya, yb, yc, ym, x2d, wo, g, bias, tab, w, wuq, wukv, gq, gkv):
    n = x2d.shape[0]
    ytile = pl.BlockSpec((TM, GROUP), lambda i: (i, 0))
    const = lambda shape: pl.BlockSpec(shape, lambda i: (0,) * len(shape))
    return pl.pallas_call(
        _out_in_proj_kernel,
        grid=(n // TM,),
        in_specs=[ytile, ytile, ytile, ytile,
                  pl.BlockSpec((TM, D_MODEL), lambda i: (i, 0)),
                  const((D_MODEL, D_MODEL)), const((1, D_MODEL)), const((1, D_MODEL)),
                  pl.BlockSpec((TM, 512), lambda i: (i, 0)),
                  const((D_MODEL, W_WIDTH)), const((256, 512)), const((128, 512)),
                  const((1, 256)), const((1, 128))],
        out_specs=[pl.BlockSpec((TM, D_MODEL), lambda i: (i, 0)),
                   pl.BlockSpec((TM, P_WIDTH), lambda i: (i, 0))],
        out_shape=[jax.ShapeDtypeStruct((n, D_MODEL), F32),
                   jax.ShapeDtypeStruct((n, P_WIDTH), BF16)],
        compiler_params=pltpu.CompilerParams(
            dimension_semantics=("parallel",), vmem_limit_bytes=VMEM_LIMIT),
        name="out_in_proj",
    )(ya, yb, yc, ym, x2d, wo, g, bias, tab, w, wuq, wukv, gq, gkv)


def kernel(x, mem, positions, w_in, rel_bias, mla_q_norm, w_uq, mla_kv_norm, w_ukv,
           swa_sinks, w_mem_kv, w_out, ln_gain, ln_bias):
    b, s, d = x.shape
    depth = w_in.shape[0]
    assert d == D_MODEL and depth == DEPTH and s % B_TK == 0 and s >= A_WIN and (b * s) % TM == 0

    cols, cperm = _inproj_cols()
    w_in_p = _take_cols(w_in, cols).astype(BF16)
    wuq_p = jnp.pad(_take_cols(w_uq, _uq_cols()), ((0, 0), (0, 256 - MLA_Q_RANK), (0, 0))).astype(BF16)
    wukv_p = _take_cols(w_ukv, _ukv_cols()).astype(BF16)
    gq = jnp.pad(mla_q_norm, ((0, 0), (0, 256 - MLA_Q_RANK)))[:, None, :]
    gkv = mla_kv_norm[:, None, :]
    rows = np.concatenate([np.arange(512), 512 + cperm, np.arange(768, 1024)])
    w_out_p = _take_cols(w_out, rows, axis=1).astype(BF16)
    w_mem_all = jnp.transpose(w_mem_kv, (1, 0, 2)).reshape(D_MODEL, depth * 512).astype(BF16)
    e_a = _bias_table_a(rel_bias)
    e_c = jnp.asarray(_mask_table_c())
    tab = _rope_tables(positions)

    memkv = _memkv(mem, w_mem_all)
    h = x.reshape(b * s, d)
    p2 = _inproj(h, tab, w_in_p[0], wuq_p[0], wukv_p[0], gq[0], gkv[0])
    for l in range(depth):
        p3 = p2.reshape(b, s, P_WIDTH)
        ya = _window_attn(p3, P_AQ, p3, P_AK, p3, P_AV, P_GA, win=A_WIN, prev=A_PREV * CHUNK, dk=256, table=e_a[l])
        yb = _attn_b(p3)
        yc = _window_attn(p3, P_CQ, p3, P_CK, p3, P_CV, P_GC, win=C_WIN, prev=SWA_PREV * CHUNK, dk=128,
                          table=e_c, sinks=swa_sinks[l])
        ym = _window_attn(p3, P_MQ, memkv, 512 * l, memkv, 512 * l + 256, P_GM, win=MEM_LEN, prev=None, dk=256)
        ys = [y.reshape(b * s, GROUP) for y in (ya, yb, yc, ym)]
        ln = (ln_gain[l][None, :], ln_bias[l][None, :])
        if l + 1 < depth:
            h, p2 = _out_in_proj(*ys, h, w_out_p[l], *ln, tab, w_in_p[l + 1], wuq_p[l + 1], wukv_p[l + 1],
                                 gq[l + 1], gkv[l + 1])
        else:
            h = _outproj(*ys, h, w_out_p[l], *ln)
    return h.reshape(b, s, d)
```

```python
import functools

import numpy as np
import jax
import jax.numpy as jnp
from jax import lax
from jax.experimental import pallas as pl
from jax.experimental.pallas import tpu as pltpu

F32 = jnp.float32
BF16 = jnp.bfloat16

D_MODEL = 1024
DEPTH = 4
CHUNK = 64
HEAD_DIM = 64
GROUP = 256
N_HEADS = 4
ROPE_THETA = 10000.0
NEG_INF = -1e30
A_PREV = 8
REL_CLIP = 128
MLA_NOPE = 64
MLA_ROPE = 32
MLA_Q_RANK = 192
MLA_KV_RANK = 128
SWA_PREV = 2
MEM_LEN = 256
ALPHA = (2.0 * DEPTH) ** 0.25

TQ = 128
A_WIN = TQ + A_PREV * CHUNK
C_WIN = TQ + SWA_PREV * CHUNK
WIN_UNROLL = 8
B_TK = 512
B_TQ = 512
TM = 512
VMEM_LIMIT = 56 * 1024 * 1024
LOG2E = 1.4426950408889634
QSCALE = HEAD_DIM ** -0.5 * LOG2E
B_QSCALE = (MLA_NOPE + MLA_ROPE) ** -0.5 * LOG2E

P_AQ, P_AK, P_AV = 0, 256, 512
P_BQ0, P_BQ1, P_BK0, P_BK1, P_BV = 768, 1024, 1280, 1536, 1792
P_CQ, P_CK, P_CV = 2048, 2304, 2432
P_MQ = 2560
P_GA, P_GB, P_GC, P_GM = 2816, 3072, 3328, 3584
P_WIDTH = 3840

W_A, W_C, W_M, W_G, W_B, W_WIDTH = 0, 768, 1280, 1536, 2560, 3072

C_HEAD_ORDER = (0, 2, 1, 3)


def _inproj_cols():
    r = np.arange
    aq, ak, av, ag = 0, 256, 512, 768
    bcq, bckv, bkr, bg = 1024, 1216, 1344, 1376
    cq, ck, cv, cg = 1632, 1888, 2016, 2144
    mq, mg = 2400, 2656
    cperm = np.concatenate([r(64) + 64 * h for h in C_HEAD_ORDER])
    pad = lambda n: np.full(n, -1)
    cols = np.concatenate([
        aq + r(256), ak + r(256), av + r(256),
        cq + cperm, ck + r(128), cv + r(128),
        mq + r(256),
        ag + r(256), bg + r(256), cg + cperm, mg + r(256),
        bcq + r(192), pad(64), bckv + r(128), bkr + r(32), bkr + r(32), pad(64),
    ])
    assert cols.shape[0] == W_WIDTH
    return cols, cperm


def _take_cols(w, cols, axis=-1):
    axis = axis % w.ndim
    pieces, i = [], 0
    while i < len(cols):
        j = i + 1
        if cols[i] < 0:
            while j < len(cols) and cols[j] < 0:
                j += 1
            shape = w.shape[:axis] + (j - i,) + w.shape[axis + 1:]
            pieces.append(jnp.zeros(shape, w.dtype))
        else:
            while j < len(cols) and cols[j] == cols[j - 1] + 1:
                j += 1
            pieces.append(lax.slice_in_dim(w, int(cols[i]), int(cols[i]) + (j - i), axis=axis))
        i = j
    return jnp.concatenate(pieces, axis=axis)


def _uq_cols():
    r = np.arange
    per = MLA_NOPE + MLA_ROPE
    out = []
    for p in range(2):
        h0, h1 = 2 * p, 2 * p + 1
        out += [per * h0 + r(64), per * h1 + r(64),
                per * h0 + 64 + r(32), per * h1 + 64 + r(32), np.full(64, -1)]
    return np.concatenate(out)


def _ukv_cols():
    r = np.arange
    return np.concatenate([128 * h + r(64) for h in range(4)] + [128 * h + 64 + r(64) for h in range(4)])


def _rope_pieces(positions):
    pos = positions.astype(F32).reshape(-1, 1)
    narrow, col0, expand = [], 0, np.zeros((128, 512), np.float32)
    for t, d in enumerate((HEAD_DIM, MLA_ROPE)):
        half = d // 2
        inv = ROPE_THETA ** (-jnp.arange(0, d, 2, dtype=F32) / d)
        ang = pos * inv[None, :]
        narrow += [jnp.cos(ang), jnp.sin(ang)]
        lane = np.arange(128)
        k = (lane % d) % half
        expand[col0 + k, 256 * t + lane] = 1.0
        expand[col0 + half + k, 256 * t + 128 + lane] = np.where(lane % d < half, -1.0, 1.0)
        col0 += d
    x = jnp.pad(jnp.concatenate(narrow, axis=1), ((0, 0), (0, 128 - col0)))
    hi = x.astype(BF16)
    rest = x - hi.astype(F32)
    mid = rest.astype(BF16)
    lo = (rest - mid.astype(F32)).astype(BF16)
    return jnp.concatenate([hi, mid, lo], axis=1), jnp.asarray(np.concatenate([expand] * 3, axis=0), BF16)


def _bias_table_a(rel_bias):
    width, period = 9 * 128, 9 * 128 + TQ
    k = np.arange(period)
    d = np.where(k < width, A_PREV * CHUNK - k, A_PREV * CHUNK + period - k)
    idx = np.clip(d, -REL_CLIP, REL_CLIP) + REL_CLIP
    n_hi = A_PREV * CHUNK - REL_CLIP + 1
    n_lo = width - n_hi - (2 * REL_CLIP - 1)
    expect = np.concatenate([np.full(n_hi, 2 * REL_CLIP), np.arange(2 * REL_CLIP - 1, 0, -1),
                             np.zeros(n_lo, np.int64), np.full(period - width, 2 * REL_CLIP)])
    assert np.array_equal(idx, expect)
    rep = lambda col, n: jnp.broadcast_to(rel_bias[:, :, col:col + 1], rel_bias.shape[:2] + (n,))
    gp = jnp.concatenate([rep(2 * REL_CLIP, n_hi), jnp.flip(rel_bias[:, :, 1:2 * REL_CLIP], axis=-1),
                          rep(0, n_lo), rep(2 * REL_CLIP, period - width)], axis=-1) * LOG2E
    flat = jnp.tile(gp, (1, 1, TQ))[:, :, :TQ * (period - 1)]
    skew = flat.reshape(gp.shape[0], N_HEADS, TQ, period - 1)[..., :width]
    i = np.arange(TQ)[:, None]
    m = np.arange(width)[None, :]
    dchunk = i // CHUNK + A_PREV - m // CHUNK
    valid = (dchunk >= 0) & (dchunk <= A_PREV)
    t = jnp.where(jnp.asarray(valid)[None, None], skew, NEG_INF)
    return jnp.transpose(t, (0, 3, 1, 2)).reshape(gp.shape[0], 9, 128, N_HEADS * TQ)


def _mask_table_c():
    m = np.arange(3 * 128)[:, None]
    i = np.arange(TQ)[None, :]
    dchunk = i // CHUNK + SWA_PREV - m // CHUNK
    valid = (dchunk >= 0) & (dchunk <= SWA_PREV)
    t = np.where(valid, 0.0, NEG_INF).astype(np.float32)
    return np.tile(t, (1, N_HEADS)).reshape(3, 128, N_HEADS * TQ)


def _dot(a, b):
    return jnp.dot(a, b, preferred_element_type=F32)


def _dot_nt(a, b):
    return lax.dot_general(a, b, (((1,), (1,)), ((), ())), preferred_element_type=F32)


def _dot_tn(a, b):
    return lax.dot_general(a, b, (((0,), (0,)), ((), ())), preferred_element_type=F32)


def _rope(x, cos, sin_signed, half):
    lane = lax.broadcasted_iota(jnp.int32, x.shape, 1)
    first = (lane & (2 * half - 1)) < half
    swapped = jnp.where(first, pltpu.roll(x, 128 - half, 1), pltpu.roll(x, half, 1))
    return x * cos + swapped * sin_signed


def _project(xb, tab_ref, w_ref, wuq_ref, wukv_ref, gq_ref, gkv_ref, p_ref, rows=slice(None)):
    tab_ref, p_ref = tab_ref.at[rows], p_ref.at[rows]
    cos64, sin64 = tab_ref[:, 0:128], tab_ref[:, 128:256]
    cos32, sin32 = tab_ref[:, 256:384], tab_ref[:, 384:512]

    def mm(lo, hi):
        return _dot(xb, w_ref[:, lo:hi])

    rb = mm(W_B, W_B + 512)
    cq = rb[:, 0:256]
    ms = jnp.sum(cq * cq, axis=-1, keepdims=True) * (1.0 / MLA_Q_RANK)
    qn = (cq * lax.rsqrt(ms + 1e-6) * gq_ref[...]).astype(BF16)
    ckv = rb[:, 256:384]
    ms = jnp.mean(ckv * ckv, axis=-1, keepdims=True)
    kvn = (ckv * lax.rsqrt(ms + 1e-6) * gkv_ref[...]).astype(BF16)
    krb = _rope(rb[:, 384:512], cos32, sin32, 16).astype(BF16)

    r = mm(W_A, W_A + 768)
    p_ref[:, P_AQ:P_AQ + 256] = (r[:, 0:256] * QSCALE).astype(BF16)
    p_ref[:, P_AK:P_AK + 512] = r[:, 256:768].astype(BF16)

    q = _dot(qn, wuq_ref[...]) * B_QSCALE
    for p in range(2):
        base = P_BQ0 + 256 * p
        p_ref[:, base:base + 128] = q[:, 256 * p:256 * p + 128].astype(BF16)
        p_ref[:, base + 128:base + 256] = _rope(q[:, 256 * p + 128:256 * p + 256], cos32, sin32, 16).astype(BF16)
    kv = _dot(kvn, wukv_ref[...])
    for p in range(2):
        base = P_BK0 + 256 * p
        p_ref[:, base:base + 128] = kv[:, 128 * p:128 * (p + 1)].astype(BF16)
        p_ref[:, base + 128:base + 256] = krb
    p_ref[:, P_BV:P_BV + 256] = kv[:, 256:512].astype(BF16)

    r = mm(W_G, W_G + 1024)
    p_ref[:, P_GA:P_GA + 1024] = (r * (1.0 / (1.0 + jnp.exp(-r)))).astype(BF16)

    r = mm(W_C, W_C + 512)
    for j in range(2):
        qj = _rope(r[:, 128 * j:128 * (j + 1)], cos64, sin64, 32)
        p_ref[:, P_CQ + 128 * j:P_CQ + 128 * (j + 1)] = (qj * QSCALE).astype(BF16)
    p_ref[:, P_CK:P_CK + 128] = _rope(r[:, 256:384], cos64, sin64, 32).astype(BF16)
    p_ref[:, P_CV:P_CV + 128] = r[:, 384:512].astype(BF16)

    r = mm(W_M, W_M + 256)
    p_ref[:, P_MQ:P_MQ + 256] = (r * QSCALE).astype(BF16)


def _inproj_kernel(x_ref, pieces_ref, expand_ref, w_ref, wuq_ref, wukv_ref, gq_ref, gkv_ref, p_ref, tab_ref):
    tab_ref[...] = _dot(pieces_ref[...], expand_ref[...])
    _project(x_ref[...].astype(BF16), tab_ref, w_ref, wuq_ref, wukv_ref, gq_ref, gkv_ref, p_ref)


def _inproj(x2d, pieces, expand3, w, wuq, wukv, gq, gkv):
    n = x2d.shape[0]
    const = lambda shape: pl.BlockSpec(shape, lambda i: (0,) * len(shape))
    return pl.pallas_call(
        _inproj_kernel,
        grid=(n // TM,),
        in_specs=[
            pl.BlockSpec((TM, D_MODEL), lambda i: (i, 0)),
            pl.BlockSpec((TM, 384), lambda i: (i, 0)),
            const((384, 512)),
            const((D_MODEL, W_WIDTH)),
            const((256, 512)),
            const((128, 512)),
            const((1, 256)),
            const((1, 128)),
        ],
        out_specs=[pl.BlockSpec((TM, P_WIDTH), lambda i: (i, 0)),
                   pl.BlockSpec((TM, 512), lambda i: (i, 0))],
        out_shape=[jax.ShapeDtypeStruct((n, P_WIDTH), BF16),
                   jax.ShapeDtypeStruct((n, 512), F32)],
        compiler_params=pltpu.CompilerParams(
            dimension_semantics=("parallel",), vmem_limit_bytes=VMEM_LIMIT),
        name="inproj",
    )(x2d, pieces, expand3, w, wuq, wukv, gq, gkv)


def _memkv_kernel(mem_ref, w_ref, o_ref):
    o_ref[0] = _dot(mem_ref[0].astype(BF16), w_ref[...]).astype(BF16)


def _memkv(mem, w_all):
    b = mem.shape[0]
    n = w_all.shape[1]
    return pl.pallas_call(
        _memkv_kernel,
        grid=(b,),
        in_specs=[pl.BlockSpec((1, MEM_LEN, D_MODEL), lambda i: (i, 0, 0)),
                  pl.BlockSpec((D_MODEL, n), lambda i: (0, 0))],
        out_specs=pl.BlockSpec((1, MEM_LEN, n), lambda i: (i, 0, 0)),
        out_shape=jax.ShapeDtypeStruct((b, MEM_LEN, n), BF16),
        compiler_params=pltpu.CompilerParams(
            dimension_semantics=("parallel",), vmem_limit_bytes=VMEM_LIMIT),
        name="memkv",
    )(mem, w_all)


def _window_attn_kernel(*refs, win, prev, dk, has_table, has_sink):
    refs = list(refs)
    sink_ref = refs.pop(0) if has_sink else None
    q_ref, k_ref, v_ref, g_ref = refs[:4]
    e_ref = refs[4] if has_table else None
    o_ref, sa_sc, sb_sc = refs[-3:]
    n_items = q_ref.shape[1] // TQ
    lanes = N_HEADS * TQ

    lane128 = lax.broadcasted_iota(jnp.int32, (TQ, 128), 1)
    lo, hi = lane128 < HEAD_DIM, lane128 >= HEAD_DIM
    if has_sink:
        col = lax.broadcasted_iota(jnp.int32, (1, lanes), 1)
        order = C_HEAD_ORDER if dk == 128 else tuple(range(N_HEADS))
        sink = jnp.where(col < TQ, sink_ref[order[0]],
                         jnp.where(col < 2 * TQ, sink_ref[order[1]],
                                   jnp.where(col < 3 * TQ, sink_ref[order[2]], sink_ref[order[3]]))) * LOG2E

    def window_start(item):
        if prev is None:
            return 0
        return pl.multiple_of(jnp.maximum(item * TQ - prev, 0), 128)

    def scores_into(item, s_sc):
        q = q_ref[0, pl.ds(pl.multiple_of(item * TQ, TQ), TQ), :].astype(F32)
        if dk == 256:
            zero = jnp.zeros((TQ, 128), F32)
            blocks = [jnp.concatenate([jnp.where(lo, q[:, 0:128], 0.0), zero], axis=1),
                      jnp.concatenate([jnp.where(hi, q[:, 0:128], 0.0), zero], axis=1),
                      jnp.concatenate([zero, jnp.where(lo, q[:, 128:256], 0.0)], axis=1),
                      jnp.concatenate([zero, jnp.where(hi, q[:, 128:256], 0.0)], axis=1)]
        else:
            blocks = [jnp.where(lo, q[:, 0:128], 0.0), jnp.where(hi, q[:, 0:128], 0.0),
                      jnp.where(lo, q[:, 128:256], 0.0), jnp.where(hi, q[:, 128:256], 0.0)]
        qs = jnp.concatenate(blocks, axis=0).astype(BF16)
        s = _dot_nt(k_ref[0, pl.ds(window_start(item), win), :], qs)
        if has_table:
            mb0 = jnp.maximum(prev // 128 - item, 0)
            s = s + jnp.concatenate([e_ref[mb0 + jb] for jb in range(win // 128)], axis=0)
        s_sc[...] = s

    def consume(item, s_sc):
        start = window_start(item)
        m = jnp.max(s_sc[...], axis=0, keepdims=True)
        if has_sink:
            m = jnp.maximum(m, sink)
        p = jnp.exp2(s_sc[...] - m)
        l = jnp.sum(p, axis=0, keepdims=True)
        if has_sink:
            l = l + jnp.exp2(sink - m)
        inv = 1.0 / l
        pb = p.astype(BF16)
        v = v_ref[0, pl.ds(start, win), :]
        parts = []
        for pr in range(2):
            vp = v[:, 128 * pr:128 * (pr + 1)] if dk == 256 else v
            ot = _dot_tn(vp, pb[:, 256 * pr:256 * (pr + 1)])
            for e in range(2):
                h = 2 * pr + e
                parts.append(ot[64 * e:64 * (e + 1), 128 * e:128 * (e + 1)] * inv[:, TQ * h:TQ * (h + 1)])
        o = jnp.concatenate(parts, axis=0).T
        rows = pl.ds(pl.multiple_of(item * TQ, TQ), TQ)
        o_ref[0, rows, :] = (o * g_ref[0, rows, :].astype(F32)).astype(BF16)

    bufs = (sa_sc, sb_sc)
    scores_into(0, bufs[0])

    def body(i, carry):
        for j in range(WIN_UNROLL):
            item = WIN_UNROLL * i + j
            scores_into(jnp.minimum(item + 1, n_items - 1), bufs[(j + 1) % 2])
            consume(item, bufs[j % 2])
        return carry

    lax.fori_loop(0, n_items // WIN_UNROLL, body, 0)


def _window_attn(q_src, q_col, k_src, k_col, v_src, v_col, g_col, *, win, prev, dk, table=None, sinks=None):
    b, s, _ = q_src.shape
    skv = k_src.shape[1]
    assert s % (WIN_UNROLL * TQ) == 0 and skv >= win
    kern = functools.partial(_window_attn_kernel, win=win, prev=prev, dk=dk,
                             has_table=table is not None, has_sink=sinks is not None)
    in_specs, args = [], []
    if sinks is not None:
        in_specs.append(pl.BlockSpec(memory_space=pltpu.SMEM)); args.append(sinks)
    in_specs += [pl.BlockSpec((1, s, 256), lambda i: (i, 0, q_col // 256)),
                 pl.BlockSpec((1, skv, dk), lambda i: (i, 0, k_col // dk)),
                 pl.BlockSpec((1, skv, dk), lambda i: (i, 0, v_col // dk)),
                 pl.BlockSpec((1, s, 256), lambda i: (i, 0, g_col // 256))]
    args += [q_src, k_src, v_src, q_src]
    if table is not None:
        in_specs.append(pl.BlockSpec(table.shape, lambda i: (0, 0, 0))); args.append(table)
    return pl.pallas_call(
        kern,
        grid=(b,),
        in_specs=in_specs,
        out_specs=pl.BlockSpec((1, s, 256), lambda i: (i, 0, 0)),
        out_shape=jax.ShapeDtypeStruct((b, s, GROUP), BF16),
        scratch_shapes=[pltpu.VMEM((win, N_HEADS * TQ), F32), pltpu.VMEM((win, N_HEADS * TQ), F32)],
        compiler_params=pltpu.CompilerParams(
            dimension_semantics=("parallel",), vmem_limit_bytes=VMEM_LIMIT),
        name="attn_win%d" % win,
    )(*args)


def _attn_b_kernel(q0_ref, q1_ref, k0_ref, k1_ref, v_ref, g_ref, o_ref,
                   qs_sc, vt_sc, sa_sc, sb_sc, m_sc, l_sc, acc_sc):
    t = pl.program_id(1)
    n_tiles = B_TQ // TQ
    lane = lax.broadcasted_iota(jnp.int32, (TQ, 128), 1)
    q_refs = (q0_ref, q1_ref)
    k_refs = (k0_ref, k1_ref)

    def stack_queries(c):
        stacked = []
        for pr in range(2):
            q = q_refs[pr][0, TQ * c:TQ * (c + 1), :].astype(F32)
            nope, rope = q[:, 0:128], q[:, 128:256]
            head_a = jnp.concatenate([jnp.where(lane < 64, nope, 0.0), jnp.where(lane < 32, rope, 0.0)], axis=1)
            head_b = jnp.concatenate([jnp.where(lane >= 64, nope, 0.0), jnp.where(lane >= 32, rope, 0.0)], axis=1)
            qs = jnp.concatenate([head_a, head_b], axis=0).astype(BF16)
            qs_sc[2 * c + pr] = qs
            stacked.append(qs)
        return stacked

    lane2 = lax.broadcasted_iota(jnp.int32, (1, 2 * TQ), 1)
    hide_first_chunk = jnp.where((lane2 & (TQ - 1)) < CHUNK, NEG_INF, 0.0)

    def scores_into(kb, c, s_sc, nk=B_TK, qs=None):
        start = pl.multiple_of(kb * B_TK, B_TK)
        for pr in range(2):
            q = qs_sc[2 * c + pr] if qs is None else qs[pr]
            s_sc[pr, 0:nk, :] = _dot_nt(k_refs[pr][0, pl.ds(start, nk), :], q)

    def transpose_values(kb, slot=0):
        vt_sc[slot] = v_ref[0, pl.ds(pl.multiple_of(kb * B_TK, B_TK), B_TK), :].T

    def consume(kb, c, s_sc, nk=B_TK, diagonal=False, slot=0):
        for pr in range(2):
            u = 2 * c + pr
            s = s_sc[pr, 0:nk, :]
            if diagonal:
                s = jnp.concatenate([s[:nk - CHUNK], s[nk - CHUNK:] + hide_first_chunk], axis=0)
            m_prev = m_sc[u]
            m_new = jnp.maximum(m_prev, jnp.max(s, axis=0, keepdims=True))
            alpha = jnp.exp2(m_prev - m_new)
            p = jnp.exp2(s - m_new)
            l_sc[u] = alpha * l_sc[u] + jnp.sum(p, axis=0, keepdims=True)
            m_sc[u] = m_new
            acc_sc[u] = alpha * acc_sc[u] + _dot(vt_sc[slot, 128 * pr:128 * (pr + 1), 0:nk], p.astype(BF16))

    bufs = (sa_sc, sb_sc)
    scores_into(0, 0, bufs[0], qs=stack_queries(0))
    for c in range(1, n_tiles):
        stack_queries(c)
    m_sc[...] = jnp.full(m_sc.shape, NEG_INF, F32)
    l_sc[...] = jnp.zeros(l_sc.shape, F32)
    acc_sc[...] = jnp.zeros(acc_sc.shape, F32)

    def past_block(kb, slot):
        for c in range(n_tiles):
            if c + 1 < n_tiles:
                scores_into(kb, c + 1, bufs[(c + 1) % 2])
            else:
                scores_into(kb + 1, 0, bufs[0])
            consume(kb, c, bufs[c % 2], slot=slot)

    def two_blocks(i, carry):
        transpose_values(2 * i, 0)
        transpose_values(2 * i + 1, 1)
        past_block(2 * i, 0)
        past_block(2 * i + 1, 1)
        return carry

    def one_block(kb, carry):
        transpose_values(kb, 0)
        past_block(kb, 0)
        return carry

    lax.fori_loop(0, t >> 1, two_blocks, 0)
    lax.fori_loop(t & ~1, t, one_block, 0)

    transpose_values(t)
    for c in range(n_tiles):
        if c + 1 < n_tiles:
            scores_into(t, c + 1, bufs[(c + 1) % 2], nk=TQ * (c + 2))
        consume(t, c, bufs[c % 2], nk=TQ * (c + 1), diagonal=True)

    for c in range(n_tiles):
        parts = []
        for pr in range(2):
            u = 2 * c + pr
            inv = 1.0 / l_sc[u]
            for e in range(2):
                parts.append(acc_sc[u, 64 * e:64 * (e + 1), 128 * e:128 * (e + 1)] * inv[:, 128 * e:128 * (e + 1)])
        o = jnp.concatenate(parts, axis=0).T
        o_ref[0, TQ * c:TQ * (c + 1), :] = (o * g_ref[0, TQ * c:TQ * (c + 1), :].astype(F32)).astype(BF16)


def _attn_b(p3):
    b, s, _ = p3.shape
    n_units = 2 * (B_TQ // TQ)
    return pl.pallas_call(
        _attn_b_kernel,
        grid=(b, s // B_TQ),
        in_specs=[
            pl.BlockSpec((1, B_TQ, 256), lambda i, t: (i, t, P_BQ0 // 256)),
            pl.BlockSpec((1, B_TQ, 256), lambda i, t: (i, t, P_BQ1 // 256)),
            pl.BlockSpec((1, s, 256), lambda i, t: (i, 0, P_BK0 // 256)),
            pl.BlockSpec((1, s, 256), lambda i, t: (i, 0, P_BK1 // 256)),
            pl.BlockSpec((1, s, 256), lambda i, t: (i, 0, P_BV // 256)),
            pl.BlockSpec((1, B_TQ, 256), lambda i, t: (i, t, P_GB // 256)),
        ],
        out_specs=pl.BlockSpec((1, B_TQ, 256), lambda i, t: (i, t, 0)),
        out_shape=jax.ShapeDtypeStruct((b, s, GROUP), BF16),
        scratch_shapes=[pltpu.VMEM((n_units, 2 * TQ, 256), BF16),
                        pltpu.VMEM((2, GROUP, B_TK), BF16),
                        pltpu.VMEM((2, B_TK, 2 * TQ), F32),
                        pltpu.VMEM((2, B_TK, 2 * TQ), F32),
                        pltpu.VMEM((n_units, 1, 2 * TQ), F32),
                        pltpu.VMEM((n_units, 1, 2 * TQ), F32),
                        pltpu.VMEM((n_units, 128, 2 * TQ), F32)],
        compiler_params=pltpu.CompilerParams(
            dimension_semantics=("parallel", "arbitrary"), vmem_limit_bytes=VMEM_LIMIT),
        name="attn_b",
    )(p3, p3, p3, p3, p3, p3)


def _residual_norm(ya_ref, yb_ref, yc_ref, ym_ref, x_ref, w_ref, g_ref, b_ref, rows=slice(None)):
    y = (_dot(ya_ref[rows, :], w_ref[0:256, :]) + _dot(yb_ref[rows, :], w_ref[256:512, :])
         + _dot(yc_ref[rows, :], w_ref[512:768, :]) + _dot(ym_ref[rows, :], w_ref[768:1024, :]))
    z = ALPHA * x_ref[rows, :] + y
    mu = jnp.mean(z, axis=-1, keepdims=True)
    zc = z - mu
    var = jnp.mean(zc * zc, axis=-1, keepdims=True)
    return zc * lax.rsqrt(var + 1e-5) * g_ref[...] + b_ref[...]


def _outproj_kernel(ya_ref, yb_ref, yc_ref, ym_ref, x_ref, w_ref, g_ref, b_ref, o_ref):
    for h in range(2):
        rows = pl.ds(h * (TM // 2), TM // 2)
        o_ref[rows, :] = _residual_norm(ya_ref, yb_ref, yc_ref, ym_ref, x_ref, w_ref, g_ref, b_ref, rows)


def _out_in_proj_kernel(ya_ref, yb_ref, yc_ref, ym_ref, x_ref, wo_ref, g_ref, b_ref,
                        tab_ref, w_ref, wuq_ref, wukv_ref, gq_ref, gkv_ref, o_ref, p_ref):
    halves = [pl.ds(h * (TM // 2), TM // 2) for h in range(2)]
    xb = []
    for rows in halves:
        xn = _residual_norm(ya_ref, yb_ref, yc_ref, ym_ref, x_ref, wo_ref, g_ref, b_ref, rows)
        o_ref[rows, :] = xn
        xb.append(xn.astype(BF16))
    for rows, x in zip(halves, xb):
        _project(x, tab_ref, w_ref, wuq_ref, wukv_ref, gq_ref, gkv_ref, p_ref, rows)


def _outproj(ya, yb, yc, ym, x2d, w, g, bias):
    n = x2d.shape[0]
    ytile = pl.BlockSpec((TM, GROUP), lambda i: (i, 0))
    const = lambda shape: pl.BlockSpec(shape, lambda i: (0,) * len(shape))
    return pl.pallas_call(
        _outproj_kernel,
        grid=(n // TM,),
        in_specs=[ytile, ytile, ytile, ytile,
                  pl.BlockSpec((TM, D_MODEL), lambda i: (i, 0)),
                  const((D_MODEL, D_MODEL)), const((1, D_MODEL)), const((1, D_MODEL))],
        out_specs=pl.BlockSpec((TM, D_MODEL), lambda i: (i, 0)),
        out_shape=jax.ShapeDtypeStruct((n, D_MODEL), F32),
        compiler_params=pltpu.CompilerParams(
            dimension_semantics=("parallel",), vmem_limit_bytes=VMEM_LIMIT),
        name="outproj",
    )(ya, yb, yc, ym, x2d, w, g, bias)


def _out_in_proj(ya, yb, yc, ym, x2d, wo, g, bias, tab, w, wuq, wukv, gq, gkv):
    n = x2d.shape[0]
    ytile = pl.BlockSpec((TM, GROUP), lambda i: (i, 0))
    const = lambda shape: pl.BlockSpec(shape, lambda i: (0,) * len(shape))
    return pl.pallas_call(
        _out_in_proj_kernel,
        grid=(n // TM,),
        in_specs=[ytile, ytile, ytile, ytile,
                  pl.BlockSpec((TM, D_MODEL), lambda i: (i, 0)),
                  const((D_MODEL, D_MODEL)), const((1, D_MODEL)), const((1, D_MODEL)),
                  pl.BlockSpec((TM, 512), lambda i: (i, 0)),
                  const((D_MODEL, W_WIDTH)), const((256, 512)), const((128, 512)),
                  const((1, 256)), const((1, 128))],
        out_specs=[pl.BlockSpec((TM, D_MODEL), lambda i: (i, 0)),
                   pl.BlockSpec((TM, P_WIDTH), lambda i: (i, 0))],
        out_shape=[jax.ShapeDtypeStruct((n, D_MODEL), F32),
                   jax.ShapeDtypeStruct((n, P_WIDTH), BF16)],
        compiler_params=pltpu.CompilerParams(
            dimension_semantics=("parallel",), vmem_limit_bytes=VMEM_LIMIT),
        name="out_in_proj",
    )(ya, yb, yc, ym, x2d, wo, g, bias, tab, w, wuq, wukv, gq, gkv)


def kernel(x, mem, positions, w_in, rel_bias, mla_q_norm, w_uq, mla_kv_norm, w_ukv,
           swa_sinks, w_mem_kv, w_out, ln_gain, ln_bias):
    b, s, d = x.shape
    depth = w_in.shape[0]
    assert d == D_MODEL and depth == DEPTH and s % B_TK == 0 and s >= A_WIN and (b * s) % TM == 0

    cols, cperm = _inproj_cols()
    w_in_p = _take_cols(w_in, cols).astype(BF16)
    wuq_p = jnp.pad(_take_cols(w_uq, _uq_cols()), ((0, 0), (0, 256 - MLA_Q_RANK), (0, 0))).astype(BF16)
    wukv_p = _take_cols(w_ukv, _ukv_cols()).astype(BF16)
    gq = jnp.pad(mla_q_norm, ((0, 0), (0, 256 - MLA_Q_RANK)))[:, None, :]
    gkv = mla_kv_norm[:, None, :]
    rows = np.concatenate([np.arange(512), 512 + cperm, np.arange(768, 1024)])
    w_out_p = _take_cols(w_out, rows, axis=1).astype(BF16)
    w_mem_all = jnp.transpose(w_mem_kv, (1, 0, 2)).reshape(D_MODEL, depth * 512).astype(BF16)
    e_a = _bias_table_a(rel_bias)
    e_c = jnp.asarray(_mask_table_c())
    pieces, expand3 = _rope_pieces(positions)

    memkv = _memkv(mem, w_mem_all)
    h = x.reshape(b * s, d)
    p2, tab = _inproj(h, pieces, expand3, w_in_p[0], wuq_p[0], wukv_p[0], gq[0], gkv[0])
    for l in range(depth):
        p3 = p2.reshape(b, s, P_WIDTH)
        ya = _window_attn(p3, P_AQ, p3, P_AK, p3, P_AV, P_GA, win=A_WIN, prev=A_PREV * CHUNK, dk=256, table=e_a[l])
        yb = _attn_b(p3)
        yc = _window_attn(p3, P_CQ, p3, P_CK, p3, P_CV, P_GC, win=C_WIN, prev=SWA_PREV * CHUNK, dk=128,
                          table=e_c, sinks=swa_sinks[l])
        ym = _window_attn(p3, P_MQ, memkv, 512 * l, memkv, 512 * l + 256, P_GM, win=MEM_LEN, prev=None, dk=256)
        ys = [y.reshape(b * s, GROUP) for y in (ya, yb, yc, ym)]
        ln = (ln_gain[l][None, :], ln_bias[l][None, :])
        if l + 1 < depth:
            h, p2 = _out_in_proj(*ys, h, w_out_p[l], *ln, tab, w_in_p[l + 1], wuq_p[l + 1], wukv_p[l + 1],
                                 gq[l + 1], gkv[l + 1])
        else:
            h = _outproj(*ys, h, w_out_p[l], *ln)
    return h.reshape(b, s, d)
```

```python
import functools

import numpy as np
import jax
import jax.numpy as jnp
from jax import lax
from jax.experimental import pallas as pl
from jax.experimental.pallas import tpu as pltpu

F32 = jnp.float32
BF16 = jnp.bfloat16

D_MODEL = 1024
DEPTH = 4
CHUNK = 64
HEAD_DIM = 64
GROUP = 256
N_HEADS = 4
ROPE_THETA = 10000.0
NEG_INF = -1e30
A_PREV = 8
REL_CLIP = 128
MLA_NOPE = 64
MLA_ROPE = 32
MLA_Q_RANK = 192
MLA_KV_RANK = 128
SWA_PREV = 2
MEM_LEN = 256
ALPHA = (2.0 * DEPTH) ** 0.25

TQ = 128
A_WIN = TQ + A_PREV * CHUNK
C_WIN = TQ + SWA_PREV * CHUNK
WIN_UNROLL = 8
B_TK = 512
B_TQ = 512
TM = 512
VMEM_LIMIT = 56 * 1024 * 1024
LOG2E = 1.4426950408889634
QSCALE = HEAD_DIM ** -0.5 * LOG2E
B_QSCALE = (MLA_NOPE + MLA_ROPE) ** -0.5 * LOG2E

P_AQ, P_AK, P_AV = 0, 256, 512
P_BQ0, P_BQ1, P_BK0, P_BK1, P_BV = 768, 1024, 1280, 1536, 1792
P_CQ, P_CK, P_CV = 2048, 2304, 2432
P_MQ = 2560
P_GA, P_GB, P_GC, P_GM = 2816, 3072, 3328, 3584
P_WIDTH = 3840

W_A, W_C, W_M, W_G, W_B, W_WIDTH = 0, 768, 1280, 1536, 2560, 3072

C_HEAD_ORDER = (0, 2, 1, 3)


def _inproj_cols():
    r = np.arange
    aq, ak, av, ag = 0, 256, 512, 768
    bcq, bckv, bkr, bg = 1024, 1216, 1344, 1376
    cq, ck, cv, cg = 1632, 1888, 2016, 2144
    mq, mg = 2400, 2656
    cperm = np.concatenate([r(64) + 64 * h for h in C_HEAD_ORDER])
    pad = lambda n: np.full(n, -1)
    cols = np.concatenate([
        aq + r(256), ak + r(256), av + r(256),
        cq + cperm, ck + r(128), cv + r(128),
        mq + r(256),
        ag + r(256), bg + r(256), cg + cperm, mg + r(256),
        bcq + r(192), pad(64), bckv + r(128), bkr + r(32), bkr + r(32), pad(64),
    ])
    assert cols.shape[0] == W_WIDTH
    return cols, cperm


def _take_cols(w, cols, axis=-1):
    axis = axis % w.ndim
    pieces, i = [], 0
    while i < len(cols):
        j = i + 1
        if cols[i] < 0:
            while j < len(cols) and cols[j] < 0:
                j += 1
            shape = w.shape[:axis] + (j - i,) + w.shape[axis + 1:]
            pieces.append(jnp.zeros(shape, w.dtype))
        else:
            while j < len(cols) and cols[j] == cols[j - 1] + 1:
                j += 1
            pieces.append(lax.slice_in_dim(w, int(cols[i]), int(cols[i]) + (j - i), axis=axis))
        i = j
    return jnp.concatenate(pieces, axis=axis)


def _uq_cols():
    r = np.arange
    per = MLA_NOPE + MLA_ROPE
    out = []
    for p in range(2):
        h0, h1 = 2 * p, 2 * p + 1
        out += [per * h0 + r(64), per * h1 + r(64),
                per * h0 + 64 + r(32), per * h1 + 64 + r(32), np.full(64, -1)]
    return np.concatenate(out)


def _ukv_cols():
    r = np.arange
    return np.concatenate([128 * h + r(64) for h in range(4)] + [128 * h + 64 + r(64) for h in range(4)])


def _rope_pieces(positions):
    pos = positions.astype(F32).reshape(-1, 1)
    narrow, col0, expand = [], 0, np.zeros((128, 512), np.float32)
    for t, d in enumerate((HEAD_DIM, MLA_ROPE)):
        half = d // 2
        inv = ROPE_THETA ** (-jnp.arange(0, d, 2, dtype=F32) / d)
        ang = pos * inv[None, :]
        narrow += [jnp.cos(ang), jnp.sin(ang)]
        lane = np.arange(128)
        k = (lane % d) % half
        expand[col0 + k, 256 * t + lane] = 1.0
        expand[col0 + half + k, 256 * t + 128 + lane] = np.where(lane % d < half, -1.0, 1.0)
        col0 += d
    x = jnp.pad(jnp.concatenate(narrow, axis=1), ((0, 0), (0, 128 - col0)))
    x = lax.optimization_barrier(x)
    hi = x.astype(BF16)
    rest = x - hi.astype(F32)
    mid = rest.astype(BF16)
    lo = (rest - mid.astype(F32)).astype(BF16)
    return jnp.concatenate([hi, mid, lo], axis=1), jnp.asarray(np.concatenate([expand] * 3, axis=0), BF16)


def _bias_table_a(rel_bias):
    width, period = 9 * 128, 9 * 128 + TQ
    k = np.arange(period)
    d = np.where(k < width, A_PREV * CHUNK - k, A_PREV * CHUNK + period - k)
    idx = np.clip(d, -REL_CLIP, REL_CLIP) + REL_CLIP
    n_hi = A_PREV * CHUNK - REL_CLIP + 1
    n_lo = width - n_hi - (2 * REL_CLIP - 1)
    expect = np.concatenate([np.full(n_hi, 2 * REL_CLIP), np.arange(2 * REL_CLIP - 1, 0, -1),
                             np.zeros(n_lo, np.int64), np.full(period - width, 2 * REL_CLIP)])
    assert np.array_equal(idx, expect)
    rep = lambda col, n: jnp.broadcast_to(rel_bias[:, :, col:col + 1], rel_bias.shape[:2] + (n,))
    gp = jnp.concatenate([rep(2 * REL_CLIP, n_hi), jnp.flip(rel_bias[:, :, 1:2 * REL_CLIP], axis=-1),
                          rep(0, n_lo), rep(2 * REL_CLIP, period - width)], axis=-1) * LOG2E
    flat = jnp.tile(gp, (1, 1, TQ))[:, :, :TQ * (period - 1)]
    skew = flat.reshape(gp.shape[0], N_HEADS, TQ, period - 1)[..., :width]
    i = np.arange(TQ)[:, None]
    m = np.arange(width)[None, :]
    dchunk = i // CHUNK + A_PREV - m // CHUNK
    valid = (dchunk >= 0) & (dchunk <= A_PREV)
    t = jnp.where(jnp.asarray(valid)[None, None], skew, NEG_INF)
    return jnp.transpose(t, (0, 3, 1, 2)).reshape(gp.shape[0], 9, 128, N_HEADS * TQ)


def _mask_table_c():
    m = np.arange(3 * 128)[:, None]
    i = np.arange(TQ)[None, :]
    dchunk = i // CHUNK + SWA_PREV - m // CHUNK
    valid = (dchunk >= 0) & (dchunk <= SWA_PREV)
    t = np.where(valid, 0.0, NEG_INF).astype(np.float32)
    return np.tile(t, (1, N_HEADS)).reshape(3, 128, N_HEADS * TQ)


def _dot(a, b):
    return jnp.dot(a, b, preferred_element_type=F32)


def _dot_nt(a, b):
    return lax.dot_general(a, b, (((1,), (1,)), ((), ())), preferred_element_type=F32)


def _dot_tn(a, b):
    return lax.dot_general(a, b, (((0,), (0,)), ((), ())), preferred_element_type=F32)


def _rope(x, cos, sin_signed, half):
    lane = lax.broadcasted_iota(jnp.int32, x.shape, 1)
    first = (lane & (2 * half - 1)) < half
    swapped = jnp.where(first, pltpu.roll(x, 128 - half, 1), pltpu.roll(x, half, 1))
    return x * cos + swapped * sin_signed


def _project(xb, tab_ref, w_ref, wuq_ref, wukv_ref, gq_ref, gkv_ref, p_ref, rows=slice(None)):
    tab_ref, p_ref = tab_ref.at[rows], p_ref.at[rows]
    cos64, sin64 = tab_ref[:, 0:128], tab_ref[:, 128:256]
    cos32, sin32 = tab_ref[:, 256:384], tab_ref[:, 384:512]

    def mm(lo, hi):
        return _dot(xb, w_ref[:, lo:hi])

    rb = mm(W_B, W_B + 512)
    cq = rb[:, 0:256]
    ms = jnp.sum(cq * cq, axis=-1, keepdims=True) * (1.0 / MLA_Q_RANK)
    qn = (cq * lax.rsqrt(ms + 1e-6) * gq_ref[...]).astype(BF16)
    ckv = rb[:, 256:384]
    ms = jnp.mean(ckv * ckv, axis=-1, keepdims=True)
    kvn = (ckv * lax.rsqrt(ms + 1e-6) * gkv_ref[...]).astype(BF16)
    krb = _rope(rb[:, 384:512], cos32, sin32, 16).astype(BF16)

    r = mm(W_A, W_A + 768)
    p_ref[:, P_AQ:P_AQ + 256] = (r[:, 0:256] * QSCALE).astype(BF16)
    p_ref[:, P_AK:P_AK + 512] = r[:, 256:768].astype(BF16)

    q = _dot(qn, wuq_ref[...]) * B_QSCALE
    for p in range(2):
        base = P_BQ0 + 256 * p
        p_ref[:, base:base + 128] = q[:, 256 * p:256 * p + 128].astype(BF16)
        p_ref[:, base + 128:base + 256] = _rope(q[:, 256 * p + 128:256 * p + 256], cos32, sin32, 16).astype(BF16)
    kv = _dot(kvn, wukv_ref[...])
    for p in range(2):
        base = P_BK0 + 256 * p
        p_ref[:, base:base + 128] = kv[:, 128 * p:128 * (p + 1)].astype(BF16)
        p_ref[:, base + 128:base + 256] = krb
    p_ref[:, P_BV:P_BV + 256] = kv[:, 256:512].astype(BF16)

    r = mm(W_G, W_G + 1024)
    p_ref[:, P_GA:P_GA + 1024] = (r * (1.0 / (1.0 + jnp.exp(-r)))).astype(BF16)

    r = mm(W_C, W_C + 512)
    for j in range(2):
        qj = _rope(r[:, 128 * j:128 * (j + 1)], cos64, sin64, 32)
        p_ref[:, P_CQ + 128 * j:P_CQ + 128 * (j + 1)] = (qj * QSCALE).astype(BF16)
    p_ref[:, P_CK:P_CK + 128] = _rope(r[:, 256:384], cos64, sin64, 32).astype(BF16)
    p_ref[:, P_CV:P_CV + 128] = r[:, 384:512].astype(BF16)

    r = mm(W_M, W_M + 256)
    p_ref[:, P_MQ:P_MQ + 256] = (r * QSCALE).astype(BF16)


def _inproj_kernel(x_ref, pieces_ref, expand_ref, w_ref, wuq_ref, wukv_ref, gq_ref, gkv_ref, p_ref, tab_ref):
    tab_ref[...] = _dot(pieces_ref[...], expand_ref[...])
    _project(x_ref[...].astype(BF16), tab_ref, w_ref, wuq_ref, wukv_ref, gq_ref, gkv_ref, p_ref)


def _inproj(x2d, pieces, expand3, w, wuq, wukv, gq, gkv):
    n = x2d.shape[0]
    const = lambda shape: pl.BlockSpec(shape, lambda i: (0,) * len(shape))
    return pl.pallas_call(
        _inproj_kernel,
        grid=(n // TM,),
        in_specs=[
            pl.BlockSpec((TM, D_MODEL), lambda i: (i, 0)),
            pl.BlockSpec((TM, 384), lambda i: (i, 0)),
            const((384, 512)),
            const((D_MODEL, W_WIDTH)),
            const((256, 512)),
            const((128, 512)),
            const((1, 256)),
            const((1, 128)),
        ],
        out_specs=[pl.BlockSpec((TM, P_WIDTH), lambda i: (i, 0)),
                   pl.BlockSpec((TM, 512), lambda i: (i, 0))],
        out_shape=[jax.ShapeDtypeStruct((n, P_WIDTH), BF16),
                   jax.ShapeDtypeStruct((n, 512), F32)],
        compiler_params=pltpu.CompilerParams(
            dimension_semantics=("parallel",), vmem_limit_bytes=VMEM_LIMIT),
        name="inproj",
    )(x2d, pieces, expand3, w, wuq, wukv, gq, gkv)


def _memkv_kernel(mem_ref, w_ref, o_ref):
    o_ref[0] = _dot(mem_ref[0].astype(BF16), w_ref[...]).astype(BF16)


def _memkv(mem, w_all):
    b = mem.shape[0]
    n = w_all.shape[1]
    return pl.pallas_call(
        _memkv_kernel,
        grid=(b,),
        in_specs=[pl.BlockSpec((1, MEM_LEN, D_MODEL), lambda i: (i, 0, 0)),
                  pl.BlockSpec((D_MODEL, n), lambda i: (0, 0))],
        out_specs=pl.BlockSpec((1, MEM_LEN, n), lambda i: (i, 0, 0)),
        out_shape=jax.ShapeDtypeStruct((b, MEM_LEN, n), BF16),
        compiler_params=pltpu.CompilerParams(
            dimension_semantics=("parallel",), vmem_limit_bytes=VMEM_LIMIT),
        name="memkv",
    )(mem, w_all)


def _window_attn_kernel(*refs, win, prev, dk, has_table, has_sink):
    refs = list(refs)
    sink_ref = refs.pop(0) if has_sink else None
    q_ref, k_ref, v_ref, g_ref = refs[:4]
    e_ref = refs[4] if has_table else None
    o_ref, sa_sc, sb_sc = refs[-3:]
    n_items = q_ref.shape[1] // TQ
    lanes = N_HEADS * TQ

    lane128 = lax.broadcasted_iota(jnp.int32, (TQ, 128), 1)
    lo, hi = lane128 < HEAD_DIM, lane128 >= HEAD_DIM
    if has_sink:
        col = lax.broadcasted_iota(jnp.int32, (1, lanes), 1)
        order = C_HEAD_ORDER if dk == 128 else tuple(range(N_HEADS))
        sink = jnp.where(col < TQ, sink_ref[order[0]],
                         jnp.where(col < 2 * TQ, sink_ref[order[1]],
                                   jnp.where(col < 3 * TQ, sink_ref[order[2]], sink_ref[order[3]]))) * LOG2E

    def window_start(item):
        if prev is None:
            return 0
        return pl.multiple_of(jnp.maximum(item * TQ - prev, 0), 128)

    def scores_into(item, s_sc):
        q = q_ref[0, pl.ds(pl.multiple_of(item * TQ, TQ), TQ), :].astype(F32)
        if dk == 256:
            zero = jnp.zeros((TQ, 128), F32)
            blocks = [jnp.concatenate([jnp.where(lo, q[:, 0:128], 0.0), zero], axis=1),
                      jnp.concatenate([jnp.where(hi, q[:, 0:128], 0.0), zero], axis=1),
                      jnp.concatenate([zero, jnp.where(lo, q[:, 128:256], 0.0)], axis=1),
                      jnp.concatenate([zero, jnp.where(hi, q[:, 128:256], 0.0)], axis=1)]
        else:
            blocks = [jnp.where(lo, q[:, 0:128], 0.0), jnp.where(hi, q[:, 0:128], 0.0),
                      jnp.where(lo, q[:, 128:256], 0.0), jnp.where(hi, q[:, 128:256], 0.0)]
        qs = jnp.concatenate(blocks, axis=0).astype(BF16)
        s = _dot_nt(k_ref[0, pl.ds(window_start(item), win), :], qs)
        if has_table:
            mb0 = jnp.maximum(prev // 128 - item, 0)
            s = s + jnp.concatenate([e_ref[mb0 + jb] for jb in range(win // 128)], axis=0)
        s_sc[...] = s

    def consume(item, s_sc):
        start = window_start(item)
        m = jnp.max(s_sc[...], axis=0, keepdims=True)
        if has_sink:
            m = jnp.maximum(m, sink)
        p = jnp.exp2(s_sc[...] - m)
        l = jnp.sum(p, axis=0, keepdims=True)
        if has_sink:
            l = l + jnp.exp2(sink - m)
        inv = 1.0 / l
        pb = p.astype(BF16)
        v = v_ref[0, pl.ds(start, win), :]
        parts = []
        for pr in range(2):
            vp = v[:, 128 * pr:128 * (pr + 1)] if dk == 256 else v
            ot = _dot_tn(vp, pb[:, 256 * pr:256 * (pr + 1)])
            for e in range(2):
                h = 2 * pr + e
                parts.append(ot[64 * e:64 * (e + 1), 128 * e:128 * (e + 1)] * inv[:, TQ * h:TQ * (h + 1)])
        o = jnp.concatenate(parts, axis=0).T
        rows = pl.ds(pl.multiple_of(item * TQ, TQ), TQ)
        o_ref[0, rows, :] = (o * g_ref[0, rows, :].astype(F32)).astype(BF16)

    bufs = (sa_sc, sb_sc)
    scores_into(0, bufs[0])

    def body(i, carry):
        for j in range(WIN_UNROLL):
            item = WIN_UNROLL * i + j
            scores_into(jnp.minimum(item + 1, n_items - 1), bufs[(j + 1) % 2])
            consume(item, bufs[j % 2])
        return carry

    lax.fori_loop(0, n_items // WIN_UNROLL, body, 0)


def _window_attn(q_src, q_col, k_src, k_col, v_src, v_col, g_col, *, win, prev, dk, table=None, sinks=None):
    b, s, _ = q_src.shape
    skv = k_src.shape[1]
    assert s % (WIN_UNROLL * TQ) == 0 and skv >= win
    kern = functools.partial(_window_attn_kernel, win=win, prev=prev, dk=dk,
                             has_table=table is not None, has_sink=sinks is not None)
    in_specs, args = [], []
    if sinks is not None:
        in_specs.append(pl.BlockSpec(memory_space=pltpu.SMEM)); args.append(sinks)
    in_specs += [pl.BlockSpec((1, s, 256), lambda i: (i, 0, q_col // 256)),
                 pl.BlockSpec((1, skv, dk), lambda i: (i, 0, k_col // dk)),
                 pl.BlockSpec((1, skv, dk), lambda i: (i, 0, v_col // dk)),
                 pl.BlockSpec((1, s, 256), lambda i: (i, 0, g_col // 256))]
    args += [q_src, k_src, v_src, q_src]
    if table is not None:
        in_specs.append(pl.BlockSpec(table.shape, lambda i: (0, 0, 0))); args.append(table)
    return pl.pallas_call(
        kern,
        grid=(b,),
        in_specs=in_specs,
        out_specs=pl.BlockSpec((1, s, 256), lambda i: (i, 0, 0)),
        out_shape=jax.ShapeDtypeStruct((b, s, GROUP), BF16),
        scratch_shapes=[pltpu.VMEM((win, N_HEADS * TQ), F32), pltpu.VMEM((win, N_HEADS * TQ), F32)],
        compiler_params=pltpu.CompilerParams(
            dimension_semantics=("parallel",), vmem_limit_bytes=VMEM_LIMIT),
        name="attn_win%d" % win,
    )(*args)


def _attn_b_kernel(q0_ref, q1_ref, k0_ref, k1_ref, v_ref, g_ref, o_ref,
                   qs_sc, vt_sc, sa_sc, sb_sc, m_sc, l_sc, acc_sc):
    t = pl.program_id(1)
    n_tiles = B_TQ // TQ
    lane = lax.broadcasted_iota(jnp.int32, (TQ, 128), 1)
    q_refs = (q0_ref, q1_ref)
    k_refs = (k0_ref, k1_ref)

    def stack_queries(c):
        stacked = []
        for pr in range(2):
            q = q_refs[pr][0, TQ * c:TQ * (c + 1), :].astype(F32)
            nope, rope = q[:, 0:128], q[:, 128:256]
            head_a = jnp.concatenate([jnp.where(lane < 64, nope, 0.0), jnp.where(lane < 32, rope, 0.0)], axis=1)
            head_b = jnp.concatenate([jnp.where(lane >= 64, nope, 0.0), jnp.where(lane >= 32, rope, 0.0)], axis=1)
            qs = jnp.concatenate([head_a, head_b], axis=0).astype(BF16)
            qs_sc[2 * c + pr] = qs
            stacked.append(qs)
        return stacked

    lane2 = lax.broadcasted_iota(jnp.int32, (1, 2 * TQ), 1)
    hide_first_chunk = jnp.where((lane2 & (TQ - 1)) < CHUNK, NEG_INF, 0.0)

    def scores_into(kb, c, s_sc, nk=B_TK, qs=None):
        start = pl.multiple_of(kb * B_TK, B_TK)
        for pr in range(2):
            q = qs_sc[2 * c + pr] if qs is None else qs[pr]
            s_sc[pr, 0:nk, :] = _dot_nt(k_refs[pr][0, pl.ds(start, nk), :], q)

    def transpose_values(kb, slot=0):
        vt_sc[slot] = v_ref[0, pl.ds(pl.multiple_of(kb * B_TK, B_TK), B_TK), :].T

    def consume(kb, c, s_sc, nk=B_TK, diagonal=False, slot=0):
        for pr in range(2):
            u = 2 * c + pr
            s = s_sc[pr, 0:nk, :]
            if diagonal:
                s = jnp.concatenate([s[:nk - CHUNK], s[nk - CHUNK:] + hide_first_chunk], axis=0)
            m_prev = m_sc[u]
            m_new = jnp.maximum(m_prev, jnp.max(s, axis=0, keepdims=True))
            alpha = jnp.exp2(m_prev - m_new)
            p = jnp.exp2(s - m_new)
            l_sc[u] = alpha * l_sc[u] + jnp.sum(p, axis=0, keepdims=True)
            m_sc[u] = m_new
            acc_sc[u] = alpha * acc_sc[u] + _dot(vt_sc[slot, 128 * pr:128 * (pr + 1), 0:nk], p.astype(BF16))

    bufs = (sa_sc, sb_sc)
    scores_into(0, 0, bufs[0], qs=stack_queries(0))
    for c in range(1, n_tiles):
        stack_queries(c)
    m_sc[...] = jnp.full(m_sc.shape, NEG_INF, F32)
    l_sc[...] = jnp.zeros(l_sc.shape, F32)
    acc_sc[...] = jnp.zeros(acc_sc.shape, F32)

    def past_block(kb, slot):
        for c in range(n_tiles):
            if c + 1 < n_tiles:
                scores_into(kb, c + 1, bufs[(c + 1) % 2])
            else:
                scores_into(kb + 1, 0, bufs[0])
            consume(kb, c, bufs[c % 2], slot=slot)

    def two_blocks(i, carry):
        transpose_values(2 * i, 0)
        transpose_values(2 * i + 1, 1)
        past_block(2 * i, 0)
        past_block(2 * i + 1, 1)
        return carry

    def one_block(kb, carry):
        transpose_values(kb, 0)
        past_block(kb, 0)
        return carry

    lax.fori_loop(0, t >> 1, two_blocks, 0)
    lax.fori_loop(t & ~1, t, one_block, 0)

    transpose_values(t)
    for c in range(n_tiles):
        if c + 1 < n_tiles:
            scores_into(t, c + 1, bufs[(c + 1) % 2], nk=TQ * (c + 2))
        consume(t, c, bufs[c % 2], nk=TQ * (c + 1), diagonal=True)

    for c in range(n_tiles):
        parts = []
        for pr in range(2):
            u = 2 * c + pr
            inv = 1.0 / l_sc[u]
            for e in range(2):
                parts.append(acc_sc[u, 64 * e:64 * (e + 1), 128 * e:128 * (e + 1)] * inv[:, 128 * e:128 * (e + 1)])
        o = jnp.concatenate(parts, axis=0).T
        o_ref[0, TQ * c:TQ * (c + 1), :] = (o * g_ref[0, TQ * c:TQ * (c + 1), :].astype(F32)).astype(BF16)


def _attn_b(p3):
    b, s, _ = p3.shape
    n_units = 2 * (B_TQ // TQ)
    return pl.pallas_call(
        _attn_b_kernel,
        grid=(b, s // B_TQ),
        in_specs=[
            pl.BlockSpec((1, B_TQ, 256), lambda i, t: (i, t, P_BQ0 // 256)),
            pl.BlockSpec((1, B_TQ, 256), lambda i, t: (i, t, P_BQ1 // 256)),
            pl.BlockSpec((1, s, 256), lambda i, t: (i, 0, P_BK0 // 256)),
            pl.BlockSpec((1, s, 256), lambda i, t: (i, 0, P_BK1 // 256)),
            pl.BlockSpec((1, s, 256), lambda i, t: (i, 0, P_BV // 256)),
            pl.BlockSpec((1, B_TQ, 256), lambda i, t: (i, t, P_GB // 256)),
        ],
        out_specs=pl.BlockSpec((1, B_TQ, 256), lambda i, t: (i, t, 0)),
        out_shape=jax.ShapeDtypeStruct((b, s, GROUP), BF16),
        scratch_shapes=[pltpu.VMEM((n_units, 2 * TQ, 256), BF16),
                        pltpu.VMEM((2, GROUP, B_TK), BF16),
                        pltpu.VMEM((2, B_TK, 2 * TQ), F32),
                        pltpu.VMEM((2, B_TK, 2 * TQ), F32),
                        pltpu.VMEM((n_units, 1, 2 * TQ), F32),
                        pltpu.VMEM((n_units, 1, 2 * TQ), F32),
                        pltpu.VMEM((n_units, 128, 2 * TQ), F32)],
        compiler_params=pltpu.CompilerParams(
            dimension_semantics=("parallel", "arbitrary"), vmem_limit_bytes=VMEM_LIMIT),
        name="attn_b",
    )(p3, p3, p3, p3, p3, p3)


def _residual_norm(ya_ref, yb_ref, yc_ref, ym_ref, x_ref, w_ref, g_ref, b_ref, rows=slice(None)):
    y = (_dot(ya_ref[rows, :], w_ref[0:256, :]) + _dot(yb_ref[rows, :], w_ref[256:512, :])
         + _dot(yc_ref[rows, :], w_ref[512:768, :]) + _dot(ym_ref[rows, :], w_ref[768:1024, :]))
    z = ALPHA * x_ref[rows, :] + y
    mu = jnp.mean(z, axis=-1, keepdims=True)
    zc = z - mu
    var = jnp.mean(zc * zc, axis=-1, keepdims=True)
    return zc * lax.rsqrt(var + 1e-5) * g_ref[...] + b_ref[...]


def _outproj_kernel(ya_ref, yb_ref, yc_ref, ym_ref, x_ref, w_ref, g_ref, b_ref, o_ref):
    for h in range(2):
        rows = pl.ds(h * (TM // 2), TM // 2)
        o_ref[rows, :] = _residual_norm(ya_ref, yb_ref, yc_ref, ym_ref, x_ref, w_ref, g_ref, b_ref, rows)


def _out_in_proj_kernel(ya_ref, yb_ref, yc_ref, ym_ref, x_ref, wo_ref, g_ref, b_ref,
                        tab_ref, w_ref, wuq_ref, wukv_ref, gq_ref, gkv_ref, o_ref, p_ref):
    halves = [pl.ds(h * (TM // 2), TM // 2) for h in range(2)]
    xb = []
    for rows in halves:
        xn = _residual_norm(ya_ref, yb_ref, yc_ref, ym_ref, x_ref, wo_ref, g_ref, b_ref, rows)
        o_ref[rows, :] = xn
        xb.append(xn.astype(BF16))
    for rows, x in zip(halves, xb):
        _project(x, tab_ref, w_ref, wuq_ref, wukv_ref, gq_ref, gkv_ref, p_ref, rows)


def _outproj(ya, yb, yc, ym, x2d, w, g, bias):
    n = x2d.shape[0]
    ytile = pl.BlockSpec((TM, GROUP), lambda i: (i, 0))
    const = lambda shape: pl.BlockSpec(shape, lambda i: (0,) * len(shape))
    return pl.pallas_call(
        _outproj_kernel,
        grid=(n // TM,),
        in_specs=[ytile, ytile, ytile, ytile,
                  pl.BlockSpec((TM, D_MODEL), lambda i: (i, 0)),
                  const((D_MODEL, D_MODEL)), const((1, D_MODEL)), const((1, D_MODEL))],
        out_specs=pl.BlockSpec((TM, D_MODEL), lambda i: (i, 0)),
        out_shape=jax.ShapeDtypeStruct((n, D_MODEL), F32),
        compiler_params=pltpu.CompilerParams(
            dimension_semantics=("parallel",), vmem_limit_bytes=VMEM_LIMIT),
        name="outproj",
    )(ya, yb, yc, ym, x2d, w, g, bias)


def _out_in_proj(ya, yb, yc, ym, x2d, wo, g, bias, tab, w, wuq, wukv, gq, gkv):
    n = x2d.shape[0]
    ytile = pl.BlockSpec((TM, GROUP), lambda i: (i, 0))
    const = lambda shape: pl.BlockSpec(shape, lambda i: (0,) * len(shape))
    return pl.pallas_call(
        _out_in_proj_kernel,
        grid=(n // TM,),
        in_specs=[ytile, ytile, ytile, ytile,
                  pl.BlockSpec((TM, D_MODEL), lambda i: (i, 0)),
                  const((D_MODEL, D_MODEL)), const((1, D_MODEL)), const((1, D_MODEL)),
                  pl.BlockSpec((TM, 512), lambda i: (i, 0)),
                  const((D_MODEL, W_WIDTH)), const((256, 512)), const((128, 512)),
                  const((1, 256)), const((1, 128))],
        out_specs=[pl.BlockSpec((TM, D_MODEL), lambda i: (i, 0)),
                   pl.BlockSpec((TM, P_WIDTH), lambda i: (i, 0))],
        out_shape=[jax.ShapeDtypeStruct((n, D_MODEL), F32),
                   jax.ShapeDtypeStruct((n, P_WIDTH), BF16)],
        compiler_params=pltpu.CompilerParams(
            dimension_semantics=("parallel",), vmem_limit_bytes=VMEM_LIMIT),
        name="out_in_proj",
    )(ya, yb, yc, ym, x2d, wo, g, bias, tab, w, wuq, wukv, gq, gkv)


def kernel(x, mem, positions, w_in, rel_bias, mla_q_norm, w_uq, mla_kv_norm, w_ukv,
           swa_sinks, w_mem_kv, w_out, ln_gain, ln_bias):
    b, s, d = x.shape
    depth = w_in.shape[0]
    assert d == D_MODEL and depth == DEPTH and s % B_TK == 0 and s >= A_WIN and (b * s) % TM == 0

    cols, cperm = _inproj_cols()
    w_in_p = _take_cols(w_in, cols).astype(BF16)
    wuq_p = jnp.pad(_take_cols(w_uq, _uq_cols()), ((0, 0), (0, 256 - MLA_Q_RANK), (0, 0))).astype(BF16)
    wukv_p = _take_cols(w_ukv, _ukv_cols()).astype(BF16)
    gq = jnp.pad(mla_q_norm, ((0, 0), (0, 256 - MLA_Q_RANK)))[:, None, :]
    gkv = mla_kv_norm[:, None, :]
    rows = np.concatenate([np.arange(512), 512 + cperm, np.arange(768, 1024)])
    w_out_p = _take_cols(w_out, rows, axis=1).astype(BF16)
    w_mem_all = jnp.transpose(w_mem_kv, (1, 0, 2)).reshape(D_MODEL, depth * 512).astype(BF16)
    e_a = _bias_table_a(rel_bias)
    e_c = jnp.asarray(_mask_table_c())
    pieces, expand3 = _rope_pieces(positions)

    memkv = _memkv(mem, w_mem_all)
    h = x.reshape(b * s, d)
    p2, tab = _inproj(h, pieces, expand3, w_in_p[0], wuq_p[0], wukv_p[0], gq[0], gkv[0])
    for l in range(depth):
        p3 = p2.reshape(b, s, P_WIDTH)
        ya = _window_attn(p3, P_AQ, p3, P_AK, p3, P_AV, P_GA, win=A_WIN, prev=A_PREV * CHUNK, dk=256, table=e_a[l])
        yb = _attn_b(p3)
        yc = _window_attn(p3, P_CQ, p3, P_CK, p3, P_CV, P_GC, win=C_WIN, prev=SWA_PREV * CHUNK, dk=128,
                          table=e_c, sinks=swa_sinks[l])
        ym = _window_attn(p3, P_MQ, memkv, 512 * l, memkv, 512 * l + 256, P_GM, win=MEM_LEN, prev=None, dk=256)
        ys = [y.reshape(b * s, GROUP) for y in (ya, yb, yc, ym)]
        ln = (ln_gain[l][None, :], ln_bias[l][None, :])
        if l + 1 < depth:
            h, p2 = _out_in_proj(*ys, h, w_out_p[l], *ln, tab, w_in_p[l + 1], wuq_p[l + 1], wukv_p[l + 1],
                                 gq[l + 1], gkv[l + 1])
        else:
            h = _outproj(*ys, h, w_out_p[l], *ln)
    return h.reshape(b, s, d)
```

```python
import functools

import numpy as np
import jax
import jax.numpy as jnp
from jax import lax
from jax.experimental import pallas as pl
from jax.experimental.pallas import tpu as pltpu

F32 = jnp.float32
BF16 = jnp.bfloat16

D_MODEL = 1024
DEPTH = 4
CHUNK = 64
HEAD_DIM = 64
GROUP = 256
N_HEADS = 4
ROPE_THETA = 10000.0
NEG_INF = -1e30
A_PREV = 8
REL_CLIP = 128
MLA_NOPE = 64
MLA_ROPE = 32
MLA_Q_RANK = 192
MLA_KV_RANK = 128
SWA_PREV = 2
MEM_LEN = 256
ALPHA = (2.0 * DEPTH) ** 0.25

TQ = 128
A_WIN = TQ + A_PREV * CHUNK
C_WIN = TQ + SWA_PREV * CHUNK
WIN_UNROLL = 8
B_TK = 512
B_TQ = 512
TM = 512
VMEM_LIMIT = 56 * 1024 * 1024
LOG2E = 1.4426950408889634
QSCALE = HEAD_DIM ** -0.5 * LOG2E
B_QSCALE = (MLA_NOPE + MLA_ROPE) ** -0.5 * LOG2E

P_AQ, P_AK, P_AV = 0, 256, 512
P_BQ0, P_BQ1, P_BK0, P_BK1, P_BV = 768, 1024, 1280, 1536, 1792
P_CQ, P_CK, P_CV = 2048, 2304, 2432
P_MQ = 2560
P_GA, P_GB, P_GC, P_GM = 2816, 3072, 3328, 3584
P_WIDTH = 3840

W_A, W_C, W_M, W_G, W_B, W_WIDTH = 0, 768, 1280, 1536, 2560, 3072

C_HEAD_ORDER = (0, 2, 1, 3)


def _inproj_cols():
    r = np.arange
    aq, ak, av, ag = 0, 256, 512, 768
    bcq, bckv, bkr, bg = 1024, 1216, 1344, 1376
    cq, ck, cv, cg = 1632, 1888, 2016, 2144
    mq, mg = 2400, 2656
    cperm = np.concatenate([r(64) + 64 * h for h in C_HEAD_ORDER])
    pad = lambda n: np.full(n, -1)
    cols = np.concatenate([
        aq + r(256), ak + r(256), av + r(256),
        cq + cperm, ck + r(128), cv + r(128),
        mq + r(256),
        ag + r(256), bg + r(256), cg + cperm, mg + r(256),
        bcq + r(192), pad(64), bckv + r(128), bkr + r(32), bkr + r(32), pad(64),
    ])
    assert cols.shape[0] == W_WIDTH
    return cols, cperm


def _take_cols(w, cols, axis=-1):
    axis = axis % w.ndim
    pieces, i = [], 0
    while i < len(cols):
        j = i + 1
        if cols[i] < 0:
            while j < len(cols) and cols[j] < 0:
                j += 1
            shape = w.shape[:axis] + (j - i,) + w.shape[axis + 1:]
            pieces.append(jnp.zeros(shape, w.dtype))
        else:
            while j < len(cols) and cols[j] == cols[j - 1] + 1:
                j += 1
            pieces.append(lax.slice_in_dim(w, int(cols[i]), int(cols[i]) + (j - i), axis=axis))
        i = j
    return jnp.concatenate(pieces, axis=axis)


def _uq_cols():
    r = np.arange
    per = MLA_NOPE + MLA_ROPE
    out = []
    for p in range(2):
        h0, h1 = 2 * p, 2 * p + 1
        out += [per * h0 + r(64), per * h1 + r(64),
                per * h0 + 64 + r(32), per * h1 + 64 + r(32), np.full(64, -1)]
    return np.concatenate(out)


def _ukv_cols():
    r = np.arange
    return np.concatenate([128 * h + r(64) for h in range(4)] + [128 * h + 64 + r(64) for h in range(4)])


def _rope_pieces(positions):
    pos = positions.astype(F32).reshape(1, -1)
    narrow, col0, expand = [], 0, np.zeros((128, 512), np.float32)
    for t, d in enumerate((HEAD_DIM, MLA_ROPE)):
        half = d // 2
        inv = ROPE_THETA ** (-jnp.arange(0, d, 2, dtype=F32) / d)
        ang = inv[:, None] * pos
        narrow += [jnp.cos(ang), jnp.sin(ang)]
        lane = np.arange(128)
        k = (lane % d) % half
        expand[col0 + k, 256 * t + lane] = 1.0
        expand[col0 + half + k, 256 * t + 128 + lane] = np.where(lane % d < half, -1.0, 1.0)
        col0 += d
    x = jnp.pad(jnp.concatenate(narrow, axis=0), ((0, 128 - col0), (0, 0)))
    x = lax.optimization_barrier(x)
    hi = x.astype(BF16)
    rest = x - hi.astype(F32)
    mid = rest.astype(BF16)
    lo = (rest - mid.astype(F32)).astype(BF16)
    return jnp.concatenate([hi, mid, lo], axis=0), jnp.asarray(np.concatenate([expand] * 3, axis=0), BF16)


def _bias_table_a(rel_bias):
    width, period = 9 * 128, 9 * 128 + TQ
    k = np.arange(period)
    d = np.where(k < width, A_PREV * CHUNK - k, A_PREV * CHUNK + period - k)
    idx = np.clip(d, -REL_CLIP, REL_CLIP) + REL_CLIP
    n_hi = A_PREV * CHUNK - REL_CLIP + 1
    n_lo = width - n_hi - (2 * REL_CLIP - 1)
    expect = np.concatenate([np.full(n_hi, 2 * REL_CLIP), np.arange(2 * REL_CLIP - 1, 0, -1),
                             np.zeros(n_lo, np.int64), np.full(period - width, 2 * REL_CLIP)])
    assert np.array_equal(idx, expect)
    rep = lambda col, n: jnp.broadcast_to(rel_bias[:, :, col:col + 1], rel_bias.shape[:2] + (n,))
    gp = jnp.concatenate([rep(2 * REL_CLIP, n_hi), jnp.flip(rel_bias[:, :, 1:2 * REL_CLIP], axis=-1),
                          rep(0, n_lo), rep(2 * REL_CLIP, period - width)], axis=-1) * LOG2E
    flat = jnp.tile(gp, (1, 1, TQ))[:, :, :TQ * (period - 1)]
    skew = flat.reshape(gp.shape[0], N_HEADS, TQ, period - 1)[..., :width]
    i = np.arange(TQ)[:, None]
    m = np.arange(width)[None, :]
    dchunk = i // CHUNK + A_PREV - m // CHUNK
    valid = (dchunk >= 0) & (dchunk <= A_PREV)
    t = jnp.where(jnp.asarray(valid)[None, None], skew, NEG_INF)
    return jnp.transpose(t, (0, 3, 1, 2)).reshape(gp.shape[0], 9, 128, N_HEADS * TQ)


def _mask_table_c():
    m = np.arange(3 * 128)[:, None]
    i = np.arange(TQ)[None, :]
    dchunk = i // CHUNK + SWA_PREV - m // CHUNK
    valid = (dchunk >= 0) & (dchunk <= SWA_PREV)
    t = np.where(valid, 0.0, NEG_INF).astype(np.float32)
    return np.tile(t, (1, N_HEADS)).reshape(3, 128, N_HEADS * TQ)


def _dot(a, b):
    return jnp.dot(a, b, preferred_element_type=F32)


def _dot_nt(a, b):
    return lax.dot_general(a, b, (((1,), (1,)), ((), ())), preferred_element_type=F32)


def _dot_tn(a, b):
    return lax.dot_general(a, b, (((0,), (0,)), ((), ())), preferred_element_type=F32)


def _rope(x, cos, sin_signed, half):
    lane = lax.broadcasted_iota(jnp.int32, x.shape, 1)
    first = (lane & (2 * half - 1)) < half
    swapped = jnp.where(first, pltpu.roll(x, 128 - half, 1), pltpu.roll(x, half, 1))
    return x * cos + swapped * sin_signed


def _project(xb, tab_ref, w_ref, wuq_ref, wukv_ref, gq_ref, gkv_ref, p_ref, rows=slice(None)):
    tab_ref, p_ref = tab_ref.at[rows], p_ref.at[rows]
    cos64, sin64 = tab_ref[:, 0:128], tab_ref[:, 128:256]
    cos32, sin32 = tab_ref[:, 256:384], tab_ref[:, 384:512]

    def mm(lo, hi):
        return _dot(xb, w_ref[:, lo:hi])

    rb = mm(W_B, W_B + 512)
    cq = rb[:, 0:256]
    ms = jnp.sum(cq * cq, axis=-1, keepdims=True) * (1.0 / MLA_Q_RANK)
    qn = (cq * lax.rsqrt(ms + 1e-6) * gq_ref[...]).astype(BF16)
    ckv = rb[:, 256:384]
    ms = jnp.mean(ckv * ckv, axis=-1, keepdims=True)
    kvn = (ckv * lax.rsqrt(ms + 1e-6) * gkv_ref[...]).astype(BF16)
    krb = _rope(rb[:, 384:512], cos32, sin32, 16).astype(BF16)

    r = mm(W_A, W_A + 768)
    p_ref[:, P_AQ:P_AQ + 256] = (r[:, 0:256] * QSCALE).astype(BF16)
    p_ref[:, P_AK:P_AK + 512] = r[:, 256:768].astype(BF16)

    q = _dot(qn, wuq_ref[...]) * B_QSCALE
    for p in range(2):
        base = P_BQ0 + 256 * p
        p_ref[:, base:base + 128] = q[:, 256 * p:256 * p + 128].astype(BF16)
        p_ref[:, base + 128:base + 256] = _rope(q[:, 256 * p + 128:256 * p + 256], cos32, sin32, 16).astype(BF16)
    kv = _dot(kvn, wukv_ref[...])
    for p in range(2):
        base = P_BK0 + 256 * p
        p_ref[:, base:base + 128] = kv[:, 128 * p:128 * (p + 1)].astype(BF16)
        p_ref[:, base + 128:base + 256] = krb
    p_ref[:, P_BV:P_BV + 256] = kv[:, 256:512].astype(BF16)

    r = mm(W_G, W_G + 1024)
    p_ref[:, P_GA:P_GA + 1024] = (r * (1.0 / (1.0 + jnp.exp(-r)))).astype(BF16)

    r = mm(W_C, W_C + 512)
    for j in range(2):
        qj = _rope(r[:, 128 * j:128 * (j + 1)], cos64, sin64, 32)
        p_ref[:, P_CQ + 128 * j:P_CQ + 128 * (j + 1)] = (qj * QSCALE).astype(BF16)
    p_ref[:, P_CK:P_CK + 128] = _rope(r[:, 256:384], cos64, sin64, 32).astype(BF16)
    p_ref[:, P_CV:P_CV + 128] = r[:, 384:512].astype(BF16)

    r = mm(W_M, W_M + 256)
    p_ref[:, P_MQ:P_MQ + 256] = (r * QSCALE).astype(BF16)


def _inproj_kernel(x_ref, pieces_ref, expand_ref, w_ref, wuq_ref, wukv_ref, gq_ref, gkv_ref, p_ref, tab_ref):
    tab_ref[...] = _dot_tn(pieces_ref[...], expand_ref[...])
    _project(x_ref[...].astype(BF16), tab_ref, w_ref, wuq_ref, wukv_ref, gq_ref, gkv_ref, p_ref)


def _inproj(x2d, pieces, expand3, w, wuq, wukv, gq, gkv):
    n = x2d.shape[0]
    const = lambda shape: pl.BlockSpec(shape, lambda i: (0,) * len(shape))
    return pl.pallas_call(
        _inproj_kernel,
        grid=(n // TM,),
        in_specs=[
            pl.BlockSpec((TM, D_MODEL), lambda i: (i, 0)),
            pl.BlockSpec((384, TM), lambda i: (0, i)),
            const((384, 512)),
            const((D_MODEL, W_WIDTH)),
            const((256, 512)),
            const((128, 512)),
            const((1, 256)),
            const((1, 128)),
        ],
        out_specs=[pl.BlockSpec((TM, P_WIDTH), lambda i: (i, 0)),
                   pl.BlockSpec((TM, 512), lambda i: (i, 0))],
        out_shape=[jax.ShapeDtypeStruct((n, P_WIDTH), BF16),
                   jax.ShapeDtypeStruct((n, 512), F32)],
        compiler_params=pltpu.CompilerParams(
            dimension_semantics=("parallel",), vmem_limit_bytes=VMEM_LIMIT),
        name="inproj",
    )(x2d, pieces, expand3, w, wuq, wukv, gq, gkv)


def _memkv_kernel(mem_ref, w_ref, o_ref):
    o_ref[0] = _dot(mem_ref[0].astype(BF16), w_ref[...]).astype(BF16)


def _memkv(mem, w_all):
    b = mem.shape[0]
    n = w_all.shape[1]
    return pl.pallas_call(
        _memkv_kernel,
        grid=(b,),
        in_specs=[pl.BlockSpec((1, MEM_LEN, D_MODEL), lambda i: (i, 0, 0)),
                  pl.BlockSpec((D_MODEL, n), lambda i: (0, 0))],
        out_specs=pl.BlockSpec((1, MEM_LEN, n), lambda i: (i, 0, 0)),
        out_shape=jax.ShapeDtypeStruct((b, MEM_LEN, n), BF16),
        compiler_params=pltpu.CompilerParams(
            dimension_semantics=("parallel",), vmem_limit_bytes=VMEM_LIMIT),
        name="memkv",
    )(mem, w_all)


def _window_attn_kernel(*refs, win, prev, dk, has_table, has_sink):
    refs = list(refs)
    sink_ref = refs.pop(0) if has_sink else None
    q_ref, k_ref, v_ref, g_ref = refs[:4]
    e_ref = refs[4] if has_table else None
    o_ref, sa_sc, sb_sc = refs[-3:]
    n_items = q_ref.shape[1] // TQ
    lanes = N_HEADS * TQ

    lane128 = lax.broadcasted_iota(jnp.int32, (TQ, 128), 1)
    lo, hi = lane128 < HEAD_DIM, lane128 >= HEAD_DIM
    if has_sink:
        col = lax.broadcasted_iota(jnp.int32, (1, lanes), 1)
        order = C_HEAD_ORDER if dk == 128 else tuple(range(N_HEADS))
        sink = jnp.where(col < TQ, sink_ref[order[0]],
                         jnp.where(col < 2 * TQ, sink_ref[order[1]],
                                   jnp.where(col < 3 * TQ, sink_ref[order[2]], sink_ref[order[3]]))) * LOG2E

    def window_start(item):
        if prev is None:
            return 0
        return pl.multiple_of(jnp.maximum(item * TQ - prev, 0), 128)

    def scores_into(item, s_sc):
        q = q_ref[0, pl.ds(pl.multiple_of(item * TQ, TQ), TQ), :].astype(F32)
        if dk == 256:
            zero = jnp.zeros((TQ, 128), F32)
            blocks = [jnp.concatenate([jnp.where(lo, q[:, 0:128], 0.0), zero], axis=1),
                      jnp.concatenate([jnp.where(hi, q[:, 0:128], 0.0), zero], axis=1),
                      jnp.concatenate([zero, jnp.where(lo, q[:, 128:256], 0.0)], axis=1),
                      jnp.concatenate([zero, jnp.where(hi, q[:, 128:256], 0.0)], axis=1)]
        else:
            blocks = [jnp.where(lo, q[:, 0:128], 0.0), jnp.where(hi, q[:, 0:128], 0.0),
                      jnp.where(lo, q[:, 128:256], 0.0), jnp.where(hi, q[:, 128:256], 0.0)]
        qs = jnp.concatenate(blocks, axis=0).astype(BF16)
        s = _dot_nt(k_ref[0, pl.ds(window_start(item), win), :], qs)
        if has_table:
            mb0 = jnp.maximum(prev // 128 - item, 0)
            s = s + jnp.concatenate([e_ref[mb0 + jb] for jb in range(win // 128)], axis=0)
        s_sc[...] = s

    def consume(item, s_sc):
        start = window_start(item)
        v = v_ref[0, pl.ds(start, win), :]
        parts = []
        for pr in range(2):
            cols = slice(256 * pr, 256 * (pr + 1))
            m = jnp.max(s_sc[:, cols], axis=0, keepdims=True)
            if has_sink:
                m = jnp.maximum(m, sink[:, cols])
            p = jnp.exp2(s_sc[:, cols] - m)
            l = jnp.sum(p, axis=0, keepdims=True)
            if has_sink:
                l = l + jnp.exp2(sink[:, cols] - m)
            inv = 1.0 / l
            vp = v[:, 128 * pr:128 * (pr + 1)] if dk == 256 else v
            ot = _dot_tn(vp, p.astype(BF16))
            for e in range(2):
                parts.append(ot[64 * e:64 * (e + 1), 128 * e:128 * (e + 1)] * inv[:, TQ * e:TQ * (e + 1)])
        o = jnp.concatenate(parts, axis=0).T
        rows = pl.ds(pl.multiple_of(item * TQ, TQ), TQ)
        o_ref[0, rows, :] = (o * g_ref[0, rows, :].astype(F32)).astype(BF16)

    bufs = (sa_sc, sb_sc)
    scores_into(0, bufs[0])

    def body(i, carry):
        for j in range(WIN_UNROLL):
            item = WIN_UNROLL * i + j
            scores_into(jnp.minimum(item + 1, n_items - 1), bufs[(j + 1) % 2])
            consume(item, bufs[j % 2])
        return carry

    lax.fori_loop(0, n_items // WIN_UNROLL, body, 0)


def _window_attn(q_src, q_col, k_src, k_col, v_src, v_col, g_col, *, win, prev, dk, table=None, sinks=None):
    b, s, _ = q_src.shape
    skv = k_src.shape[1]
    assert s % (WIN_UNROLL * TQ) == 0 and skv >= win
    kern = functools.partial(_window_attn_kernel, win=win, prev=prev, dk=dk,
                             has_table=table is not None, has_sink=sinks is not None)
    in_specs, args = [], []
    if sinks is not None:
        in_specs.append(pl.BlockSpec(memory_space=pltpu.SMEM)); args.append(sinks)
    in_specs += [pl.BlockSpec((1, s, 256), lambda i: (i, 0, q_col // 256)),
                 pl.BlockSpec((1, skv, dk), lambda i: (i, 0, k_col // dk)),
                 pl.BlockSpec((1, skv, dk), lambda i: (i, 0, v_col // dk)),
                 pl.BlockSpec((1, s, 256), lambda i: (i, 0, g_col // 256))]
    args += [q_src, k_src, v_src, q_src]
    if table is not None:
        in_specs.append(pl.BlockSpec(table.shape, lambda i: (0, 0, 0))); args.append(table)
    return pl.pallas_call(
        kern,
        grid=(b,),
        in_specs=in_specs,
        out_specs=pl.BlockSpec((1, s, 256), lambda i: (i, 0, 0)),
        out_shape=jax.ShapeDtypeStruct((b, s, GROUP), BF16),
        scratch_shapes=[pltpu.VMEM((win, N_HEADS * TQ), F32), pltpu.VMEM((win, N_HEADS * TQ), F32)],
        compiler_params=pltpu.CompilerParams(
            dimension_semantics=("parallel",), vmem_limit_bytes=VMEM_LIMIT),
        name="attn_win%d" % win,
    )(*args)


def _attn_b_kernel(q0_ref, q1_ref, k0_ref, k1_ref, v_ref, g_ref, o_ref,
                   qs_sc, vt_sc, sa_sc, sb_sc, m_sc, l_sc, acc_sc):
    t = pl.program_id(1)
    n_tiles = B_TQ // TQ
    lane = lax.broadcasted_iota(jnp.int32, (TQ, 128), 1)
    q_refs = (q0_ref, q1_ref)
    k_refs = (k0_ref, k1_ref)

    def stack_queries(c):
        stacked = []
        for pr in range(2):
            q = q_refs[pr][0, TQ * c:TQ * (c + 1), :].astype(F32)
            nope, rope = q[:, 0:128], q[:, 128:256]
            head_a = jnp.concatenate([jnp.where(lane < 64, nope, 0.0), jnp.where(lane < 32, rope, 0.0)], axis=1)
            head_b = jnp.concatenate([jnp.where(lane >= 64, nope, 0.0), jnp.where(lane >= 32, rope, 0.0)], axis=1)
            qs = jnp.concatenate([head_a, head_b], axis=0).astype(BF16)
            qs_sc[2 * c + pr] = qs
            stacked.append(qs)
        return stacked

    lane2 = lax.broadcasted_iota(jnp.int32, (1, 2 * TQ), 1)
    hide_first_chunk = jnp.where((lane2 & (TQ - 1)) < CHUNK, NEG_INF, 0.0)

    def scores_into(kb, c, s_sc, nk=B_TK, qs=None):
        start = pl.multiple_of(kb * B_TK, B_TK)
        for pr in range(2):
            q = qs_sc[2 * c + pr] if qs is None else qs[pr]
            s_sc[pr, 0:nk, :] = _dot_nt(k_refs[pr][0, pl.ds(start, nk), :], q)

    def transpose_values(kb, slot=0):
        vt_sc[slot] = v_ref[0, pl.ds(pl.multiple_of(kb * B_TK, B_TK), B_TK), :].T

    def consume(kb, c, s_sc, nk=B_TK, diagonal=False, slot=0):
        for pr in range(2):
            u = 2 * c + pr
            s = s_sc[pr, 0:nk, :]
            if diagonal:
                s = jnp.concatenate([s[:nk - CHUNK], s[nk - CHUNK:] + hide_first_chunk], axis=0)
            m_prev = m_sc[u]
            m_new = jnp.maximum(m_prev, jnp.max(s, axis=0, keepdims=True))
            alpha = jnp.exp2(m_prev - m_new)
            p = jnp.exp2(s - m_new)
            l_sc[u] = alpha * l_sc[u] + jnp.sum(p, axis=0, keepdims=True)
            m_sc[u] = m_new
            acc_sc[u] = alpha * acc_sc[u] + _dot(vt_sc[slot, 128 * pr:128 * (pr + 1), 0:nk], p.astype(BF16))

    bufs = (sa_sc, sb_sc)
    scores_into(0, 0, bufs[0], qs=stack_queries(0))
    for c in range(1, n_tiles):
        stack_queries(c)
    m_sc[...] = jnp.full(m_sc.shape, NEG_INF, F32)
    l_sc[...] = jnp.zeros(l_sc.shape, F32)
    acc_sc[...] = jnp.zeros(acc_sc.shape, F32)

    def past_block(kb, slot):
        for c in range(n_tiles):
            if c + 1 < n_tiles:
                scores_into(kb, c + 1, bufs[(c + 1) % 2])
            else:
                scores_into(kb + 1, 0, bufs[0])
            consume(kb, c, bufs[c % 2], slot=slot)

    def two_blocks(i, carry):
        transpose_values(2 * i, 0)
        transpose_values(2 * i + 1, 1)
        past_block(2 * i, 0)
        past_block(2 * i + 1, 1)
        return carry

    def one_block(kb, carry):
        transpose_values(kb, 0)
        past_block(kb, 0)
        return carry

    lax.fori_loop(0, t >> 1, two_blocks, 0)
    lax.fori_loop(t & ~1, t, one_block, 0)

    transpose_values(t)
    for c in range(n_tiles):
        if c + 1 < n_tiles:
            scores_into(t, c + 1, bufs[(c + 1) % 2], nk=TQ * (c + 2))
        consume(t, c, bufs[c % 2], nk=TQ * (c + 1), diagonal=True)

    for c in range(n_tiles):
        parts = []
        for pr in range(2):
            u = 2 * c + pr
            inv = 1.0 / l_sc[u]
            for e in range(2):
                parts.append(acc_sc[u, 64 * e:64 * (e + 1), 128 * e:128 * (e + 1)] * inv[:, 128 * e:128 * (e + 1)])
        o = jnp.concatenate(parts, axis=0).T
        o_ref[0, TQ * c:TQ * (c + 1), :] = (o * g_ref[0, TQ * c:TQ * (c + 1), :].astype(F32)).astype(BF16)


def _attn_b(p3):
    b, s, _ = p3.shape
    n_units = 2 * (B_TQ // TQ)
    return pl.pallas_call(
        _attn_b_kernel,
        grid=(b, s // B_TQ),
        in_specs=[
            pl.BlockSpec((1, B_TQ, 256), lambda i, t: (i, t, P_BQ0 // 256)),
            pl.BlockSpec((1, B_TQ, 256), lambda i, t: (i, t, P_BQ1 // 256)),
            pl.BlockSpec((1, s, 256), lambda i, t: (i, 0, P_BK0 // 256)),
            pl.BlockSpec((1, s, 256), lambda i, t: (i, 0, P_BK1 // 256)),
            pl.BlockSpec((1, s, 256), lambda i, t: (i, 0, P_BV // 256)),
            pl.BlockSpec((1, B_TQ, 256), lambda i, t: (i, t, P_GB // 256)),
        ],
        out_specs=pl.BlockSpec((1, B_TQ, 256), lambda i, t: (i, t, 0)),
        out_shape=jax.ShapeDtypeStruct((b, s, GROUP), BF16),
        scratch_shapes=[pltpu.VMEM((n_units, 2 * TQ, 256), BF16),
                        pltpu.VMEM((2, GROUP, B_TK), BF16),
                        pltpu.VMEM((2, B_TK, 2 * TQ), F32),
                        pltpu.VMEM((2, B_TK, 2 * TQ), F32),
                        pltpu.VMEM((n_units, 1, 2 * TQ), F32),
                        pltpu.VMEM((n_units, 1, 2 * TQ), F32),
                        pltpu.VMEM((n_units, 128, 2 * TQ), F32)],
        compiler_params=pltpu.CompilerParams(
            dimension_semantics=("parallel", "arbitrary"), vmem_limit_bytes=VMEM_LIMIT),
        name="attn_b",
    )(p3, p3, p3, p3, p3, p3)


def _residual_norm(ya_ref, yb_ref, yc_ref, ym_ref, x_ref, w_ref, g_ref, b_ref, rows=slice(None)):
    y = (_dot(ya_ref[rows, :], w_ref[0:256, :]) + _dot(yb_ref[rows, :], w_ref[256:512, :])
         + _dot(yc_ref[rows, :], w_ref[512:768, :]) + _dot(ym_ref[rows, :], w_ref[768:1024, :]))
    z = ALPHA * x_ref[rows, :] + y
    mu = jnp.mean(z, axis=-1, keepdims=True)
    zc = z - mu
    var = jnp.mean(zc * zc, axis=-1, keepdims=True)
    return zc * lax.rsqrt(var + 1e-5) * g_ref[...] + b_ref[...]


def _outproj_kernel(ya_ref, yb_ref, yc_ref, ym_ref, x_ref, w_ref, g_ref, b_ref, o_ref):
    for h in range(2):
        rows = pl.ds(h * (TM // 2), TM // 2)
        o_ref[rows, :] = _residual_norm(ya_ref, yb_ref, yc_ref, ym_ref, x_ref, w_ref, g_ref, b_ref, rows)


def _out_in_proj_kernel(ya_ref, yb_ref, yc_ref, ym_ref, x_ref, wo_ref, g_ref, b_ref,
                        tab_ref, w_ref, wuq_ref, wukv_ref, gq_ref, gkv_ref, o_ref, p_ref):
    halves = [pl.ds(h * (TM // 2), TM // 2) for h in range(2)]
    xb = []
    for rows in halves:
        xn = _residual_norm(ya_ref, yb_ref, yc_ref, ym_ref, x_ref, wo_ref, g_ref, b_ref, rows)
        o_ref[rows, :] = xn
        xb.append(xn.astype(BF16))
    for rows, x in zip(halves, xb):
        _project(x, tab_ref, w_ref, wuq_ref, wukv_ref, gq_ref, gkv_ref, p_ref, rows)


def _outproj(ya, yb, yc, ym, x2d, w, g, bias):
    n = x2d.shape[0]
    ytile = pl.BlockSpec((TM, GROUP), lambda i: (i, 0))
    const = lambda shape: pl.BlockSpec(shape, lambda i: (0,) * len(shape))
    return pl.pallas_call(
        _outproj_kernel,
        grid=(n // TM,),
        in_specs=[ytile, ytile, ytile, ytile,
                  pl.BlockSpec((TM, D_MODEL), lambda i: (i, 0)),
                  const((D_MODEL, D_MODEL)), const((1, D_MODEL)), const((1, D_MODEL))],
        out_specs=pl.BlockSpec((TM, D_MODEL), lambda i: (i, 0)),
        out_shape=jax.ShapeDtypeStruct((n, D_MODEL), F32),
        compiler_params=pltpu.CompilerParams(
            dimension_semantics=("parallel",), vmem_limit_bytes=VMEM_LIMIT),
        name="outproj",
    )(ya, yb, yc, ym, x2d, w, g, bias)


def _out_in_proj(ya, yb, yc, ym, x2d, wo, g, bias, tab, w, wuq, wukv, gq, gkv):
    n = x2d.shape[0]
    ytile = pl.BlockSpec((TM, GROUP), lambda i: (i, 0))
    const = lambda shape: pl.BlockSpec(shape, lambda i: (0,) * len(shape))
    return pl.pallas_call(
        _out_in_proj_kernel,
        grid=(n // TM,),
        in_specs=[ytile, ytile, ytile, ytile,
                  pl.BlockSpec((TM, D_MODEL), lambda i: (i, 0)),
                  const((D_MODEL, D_MODEL)), const((1, D_MODEL)), const((1, D_MODEL)),
                  pl.BlockSpec((TM, 512), lambda i: (i, 0)),
                  const((D_MODEL, W_WIDTH)), const((256, 512)), const((128, 512)),
                  const((1, 256)), const((1, 128))],
        out_specs=[pl.BlockSpec((TM, D_MODEL), lambda i: (i, 0)),
                   pl.BlockSpec((TM, P_WIDTH), lambda i: (i, 0))],
        out_shape=[jax.ShapeDtypeStruct((n, D_MODEL), F32),
                   jax.ShapeDtypeStruct((n, P_WIDTH), BF16)],
        compiler_params=pltpu.CompilerParams(
            dimension_semantics=("parallel",), vmem_limit_bytes=VMEM_LIMIT),
        name="out_in_proj",
    )(ya, yb, yc, ym, x2d, wo, g, bias, tab, w, wuq, wukv, gq, gkv)


def kernel(x, mem, positions, w_in, rel_bias, mla_q_norm, w_uq, mla_kv_norm, w_ukv,
           swa_sinks, w_mem_kv, w_out, ln_gain, ln_bias):
    b, s, d = x.shape
    depth = w_in.shape[0]
    assert d == D_MODEL and depth == DEPTH and s % B_TK == 0 and s >= A_WIN and (b * s) % TM == 0

    cols, cperm = _inproj_cols()
    w_in_p = _take_cols(w_in, cols).astype(BF16)
    wuq_p = jnp.pad(_take_cols(w_uq, _uq_cols()), ((0, 0), (0, 256 - MLA_Q_RANK), (0, 0))).astype(BF16)
    wukv_p = _take_cols(w_ukv, _ukv_cols()).astype(BF16)
    gq = jnp.pad(mla_q_norm, ((0, 0), (0, 256 - MLA_Q_RANK)))[:, None, :]
    gkv = mla_kv_norm[:, None, :]
    rows = np.concatenate([np.arange(512), 512 + cperm, np.arange(768, 1024)])
    w_out_p = _take_cols(w_out, rows, axis=1).astype(BF16)
    w_mem_all = jnp.transpose(w_mem_kv, (1, 0, 2)).reshape(D_MODEL, depth * 512).astype(BF16)
    e_a = _bias_table_a(rel_bias)
    e_c = jnp.asarray(_mask_table_c())
    pieces, expand3 = _rope_pieces(positions)

    memkv = _memkv(mem, w_mem_all)
    h = x.reshape(b * s, d)
    p2, tab = _inproj(h, pieces, expand3, w_in_p[0], wuq_p[0], wukv_p[0], gq[0], gkv[0])
    for l in range(depth):
        p3 = p2.reshape(b, s, P_WIDTH)
        ya = _window_attn(p3, P_AQ, p3, P_AK, p3, P_AV, P_GA, win=A_WIN, prev=A_PREV * CHUNK, dk=256, table=e_a[l])
        yb = _attn_b(p3)
        yc = _window_attn(p3, P_CQ, p3, P_CK, p3, P_CV, P_GC, win=C_WIN, prev=SWA_PREV * CHUNK, dk=128,
                          table=e_c, sinks=swa_sinks[l])
        ym = _window_attn(p3, P_MQ, memkv, 512 * l, memkv, 512 * l + 256, P_GM, win=MEM_LEN, prev=None, dk=256)
        ys = [y.reshape(b * s, GROUP) for y in (ya, yb, yc, ym)]
        ln = (ln_gain[l][None, :], ln_bias[l][None, :])
        if l + 1 < depth:
            h, p2 = _out_in_proj(*ys, h, w_out_p[l], *ln, tab, w_in_p[l + 1], wuq_p[l + 1], wukv_p[l + 1],
                                 gq[l + 1], gkv[l + 1])
        else:
            h = _outproj(*ys, h, w_out_p[l], *ln)
    return h.reshape(b, s, d)
```

```python
import functools

import numpy as np
import jax
import jax.numpy as jnp
from jax import lax
from jax.experimental import pallas as pl
from jax.experimental.pallas import tpu as pltpu

F32 = jnp.float32
BF16 = jnp.bfloat16

D_MODEL = 1024
DEPTH = 4
CHUNK = 64
HEAD_DIM = 64
GROUP = 256
N_HEADS = 4
ROPE_THETA = 10000.0
NEG_INF = -1e30
A_PREV = 8
REL_CLIP = 128
MLA_NOPE = 64
MLA_ROPE = 32
MLA_Q_RANK = 192
MLA_KV_RANK = 128
SWA_PREV = 2
MEM_LEN = 256
ALPHA = (2.0 * DEPTH) ** 0.25

TQ = 128
A_WIN = TQ + A_PREV * CHUNK
C_WIN = TQ + SWA_PREV * CHUNK
WIN_UNROLL = 8
B_TK = 512
B_TQ = 512
TM = 512
TM_OUT = 1024
VMEM_LIMIT = 56 * 1024 * 1024
LOG2E = 1.4426950408889634
QSCALE = HEAD_DIM ** -0.5 * LOG2E
B_QSCALE = (MLA_NOPE + MLA_ROPE) ** -0.5 * LOG2E

P_AQ, P_AK, P_AV = 0, 256, 512
P_BQ0, P_BQ1, P_BK0, P_BK1, P_BV = 768, 1024, 1280, 1536, 1792
P_CQ, P_CK, P_CV = 2048, 2304, 2432
P_MQ = 2560
P_GA, P_GB, P_GC, P_GM = 2816, 3072, 3328, 3584
P_WIDTH = 3840

W_A, W_C, W_M, W_G, W_B, W_WIDTH = 0, 768, 1280, 1536, 2560, 3072

C_HEAD_ORDER = (0, 2, 1, 3)


def _inproj_cols():
    r = np.arange
    aq, ak, av, ag = 0, 256, 512, 768
    bcq, bckv, bkr, bg = 1024, 1216, 1344, 1376
    cq, ck, cv, cg = 1632, 1888, 2016, 2144
    mq, mg = 2400, 2656
    cperm = np.concatenate([r(64) + 64 * h for h in C_HEAD_ORDER])
    pad = lambda n: np.full(n, -1)
    cols = np.concatenate([
        aq + r(256), ak + r(256), av + r(256),
        cq + cperm, ck + r(128), cv + r(128),
        mq + r(256),
        ag + r(256), bg + r(256), cg + cperm, mg + r(256),
        bcq + r(192), pad(64), bckv + r(128), bkr + r(32), bkr + r(32), pad(64),
    ])
    assert cols.shape[0] == W_WIDTH
    return cols, cperm


def _take_cols(w, cols, axis=-1):
    axis = axis % w.ndim
    pieces, i = [], 0
    while i < len(cols):
        j = i + 1
        if cols[i] < 0:
            while j < len(cols) and cols[j] < 0:
                j += 1
            shape = w.shape[:axis] + (j - i,) + w.shape[axis + 1:]
            pieces.append(jnp.zeros(shape, w.dtype))
        else:
            while j < len(cols) and cols[j] == cols[j - 1] + 1:
                j += 1
            pieces.append(lax.slice_in_dim(w, int(cols[i]), int(cols[i]) + (j - i), axis=axis))
        i = j
    return jnp.concatenate(pieces, axis=axis)


def _uq_cols():
    r = np.arange
    per = MLA_NOPE + MLA_ROPE
    out = []
    for p in range(2):
        h0, h1 = 2 * p, 2 * p + 1
        out += [per * h0 + r(64), per * h1 + r(64),
                per * h0 + 64 + r(32), per * h1 + 64 + r(32), np.full(64, -1)]
    return np.concatenate(out)


def _ukv_cols():
    r = np.arange
    return np.concatenate([128 * h + r(64) for h in range(4)] + [128 * h + 64 + r(64) for h in range(4)])


def _rope_pieces(positions):
    pos = positions.astype(F32).reshape(1, -1)
    narrow, col0, expand = [], 0, np.zeros((128, 512), np.float32)
    for t, d in enumerate((HEAD_DIM, MLA_ROPE)):
        half = d // 2
        inv = ROPE_THETA ** (-jnp.arange(0, d, 2, dtype=F32) / d)
        ang = inv[:, None] * pos
        narrow += [jnp.cos(ang), jnp.sin(ang)]
        lane = np.arange(128)
        k = (lane % d) % half
        expand[col0 + k, 256 * t + lane] = 1.0
        expand[col0 + half + k, 256 * t + 128 + lane] = np.where(lane % d < half, -1.0, 1.0)
        col0 += d
    x = jnp.pad(jnp.concatenate(narrow, axis=0), ((0, 128 - col0), (0, 0)))
    x = lax.optimization_barrier(x)
    hi = x.astype(BF16)
    rest = x - hi.astype(F32)
    mid = rest.astype(BF16)
    lo = (rest - mid.astype(F32)).astype(BF16)
    return jnp.concatenate([hi, mid, lo], axis=0), jnp.asarray(np.concatenate([expand] * 3, axis=0), BF16)


def _bias_table_a(rel_bias):
    width, period = 9 * 128, 9 * 128 + TQ
    k = np.arange(period)
    d = np.where(k < width, A_PREV * CHUNK - k, A_PREV * CHUNK + period - k)
    idx = np.clip(d, -REL_CLIP, REL_CLIP) + REL_CLIP
    n_hi = A_PREV * CHUNK - REL_CLIP + 1
    n_lo = width - n_hi - (2 * REL_CLIP - 1)
    expect = np.concatenate([np.full(n_hi, 2 * REL_CLIP), np.arange(2 * REL_CLIP - 1, 0, -1),
                             np.zeros(n_lo, np.int64), np.full(period - width, 2 * REL_CLIP)])
    assert np.array_equal(idx, expect)
    rep = lambda col, n: jnp.broadcast_to(rel_bias[:, :, col:col + 1], rel_bias.shape[:2] + (n,))
    gp = jnp.concatenate([rep(2 * REL_CLIP, n_hi), jnp.flip(rel_bias[:, :, 1:2 * REL_CLIP], axis=-1),
                          rep(0, n_lo), rep(2 * REL_CLIP, period - width)], axis=-1) * LOG2E
    flat = jnp.tile(gp, (1, 1, TQ))[:, :, :TQ * (period - 1)]
    skew = flat.reshape(gp.shape[0], N_HEADS, TQ, period - 1)[..., :width]
    i = np.arange(TQ)[:, None]
    m = np.arange(width)[None, :]
    dchunk = i // CHUNK + A_PREV - m // CHUNK
    valid = (dchunk >= 0) & (dchunk <= A_PREV)
    t = jnp.where(jnp.asarray(valid)[None, None], skew, NEG_INF)
    return jnp.transpose(t, (0, 3, 1, 2)).reshape(gp.shape[0], 9, 128, N_HEADS * TQ)


def _mask_table_c():
    m = np.arange(3 * 128)[:, None]
    i = np.arange(TQ)[None, :]
    dchunk = i // CHUNK + SWA_PREV - m // CHUNK
    valid = (dchunk >= 0) & (dchunk <= SWA_PREV)
    t = np.where(valid, 0.0, NEG_INF).astype(np.float32)
    return np.tile(t, (1, N_HEADS)).reshape(3, 128, N_HEADS * TQ)


def _dot(a, b):
    return jnp.dot(a, b, preferred_element_type=F32)


def _dot_nt(a, b):
    return lax.dot_general(a, b, (((1,), (1,)), ((), ())), preferred_element_type=F32)


def _dot_tn(a, b):
    return lax.dot_general(a, b, (((0,), (0,)), ((), ())), preferred_element_type=F32)


def _rope(x, cos, sin_signed, half):
    lane = lax.broadcasted_iota(jnp.int32, x.shape, 1)
    first = (lane & (2 * half - 1)) < half
    swapped = jnp.where(first, pltpu.roll(x, 128 - half, 1), pltpu.roll(x, half, 1))
    return x * cos + swapped * sin_signed


def _project(xb, tab_ref, w_ref, wuq_ref, wukv_ref, gq_ref, gkv_ref, p_ref, rows=slice(None)):
    tab_ref, p_ref = tab_ref.at[rows], p_ref.at[rows]
    cos64, sin64 = tab_ref[:, 0:128], tab_ref[:, 128:256]
    cos32, sin32 = tab_ref[:, 256:384], tab_ref[:, 384:512]

    def mm(lo, hi):
        return _dot(xb, w_ref[:, lo:hi])

    rb = mm(W_B, W_B + 512)
    cq = rb[:, 0:256]
    ms = jnp.sum(cq * cq, axis=-1, keepdims=True) * (1.0 / MLA_Q_RANK)
    qn = (cq * lax.rsqrt(ms + 1e-6) * gq_ref[...]).astype(BF16)
    ckv = rb[:, 256:384]
    ms = jnp.mean(ckv * ckv, axis=-1, keepdims=True)
    kvn = (ckv * lax.rsqrt(ms + 1e-6) * gkv_ref[...]).astype(BF16)
    krb = _rope(rb[:, 384:512], cos32, sin32, 16).astype(BF16)

    r = mm(W_A, W_A + 768)
    p_ref[:, P_AQ:P_AQ + 256] = (r[:, 0:256] * QSCALE).astype(BF16)
    p_ref[:, P_AK:P_AK + 512] = r[:, 256:768].astype(BF16)

    q = _dot(qn, wuq_ref[...]) * B_QSCALE
    for p in range(2):
        base = P_BQ0 + 256 * p
        p_ref[:, base:base + 128] = q[:, 256 * p:256 * p + 128].astype(BF16)
        p_ref[:, base + 128:base + 256] = _rope(q[:, 256 * p + 128:256 * p + 256], cos32, sin32, 16).astype(BF16)
    kv = _dot(kvn, wukv_ref[...])
    for p in range(2):
        base = P_BK0 + 256 * p
        p_ref[:, base:base + 128] = kv[:, 128 * p:128 * (p + 1)].astype(BF16)
        p_ref[:, base + 128:base + 256] = krb
    p_ref[:, P_BV:P_BV + 256] = kv[:, 256:512].astype(BF16)

    r = mm(W_G, W_G + 1024)
    p_ref[:, P_GA:P_GA + 1024] = (r * (1.0 / (1.0 + jnp.exp(-r)))).astype(BF16)

    r = mm(W_C, W_C + 512)
    for j in range(2):
        qj = _rope(r[:, 128 * j:128 * (j + 1)], cos64, sin64, 32)
        p_ref[:, P_CQ + 128 * j:P_CQ + 128 * (j + 1)] = (qj * QSCALE).astype(BF16)
    p_ref[:, P_CK:P_CK + 128] = _rope(r[:, 256:384], cos64, sin64, 32).astype(BF16)
    p_ref[:, P_CV:P_CV + 128] = r[:, 384:512].astype(BF16)

    r = mm(W_M, W_M + 256)
    p_ref[:, P_MQ:P_MQ + 256] = (r * QSCALE).astype(BF16)


def _inproj_kernel(x_ref, pieces_ref, expand_ref, w_ref, wuq_ref, wukv_ref, gq_ref, gkv_ref, p_ref, tab_ref):
    tab_ref[...] = _dot_tn(pieces_ref[...], expand_ref[...])
    _project(x_ref[...].astype(BF16), tab_ref, w_ref, wuq_ref, wukv_ref, gq_ref, gkv_ref, p_ref)


def _inproj(x2d, pieces, expand3, w, wuq, wukv, gq, gkv):
    n = x2d.shape[0]
    const = lambda shape: pl.BlockSpec(shape, lambda i: (0,) * len(shape))
    return pl.pallas_call(
        _inproj_kernel,
        grid=(n // TM,),
        in_specs=[
            pl.BlockSpec((TM, D_MODEL), lambda i: (i, 0)),
            pl.BlockSpec((384, TM), lambda i: (0, i)),
            const((384, 512)),
            const((D_MODEL, W_WIDTH)),
            const((256, 512)),
            const((128, 512)),
            const((1, 256)),
            const((1, 128)),
        ],
        out_specs=[pl.BlockSpec((TM, P_WIDTH), lambda i: (i, 0)),
                   pl.BlockSpec((TM, 512), lambda i: (i, 0))],
        out_shape=[jax.ShapeDtypeStruct((n, P_WIDTH), BF16),
                   jax.ShapeDtypeStruct((n, 512), F32)],
        compiler_params=pltpu.CompilerParams(
            dimension_semantics=("parallel",), vmem_limit_bytes=VMEM_LIMIT),
        name="inproj",
    )(x2d, pieces, expand3, w, wuq, wukv, gq, gkv)


def _memkv_kernel(mem_ref, w_ref, o_ref):
    o_ref[0] = _dot(mem_ref[0].astype(BF16), w_ref[...]).astype(BF16)


def _memkv(mem, w_all):
    b = mem.shape[0]
    n = w_all.shape[1]
    return pl.pallas_call(
        _memkv_kernel,
        grid=(b,),
        in_specs=[pl.BlockSpec((1, MEM_LEN, D_MODEL), lambda i: (i, 0, 0)),
                  pl.BlockSpec((D_MODEL, n), lambda i: (0, 0))],
        out_specs=pl.BlockSpec((1, MEM_LEN, n), lambda i: (i, 0, 0)),
        out_shape=jax.ShapeDtypeStruct((b, MEM_LEN, n), BF16),
        compiler_params=pltpu.CompilerParams(
            dimension_semantics=("parallel",), vmem_limit_bytes=VMEM_LIMIT),
        name="memkv",
    )(mem, w_all)


def _window_attn_kernel(*refs, win, prev, dk, has_table, has_sink):
    refs = list(refs)
    sink_ref = refs.pop(0) if has_sink else None
    q_ref, k_ref, v_ref, g_ref = refs[:4]
    e_ref = refs[4] if has_table else None
    o_ref, sa_sc, sb_sc = refs[-3:]
    n_items = q_ref.shape[1] // TQ
    lanes = N_HEADS * TQ

    lane128 = lax.broadcasted_iota(jnp.int32, (TQ, 128), 1)
    lo, hi = lane128 < HEAD_DIM, lane128 >= HEAD_DIM
    if has_sink:
        col = lax.broadcasted_iota(jnp.int32, (1, lanes), 1)
        order = C_HEAD_ORDER if dk == 128 else tuple(range(N_HEADS))
        sink = jnp.where(col < TQ, sink_ref[order[0]],
                         jnp.where(col < 2 * TQ, sink_ref[order[1]],
                                   jnp.where(col < 3 * TQ, sink_ref[order[2]], sink_ref[order[3]]))) * LOG2E

    def window_start(item):
        if prev is None:
            return 0
        return pl.multiple_of(jnp.maximum(item * TQ - prev, 0), 128)

    def scores_into(item, s_sc):
        q = q_ref[0, pl.ds(pl.multiple_of(item * TQ, TQ), TQ), :].astype(F32)
        if dk == 256:
            zero = jnp.zeros((TQ, 128), F32)
            blocks = [jnp.concatenate([jnp.where(lo, q[:, 0:128], 0.0), zero], axis=1),
                      jnp.concatenate([jnp.where(hi, q[:, 0:128], 0.0), zero], axis=1),
                      jnp.concatenate([zero, jnp.where(lo, q[:, 128:256], 0.0)], axis=1),
                      jnp.concatenate([zero, jnp.where(hi, q[:, 128:256], 0.0)], axis=1)]
        else:
            blocks = [jnp.where(lo, q[:, 0:128], 0.0), jnp.where(hi, q[:, 0:128], 0.0),
                      jnp.where(lo, q[:, 128:256], 0.0), jnp.where(hi, q[:, 128:256], 0.0)]
        qs = jnp.concatenate(blocks, axis=0).astype(BF16)
        s = _dot_nt(k_ref[0, pl.ds(window_start(item), win), :], qs)
        if has_table:
            mb0 = jnp.maximum(prev // 128 - item, 0)
            s = s + jnp.concatenate([e_ref[mb0 + jb] for jb in range(win // 128)], axis=0)
        s_sc[...] = s

    def consume(item, s_sc):
        start = window_start(item)
        v = v_ref[0, pl.ds(start, win), :]
        parts = []
        for pr in range(2):
            cols = slice(256 * pr, 256 * (pr + 1))
            m = jnp.max(s_sc[:, cols], axis=0, keepdims=True)
            if has_sink:
                m = jnp.maximum(m, sink[:, cols])
            p = jnp.exp2(s_sc[:, cols] - m)
            l = jnp.sum(p, axis=0, keepdims=True)
            if has_sink:
                l = l + jnp.exp2(sink[:, cols] - m)
            inv = 1.0 / l
            vp = v[:, 128 * pr:128 * (pr + 1)] if dk == 256 else v
            ot = _dot_tn(vp, p.astype(BF16))
            for e in range(2):
                parts.append(ot[64 * e:64 * (e + 1), 128 * e:128 * (e + 1)] * inv[:, TQ * e:TQ * (e + 1)])
        o = jnp.concatenate(parts, axis=0).T
        rows = pl.ds(pl.multiple_of(item * TQ, TQ), TQ)
        o_ref[0, rows, :] = (o * g_ref[0, rows, :].astype(F32)).astype(BF16)

    bufs = (sa_sc, sb_sc)
    scores_into(0, bufs[0])

    def body(i, carry):
        for j in range(WIN_UNROLL):
            item = WIN_UNROLL * i + j
            scores_into(jnp.minimum(item + 1, n_items - 1), bufs[(j + 1) % 2])
            consume(item, bufs[j % 2])
        return carry

    lax.fori_loop(0, n_items // WIN_UNROLL, body, 0)


def _window_attn(q_src, q_col, k_src, k_col, v_src, v_col, g_col, *, win, prev, dk, table=None, sinks=None):
    b, s, _ = q_src.shape
    skv = k_src.shape[1]
    assert s % (WIN_UNROLL * TQ) == 0 and skv >= win
    kern = functools.partial(_window_attn_kernel, win=win, prev=prev, dk=dk,
                             has_table=table is not None, has_sink=sinks is not None)
    in_specs, args = [], []
    if sinks is not None:
        in_specs.append(pl.BlockSpec(memory_space=pltpu.SMEM)); args.append(sinks)
    in_specs += [pl.BlockSpec((1, s, 256), lambda i: (i, 0, q_col // 256)),
                 pl.BlockSpec((1, skv, dk), lambda i: (i, 0, k_col // dk)),
                 pl.BlockSpec((1, skv, dk), lambda i: (i, 0, v_col // dk)),
                 pl.BlockSpec((1, s, 256), lambda i: (i, 0, g_col // 256))]
    args += [q_src, k_src, v_src, q_src]
    if table is not None:
        in_specs.append(pl.BlockSpec(table.shape, lambda i: (0, 0, 0))); args.append(table)
    return pl.pallas_call(
        kern,
        grid=(b,),
        in_specs=in_specs,
        out_specs=pl.BlockSpec((1, s, 256), lambda i: (i, 0, 0)),
        out_shape=jax.ShapeDtypeStruct((b, s, GROUP), BF16),
        scratch_shapes=[pltpu.VMEM((win, N_HEADS * TQ), F32), pltpu.VMEM((win, N_HEADS * TQ), F32)],
        compiler_params=pltpu.CompilerParams(
            dimension_semantics=("parallel",), vmem_limit_bytes=VMEM_LIMIT),
        name="attn_win%d" % win,
    )(*args)


def _attn_b_kernel(q0_ref, q1_ref, k0_ref, k1_ref, v_ref, g_ref, o_ref,
                   qs_sc, vt_sc, sa_sc, sb_sc, m_sc, l_sc, acc_sc):
    t = pl.program_id(1)
    n_tiles = B_TQ // TQ
    lane = lax.broadcasted_iota(jnp.int32, (TQ, 128), 1)
    q_refs = (q0_ref, q1_ref)
    k_refs = (k0_ref, k1_ref)

    def stack_queries(c):
        stacked = []
        for pr in range(2):
            q = q_refs[pr][0, TQ * c:TQ * (c + 1), :].astype(F32)
            nope, rope = q[:, 0:128], q[:, 128:256]
            head_a = jnp.concatenate([jnp.where(lane < 64, nope, 0.0), jnp.where(lane < 32, rope, 0.0)], axis=1)
            head_b = jnp.concatenate([jnp.where(lane >= 64, nope, 0.0), jnp.where(lane >= 32, rope, 0.0)], axis=1)
            qs = jnp.concatenate([head_a, head_b], axis=0).astype(BF16)
            qs_sc[2 * c + pr] = qs
            stacked.append(qs)
        return stacked

    lane2 = lax.broadcasted_iota(jnp.int32, (1, 2 * TQ), 1)
    hide_first_chunk = jnp.where((lane2 & (TQ - 1)) < CHUNK, NEG_INF, 0.0)

    def scores_into(kb, c, s_sc, nk=B_TK, qs=None):
        start = pl.multiple_of(kb * B_TK, B_TK)
        for pr in range(2):
            q = qs_sc[2 * c + pr] if qs is None else qs[pr]
            s_sc[pr, 0:nk, :] = _dot_nt(k_refs[pr][0, pl.ds(start, nk), :], q)

    def transpose_values(kb, slot=0):
        vt_sc[slot] = v_ref[0, pl.ds(pl.multiple_of(kb * B_TK, B_TK), B_TK), :].T

    def consume(kb, c, s_sc, nk=B_TK, diagonal=False, slot=0):
        for pr in range(2):
            u = 2 * c + pr
            s = s_sc[pr, 0:nk, :]
            if diagonal:
                s = jnp.concatenate([s[:nk - CHUNK], s[nk - CHUNK:] + hide_first_chunk], axis=0)
            m_prev = m_sc[u]
            m_new = jnp.maximum(m_prev, jnp.max(s, axis=0, keepdims=True))
            alpha = jnp.exp2(m_prev - m_new)
            p = jnp.exp2(s - m_new)
            l_sc[u] = alpha * l_sc[u] + jnp.sum(p, axis=0, keepdims=True)
            m_sc[u] = m_new
            acc_sc[u] = alpha * acc_sc[u] + _dot(vt_sc[slot, 128 * pr:128 * (pr + 1), 0:nk], p.astype(BF16))

    bufs = (sa_sc, sb_sc)
    scores_into(0, 0, bufs[0], qs=stack_queries(0))
    for c in range(1, n_tiles):
        stack_queries(c)
    m_sc[...] = jnp.full(m_sc.shape, NEG_INF, F32)
    l_sc[...] = jnp.zeros(l_sc.shape, F32)
    acc_sc[...] = jnp.zeros(acc_sc.shape, F32)

    def past_block(kb, slot):
        for c in range(n_tiles):
            if c + 1 < n_tiles:
                scores_into(kb, c + 1, bufs[(c + 1) % 2])
            else:
                scores_into(kb + 1, 0, bufs[0])
            consume(kb, c, bufs[c % 2], slot=slot)

    def two_blocks(i, carry):
        transpose_values(2 * i, 0)
        transpose_values(2 * i + 1, 1)
        past_block(2 * i, 0)
        past_block(2 * i + 1, 1)
        return carry

    def one_block(kb, carry):
        transpose_values(kb, 0)
        past_block(kb, 0)
        return carry

    lax.fori_loop(0, t >> 1, two_blocks, 0)
    lax.fori_loop(t & ~1, t, one_block, 0)

    transpose_values(t)
    for c in range(n_tiles):
        if c + 1 < n_tiles:
            scores_into(t, c + 1, bufs[(c + 1) % 2], nk=TQ * (c + 2))
        consume(t, c, bufs[c % 2], nk=TQ * (c + 1), diagonal=True)
        parts = []
        for pr in range(2):
            u = 2 * c + pr
            inv = 1.0 / l_sc[u]
            for e in range(2):
                parts.append(acc_sc[u, 64 * e:64 * (e + 1), 128 * e:128 * (e + 1)] * inv[:, 128 * e:128 * (e + 1)])
        o = jnp.concatenate(parts, axis=0).T
        o_ref[0, TQ * c:TQ * (c + 1), :] = (o * g_ref[0, TQ * c:TQ * (c + 1), :].astype(F32)).astype(BF16)


def _attn_b(p3):
    b, s, _ = p3.shape
    n_units = 2 * (B_TQ // TQ)
    return pl.pallas_call(
        _attn_b_kernel,
        grid=(b, s // B_TQ),
        in_specs=[
            pl.BlockSpec((1, B_TQ, 256), lambda i, t: (i, t, P_BQ0 // 256)),
            pl.BlockSpec((1, B_TQ, 256), lambda i, t: (i, t, P_BQ1 // 256)),
            pl.BlockSpec((1, s, 256), lambda i, t: (i, 0, P_BK0 // 256)),
            pl.BlockSpec((1, s, 256), lambda i, t: (i, 0, P_BK1 // 256)),
            pl.BlockSpec((1, s, 256), lambda i, t: (i, 0, P_BV // 256)),
            pl.BlockSpec((1, B_TQ, 256), lambda i, t: (i, t, P_GB // 256)),
        ],
        out_specs=pl.BlockSpec((1, B_TQ, 256), lambda i, t: (i, t, 0)),
        out_shape=jax.ShapeDtypeStruct((b, s, GROUP), BF16),
        scratch_shapes=[pltpu.VMEM((n_units, 2 * TQ, 256), BF16),
                        pltpu.VMEM((2, GROUP, B_TK), BF16),
                        pltpu.VMEM((2, B_TK, 2 * TQ), F32),
                        pltpu.VMEM((2, B_TK, 2 * TQ), F32),
                        pltpu.VMEM((n_units, 1, 2 * TQ), F32),
                        pltpu.VMEM((n_units, 1, 2 * TQ), F32),
                        pltpu.VMEM((n_units, 128, 2 * TQ), F32)],
        compiler_params=pltpu.CompilerParams(
            dimension_semantics=("parallel", "arbitrary"), vmem_limit_bytes=VMEM_LIMIT),
        name="attn_b",
    )(p3, p3, p3, p3, p3, p3)


def _residual_norm(ya_ref, yb_ref, yc_ref, ym_ref, x_ref, w_ref, g_ref, b_ref, rows=slice(None)):
    y = (_dot(ya_ref[rows, :], w_ref[0:256, :]) + _dot(yb_ref[rows, :], w_ref[256:512, :])
         + _dot(yc_ref[rows, :], w_ref[512:768, :]) + _dot(ym_ref[rows, :], w_ref[768:1024, :]))
    z = ALPHA * x_ref[rows, :] + y
    mu = jnp.mean(z, axis=-1, keepdims=True)
    zc = z - mu
    var = jnp.mean(zc * zc, axis=-1, keepdims=True)
    return zc * lax.rsqrt(var + 1e-5) * g_ref[...] + b_ref[...]


def _outproj_kernel(ya_ref, yb_ref, yc_ref, ym_ref, x_ref, w_ref, g_ref, b_ref, o_ref):
    for h in range(TM_OUT // 256):
        rows = pl.ds(h * 256, 256)
        o_ref[rows, :] = _residual_norm(ya_ref, yb_ref, yc_ref, ym_ref, x_ref, w_ref, g_ref, b_ref, rows)


def _out_in_proj_kernel(ya_ref, yb_ref, yc_ref, ym_ref, x_ref, wo_ref, g_ref, b_ref,
                        tab_ref, w_ref, wuq_ref, wukv_ref, gq_ref, gkv_ref, o_ref, p_ref):
    halves = [pl.ds(h * (TM // 2), TM // 2) for h in range(2)]
    xb = []
    for rows in halves:
        xn = _residual_norm(ya_ref, yb_ref, yc_ref, ym_ref, x_ref, wo_ref, g_ref, b_ref, rows)
        o_ref[rows, :] = xn
        xb.append(xn.astype(BF16))
    for rows, x in zip(halves, xb):
        _project(x, tab_ref, w_ref, wuq_ref, wukv_ref, gq_ref, gkv_ref, p_ref, rows)


def _outproj(ya, yb, yc, ym, x2d, w, g, bias):
    n = x2d.shape[0]
    ytile = pl.BlockSpec((TM_OUT, GROUP), lambda i: (i, 0))
    const = lambda shape: pl.BlockSpec(shape, lambda i: (0,) * len(shape))
    return pl.pallas_call(
        _outproj_kernel,
        grid=(n // TM_OUT,),
        in_specs=[ytile, ytile, ytile, ytile,
                  pl.BlockSpec((TM_OUT, D_MODEL), lambda i: (i, 0)),
                  const((D_MODEL, D_MODEL)), const((1, D_MODEL)), const((1, D_MODEL))],
        out_specs=pl.BlockSpec((TM_OUT, D_MODEL), lambda i: (i, 0)),
        out_shape=jax.ShapeDtypeStruct((n, D_MODEL), F32),
        compiler_params=pltpu.CompilerParams(
            dimension_semantics=("parallel",), vmem_limit_bytes=VMEM_LIMIT),
        name="outproj",
    )(ya, yb, yc, ym, x2d, w, g, bias)


def _out_in_proj(ya, yb, yc, ym, x2d, wo, g, bias, tab, w, wuq, wukv, gq, gkv):
    n = x2d.shape[0]
    ytile = pl.BlockSpec((TM, GROUP), lambda i: (i, 0))
    const = lambda shape: pl.BlockSpec(shape, lambda i: (0,) * len(shape))
    return pl.pallas_call(
        _out_in_proj_kernel,
        grid=(n // TM,),
        in_specs=[ytile, ytile, ytile, ytile,
                  pl.BlockSpec((TM, D_MODEL), lambda i: (i, 0)),
                  const((D_MODEL, D_MODEL)), const((1, D_MODEL)), const((1, D_MODEL)),
                  pl.BlockSpec((TM, 512), lambda i: (i, 0)),
                  const((D_MODEL, W_WIDTH)), const((256, 512)), const((128, 512)),
                  const((1, 256)), const((1, 128))],
        out_specs=[pl.BlockSpec((TM, D_MODEL), lambda i: (i, 0)),
                   pl.BlockSpec((TM, P_WIDTH), lambda i: (i, 0))],
        out_shape=[jax.ShapeDtypeStruct((n, D_MODEL), F32),
                   jax.ShapeDtypeStruct((n, P_WIDTH), BF16)],
        compiler_params=pltpu.CompilerParams(
            dimension_semantics=("parallel",), vmem_limit_bytes=VMEM_LIMIT),
        name="out_in_proj",
    )(ya, yb, yc, ym, x2d, wo, g, bias, tab, w, wuq, wukv, gq, gkv)


def kernel(x, mem, positions, w_in, rel_bias, mla_q_norm, w_uq, mla_kv_norm, w_ukv,
           swa_sinks, w_mem_kv, w_out, ln_gain, ln_bias):
    b, s, d = x.shape
    depth = w_in.shape[0]
    assert d == D_MODEL and depth == DEPTH and s % B_TK == 0 and s >= A_WIN and (b * s) % TM_OUT == 0 and TM_OUT % TM == 0

    cols, cperm = _inproj_cols()
    w_in_p = _take_cols(w_in, cols).astype(BF16)
    wuq_p = jnp.pad(_take_cols(w_uq, _uq_cols()), ((0, 0), (0, 256 - MLA_Q_RANK), (0, 0))).astype(BF16)
    wukv_p = _take_cols(w_ukv, _ukv_cols()).astype(BF16)
    gq = jnp.pad(mla_q_norm, ((0, 0), (0, 256 - MLA_Q_RANK)))[:, None, :]
    gkv = mla_kv_norm[:, None, :]
    rows = np.concatenate([np.arange(512), 512 + cperm, np.arange(768, 1024)])
    w_out_p = _take_cols(w_out, rows, axis=1).astype(BF16)
    w_mem_all = jnp.transpose(w_mem_kv, (1, 0, 2)).reshape(D_MODEL, depth * 512).astype(BF16)
    e_a = _bias_table_a(rel_bias)
    e_c = jnp.asarray(_mask_table_c())
    pieces, expand3 = _rope_pieces(positions)

    memkv = _memkv(mem, w_mem_all)
    h = x.reshape(b * s, d)
    p2, tab = _inproj(h, pieces, expand3, w_in_p[0], wuq_p[0], wukv_p[0], gq[0], gkv[0])
    for l in range(depth):
        p3 = p2.reshape(b, s, P_WIDTH)
        ya = _window_attn(p3, P_AQ, p3, P_AK, p3, P_AV, P_GA, win=A_WIN, prev=A_PREV * CHUNK, dk=256, table=e_a[l])
        yb = _attn_b(p3)
        yc = _window_attn(p3, P_CQ, p3, P_CK, p3, P_CV, P_GC, win=C_WIN, prev=SWA_PREV * CHUNK, dk=128,
                          table=e_c, sinks=swa_sinks[l])
        ym = _window_attn(p3, P_MQ, memkv, 512 * l, memkv, 512 * l + 256, P_GM, win=MEM_LEN, prev=None, dk=256)
        ys = [y.reshape(b * s, GROUP) for y in (ya, yb, yc, ym)]
        ln = (ln_gain[l][None, :], ln_bias[l][None, :])
        if l + 1 < depth:
            h, p2 = _out_in_proj(*ys, h, w_out_p[l], *ln, tab, w_in_p[l + 1], wuq_p[l + 1], wukv_p[l + 1],
                                 gq[l + 1], gkv[l + 1])
        else:
            h = _outproj(*ys, h, w_out_p[l], *ln)
    return h.reshape(b, s, d)
```

```python
import functools

import numpy as np
import jax
import jax.numpy as jnp
from jax import lax
from jax.experimental import pallas as pl
from jax.experimental.pallas import tpu as pltpu

F32 = jnp.float32
BF16 = jnp.bfloat16

D_MODEL = 1024
DEPTH = 4
CHUNK = 64
HEAD_DIM = 64
GROUP = 256
N_HEADS = 4
ROPE_THETA = 10000.0
NEG_INF = -1e30
A_PREV = 8
REL_CLIP = 128
MLA_NOPE = 64
MLA_ROPE = 32
MLA_Q_RANK = 192
MLA_KV_RANK = 128
SWA_PREV = 2
MEM_LEN = 256
ALPHA = (2.0 * DEPTH) ** 0.25

TQ = 128
A_WIN = TQ + A_PREV * CHUNK
C_WIN = TQ + SWA_PREV * CHUNK
WIN_UNROLL = 8
B_TK = 512
B_TQ = 512
TM = 512
TM_OUT = 1024
VMEM_LIMIT = 56 * 1024 * 1024
LOG2E = 1.4426950408889634
QSCALE = HEAD_DIM ** -0.5 * LOG2E
B_QSCALE = (MLA_NOPE + MLA_ROPE) ** -0.5 * LOG2E

P_AQ, P_AK, P_AV = 0, 256, 512
P_BQ0, P_BQ1, P_BK0, P_BK1, P_BV = 768, 1024, 1280, 1536, 1792
P_CQ, P_CK, P_CV = 2048, 2304, 2432
P_MQ = 2560
P_GA, P_GB, P_GC, P_GM = 2816, 3072, 3328, 3584
P_WIDTH = 3840

W_A, W_C, W_M, W_G, W_B, W_WIDTH = 0, 768, 1280, 1536, 2560, 3072

C_HEAD_ORDER = (0, 2, 1, 3)


def _inproj_cols():
    r = np.arange
    aq, ak, av, ag = 0, 256, 512, 768
    bcq, bckv, bkr, bg = 1024, 1216, 1344, 1376
    cq, ck, cv, cg = 1632, 1888, 2016, 2144
    mq, mg = 2400, 2656
    cperm = np.concatenate([r(64) + 64 * h for h in C_HEAD_ORDER])
    pad = lambda n: np.full(n, -1)
    cols = np.concatenate([
        aq + r(256), ak + r(256), av + r(256),
        cq + cperm, ck + r(128), cv + r(128),
        mq + r(256),
        ag + r(256), bg + r(256), cg + cperm, mg + r(256),
        bcq + r(192), pad(64), bckv + r(128), bkr + r(32), bkr + r(32), pad(64),
    ])
    assert cols.shape[0] == W_WIDTH
    return cols, cperm


def _take_cols(w, cols, axis=-1):
    axis = axis % w.ndim
    pieces, i = [], 0
    while i < len(cols):
        j = i + 1
        if cols[i] < 0:
            while j < len(cols) and cols[j] < 0:
                j += 1
            shape = w.shape[:axis] + (j - i,) + w.shape[axis + 1:]
            pieces.append(jnp.zeros(shape, w.dtype))
        else:
            while j < len(cols) and cols[j] == cols[j - 1] + 1:
                j += 1
            pieces.append(lax.slice_in_dim(w, int(cols[i]), int(cols[i]) + (j - i), axis=axis))
        i = j
    return jnp.concatenate(pieces, axis=axis)


def _uq_cols():
    r = np.arange
    per = MLA_NOPE + MLA_ROPE
    out = []
    for p in range(2):
        h0, h1 = 2 * p, 2 * p + 1
        out += [per * h0 + r(64), per * h1 + r(64),
                per * h0 + 64 + r(32), per * h1 + 64 + r(32), np.full(64, -1)]
    return np.concatenate(out)


def _ukv_cols():
    r = np.arange
    return np.concatenate([128 * h + r(64) for h in range(4)] + [128 * h + 64 + r(64) for h in range(4)])


def _rope_pieces(positions):
    pos = positions.astype(F32).reshape(1, -1)
    narrow, col0, expand = [], 0, np.zeros((128, 512), np.float32)
    for t, d in enumerate((HEAD_DIM, MLA_ROPE)):
        half = d // 2
        inv = ROPE_THETA ** (-jnp.arange(0, d, 2, dtype=F32) / d)
        ang = inv[:, None] * pos
        narrow += [jnp.cos(ang), jnp.sin(ang)]
        lane = np.arange(128)
        k = (lane % d) % half
        expand[col0 + k, 256 * t + lane] = 1.0
        expand[col0 + half + k, 256 * t + 128 + lane] = np.where(lane % d < half, -1.0, 1.0)
        col0 += d
    x = jnp.pad(jnp.concatenate(narrow, axis=0), ((0, 128 - col0), (0, 0)))
    x = lax.optimization_barrier(x)
    hi = x.astype(BF16)
    rest = x - hi.astype(F32)
    mid = rest.astype(BF16)
    lo = (rest - mid.astype(F32)).astype(BF16)
    return jnp.concatenate([hi, mid, lo], axis=0), jnp.asarray(np.concatenate([expand] * 3, axis=0), BF16)


def _bias_table_a(rel_bias):
    width, period = 9 * 128, 9 * 128 + TQ + 1
    k = np.arange(period)
    d = np.where(k < width, A_PREV * CHUNK - k, A_PREV * CHUNK + period - k)
    idx = np.clip(d, -REL_CLIP, REL_CLIP) + REL_CLIP
    n_hi = A_PREV * CHUNK - REL_CLIP + 1
    n_lo = width - n_hi - (2 * REL_CLIP - 1)
    expect = np.concatenate([np.full(n_hi, 2 * REL_CLIP), np.arange(2 * REL_CLIP - 1, 0, -1),
                             np.zeros(n_lo, np.int64), np.full(period - width, 2 * REL_CLIP)])
    assert np.array_equal(idx, expect)
    rep = lambda col, n: jnp.broadcast_to(rel_bias[:, :, col:col + 1], rel_bias.shape[:2] + (n,))
    gp = jnp.concatenate([rep(2 * REL_CLIP, n_hi), jnp.flip(rel_bias[:, :, 1:2 * REL_CLIP], axis=-1),
                          rep(0, n_lo), rep(2 * REL_CLIP, period - width)], axis=-1) * LOG2E
    flat = jnp.tile(gp, (1, 1, TQ))[:, :, :TQ * (period - 1)]
    skew = flat.reshape(gp.shape[0], N_HEADS, TQ, period - 1)[..., :width]
    i = np.arange(TQ)[:, None]
    m = np.arange(width)[None, :]
    dchunk = i // CHUNK + A_PREV - m // CHUNK
    valid = (dchunk >= 0) & (dchunk <= A_PREV)
    t = jnp.where(jnp.asarray(valid)[None, None], skew, NEG_INF)
    return jnp.transpose(t, (0, 3, 1, 2)).reshape(gp.shape[0], 9, 128, N_HEADS * TQ)


def _mask_table_c():
    m = np.arange(3 * 128)[:, None]
    i = np.arange(TQ)[None, :]
    dchunk = i // CHUNK + SWA_PREV - m // CHUNK
    valid = (dchunk >= 0) & (dchunk <= SWA_PREV)
    t = np.where(valid, 0.0, NEG_INF).astype(np.float32)
    return np.tile(t, (1, N_HEADS)).reshape(3, 128, N_HEADS * TQ)


def _dot(a, b):
    return jnp.dot(a, b, preferred_element_type=F32)


def _dot_nt(a, b):
    return lax.dot_general(a, b, (((1,), (1,)), ((), ())), preferred_element_type=F32)


def _dot_tn(a, b):
    return lax.dot_general(a, b, (((0,), (0,)), ((), ())), preferred_element_type=F32)


def _rope(x, cos, sin_signed, half):
    lane = lax.broadcasted_iota(jnp.int32, x.shape, 1)
    first = (lane & (2 * half - 1)) < half
    swapped = jnp.where(first, pltpu.roll(x, 128 - half, 1), pltpu.roll(x, half, 1))
    return x * cos + swapped * sin_signed


def _project(xb, tab_ref, w_ref, wuq_ref, wukv_ref, gq_ref, gkv_ref, p_ref, rows=slice(None)):
    tab_ref, p_ref = tab_ref.at[rows], p_ref.at[rows]
    cos64, sin64 = tab_ref[:, 0:128], tab_ref[:, 128:256]
    cos32, sin32 = tab_ref[:, 256:384], tab_ref[:, 384:512]

    def mm(lo, hi):
        return _dot(xb, w_ref[:, lo:hi])

    rb = mm(W_B, W_B + 512)
    cq = rb[:, 0:256]
    ms = jnp.sum(cq * cq, axis=-1, keepdims=True) * (1.0 / MLA_Q_RANK)
    qn = (cq * lax.rsqrt(ms + 1e-6) * gq_ref[...]).astype(BF16)
    ckv = rb[:, 256:384]
    ms = jnp.mean(ckv * ckv, axis=-1, keepdims=True)
    kvn = (ckv * lax.rsqrt(ms + 1e-6) * gkv_ref[...]).astype(BF16)
    krb = _rope(rb[:, 384:512], cos32, sin32, 16).astype(BF16)

    r = mm(W_A, W_A + 768)
    p_ref[:, P_AQ:P_AQ + 256] = (r[:, 0:256] * QSCALE).astype(BF16)
    p_ref[:, P_AK:P_AK + 512] = r[:, 256:768].astype(BF16)

    q = _dot(qn, wuq_ref[...]) * B_QSCALE
    for p in range(2):
        base = P_BQ0 + 256 * p
        p_ref[:, base:base + 128] = q[:, 256 * p:256 * p + 128].astype(BF16)
        p_ref[:, base + 128:base + 256] = _rope(q[:, 256 * p + 128:256 * p + 256], cos32, sin32, 16).astype(BF16)
    kv = _dot(kvn, wukv_ref[...])
    for p in range(2):
        base = P_BK0 + 256 * p
        p_ref[:, base:base + 128] = kv[:, 128 * p:128 * (p + 1)].astype(BF16)
        p_ref[:, base + 128:base + 256] = krb
    p_ref[:, P_BV:P_BV + 256] = kv[:, 256:512].astype(BF16)

    r = mm(W_G, W_G + 1024)
    p_ref[:, P_GA:P_GA + 1024] = (r * (1.0 / (1.0 + jnp.exp(-r)))).astype(BF16)

    r = mm(W_C, W_C + 512)
    for j in range(2):
        qj = _rope(r[:, 128 * j:128 * (j + 1)], cos64, sin64, 32)
        p_ref[:, P_CQ + 128 * j:P_CQ + 128 * (j + 1)] = (qj * QSCALE).astype(BF16)
    p_ref[:, P_CK:P_CK + 128] = _rope(r[:, 256:384], cos64, sin64, 32).astype(BF16)
    p_ref[:, P_CV:P_CV + 128] = r[:, 384:512].astype(BF16)

    r = mm(W_M, W_M + 256)
    p_ref[:, P_MQ:P_MQ + 256] = (r * QSCALE).astype(BF16)


def _inproj_kernel(x_ref, pieces_ref, expand_ref, w_ref, wuq_ref, wukv_ref, gq_ref, gkv_ref, p_ref, tab_ref):
    tab_ref[...] = _dot_tn(pieces_ref[...], expand_ref[...])
    _project(x_ref[...].astype(BF16), tab_ref, w_ref, wuq_ref, wukv_ref, gq_ref, gkv_ref, p_ref)


def _inproj(x2d, pieces, expand3, w, wuq, wukv, gq, gkv):
    n = x2d.shape[0]
    const = lambda shape: pl.BlockSpec(shape, lambda i: (0,) * len(shape))
    return pl.pallas_call(
        _inproj_kernel,
        grid=(n // TM,),
        in_specs=[
            pl.BlockSpec((TM, D_MODEL), lambda i: (i, 0)),
            pl.BlockSpec((384, TM), lambda i: (0, i)),
            const((384, 512)),
            const((D_MODEL, W_WIDTH)),
            const((256, 512)),
            const((128, 512)),
            const((1, 256)),
            const((1, 128)),
        ],
        out_specs=[pl.BlockSpec((TM, P_WIDTH), lambda i: (i, 0)),
                   pl.BlockSpec((TM, 512), lambda i: (i, 0))],
        out_shape=[jax.ShapeDtypeStruct((n, P_WIDTH), BF16),
                   jax.ShapeDtypeStruct((n, 512), F32)],
        compiler_params=pltpu.CompilerParams(
            dimension_semantics=("parallel",), vmem_limit_bytes=VMEM_LIMIT),
        name="inproj",
    )(x2d, pieces, expand3, w, wuq, wukv, gq, gkv)


def _memkv_kernel(mem_ref, w_ref, o_ref):
    o_ref[0] = _dot(mem_ref[0].astype(BF16), w_ref[...]).astype(BF16)


def _memkv(mem, w_all):
    b = mem.shape[0]
    n = w_all.shape[1]
    return pl.pallas_call(
        _memkv_kernel,
        grid=(b,),
        in_specs=[pl.BlockSpec((1, MEM_LEN, D_MODEL), lambda i: (i, 0, 0)),
                  pl.BlockSpec((D_MODEL, n), lambda i: (0, 0))],
        out_specs=pl.BlockSpec((1, MEM_LEN, n), lambda i: (i, 0, 0)),
        out_shape=jax.ShapeDtypeStruct((b, MEM_LEN, n), BF16),
        compiler_params=pltpu.CompilerParams(
            dimension_semantics=("parallel",), vmem_limit_bytes=VMEM_LIMIT),
        name="memkv",
    )(mem, w_all)


def _window_attn_kernel(*refs, win, prev, dk, has_table, has_sink):
    refs = list(refs)
    sink_ref = refs.pop(0) if has_sink else None
    q_ref, k_ref, v_ref, g_ref = refs[:4]
    e_ref = refs[4] if has_table else None
    o_ref, sa_sc, sb_sc = refs[-3:]
    n_items = q_ref.shape[1] // TQ
    lanes = N_HEADS * TQ

    lane128 = lax.broadcasted_iota(jnp.int32, (TQ, 128), 1)
    lo, hi = lane128 < HEAD_DIM, lane128 >= HEAD_DIM
    if has_sink:
        col = lax.broadcasted_iota(jnp.int32, (1, lanes), 1)
        order = C_HEAD_ORDER if dk == 128 else tuple(range(N_HEADS))
        sink = jnp.where(col < TQ, sink_ref[order[0]],
                         jnp.where(col < 2 * TQ, sink_ref[order[1]],
                                   jnp.where(col < 3 * TQ, sink_ref[order[2]], sink_ref[order[3]]))) * LOG2E

    def window_start(item):
        if prev is None:
            return 0
        return pl.multiple_of(jnp.maximum(item * TQ - prev, 0), 128)

    def scores_into(item, s_sc):
        q = q_ref[0, pl.ds(pl.multiple_of(item * TQ, TQ), TQ), :].astype(F32)
        if dk == 256:
            zero = jnp.zeros((TQ, 128), F32)
            blocks = [jnp.concatenate([jnp.where(lo, q[:, 0:128], 0.0), zero], axis=1),
                      jnp.concatenate([jnp.where(hi, q[:, 0:128], 0.0), zero], axis=1),
                      jnp.concatenate([zero, jnp.where(lo, q[:, 128:256], 0.0)], axis=1),
                      jnp.concatenate([zero, jnp.where(hi, q[:, 128:256], 0.0)], axis=1)]
        else:
            blocks = [jnp.where(lo, q[:, 0:128], 0.0), jnp.where(hi, q[:, 0:128], 0.0),
                      jnp.where(lo, q[:, 128:256], 0.0), jnp.where(hi, q[:, 128:256], 0.0)]
        qs = jnp.concatenate(blocks, axis=0).astype(BF16)
        s = _dot_nt(k_ref[0, pl.ds(window_start(item), win), :], qs)
        if has_table:
            mb0 = jnp.maximum(prev // 128 - item, 0)
            s = s + jnp.concatenate([e_ref[mb0 + jb] for jb in range(win // 128)], axis=0)
        s_sc[...] = s

    def consume(item, s_sc):
        start = window_start(item)
        v = v_ref[0, pl.ds(start, win), :]
        parts = []
        for pr in range(2):
            cols = slice(256 * pr, 256 * (pr + 1))
            m = jnp.max(s_sc[:, cols], axis=0, keepdims=True)
            if has_sink:
                m = jnp.maximum(m, sink[:, cols])
            p = jnp.exp2(s_sc[:, cols] - m)
            l = jnp.sum(p, axis=0, keepdims=True)
            if has_sink:
                l = l + jnp.exp2(sink[:, cols] - m)
            inv = 1.0 / l
            vp = v[:, 128 * pr:128 * (pr + 1)] if dk == 256 else v
            ot = _dot_tn(vp, p.astype(BF16))
            for e in range(2):
                parts.append(ot[64 * e:64 * (e + 1), 128 * e:128 * (e + 1)] * inv[:, TQ * e:TQ * (e + 1)])
        o = jnp.concatenate(parts, axis=0).T
        rows = pl.ds(pl.multiple_of(item * TQ, TQ), TQ)
        o_ref[0, rows, :] = (o * g_ref[0, rows, :].astype(F32)).astype(BF16)

    bufs = (sa_sc, sb_sc)
    scores_into(0, bufs[0])

    def body(i, carry):
        for j in range(WIN_UNROLL):
            item = WIN_UNROLL * i + j
            scores_into(jnp.minimum(item + 1, n_items - 1), bufs[(j + 1) % 2])
            consume(item, bufs[j % 2])
        return carry

    lax.fori_loop(0, n_items // WIN_UNROLL, body, 0)


def _window_attn(q_src, q_col, k_src, k_col, v_src, v_col, g_col, *, win, prev, dk, table=None, sinks=None):
    b, s, _ = q_src.shape
    skv = k_src.shape[1]
    assert s % (WIN_UNROLL * TQ) == 0 and skv >= win
    kern = functools.partial(_window_attn_kernel, win=win, prev=prev, dk=dk,
                             has_table=table is not None, has_sink=sinks is not None)
    in_specs, args = [], []
    if sinks is not None:
        in_specs.append(pl.BlockSpec(memory_space=pltpu.SMEM)); args.append(sinks)
    in_specs += [pl.BlockSpec((1, s, 256), lambda i: (i, 0, q_col // 256)),
                 pl.BlockSpec((1, skv, dk), lambda i: (i, 0, k_col // dk)),
                 pl.BlockSpec((1, skv, dk), lambda i: (i, 0, v_col // dk)),
                 pl.BlockSpec((1, s, 256), lambda i: (i, 0, g_col // 256))]
    args += [q_src, k_src, v_src, q_src]
    if table is not None:
        in_specs.append(pl.BlockSpec(table.shape, lambda i: (0, 0, 0))); args.append(table)
    return pl.pallas_call(
        kern,
        grid=(b,),
        in_specs=in_specs,
        out_specs=pl.BlockSpec((1, s, 256), lambda i: (i, 0, 0)),
        out_shape=jax.ShapeDtypeStruct((b, s, GROUP), BF16),
        scratch_shapes=[pltpu.VMEM((win, N_HEADS * TQ), F32), pltpu.VMEM((win, N_HEADS * TQ), F32)],
        compiler_params=pltpu.CompilerParams(
            dimension_semantics=("parallel",), vmem_limit_bytes=VMEM_LIMIT),
        name="attn_win%d" % win,
    )(*args)


def _attn_b_kernel(q0_ref, q1_ref, k0_ref, k1_ref, v_ref, g_ref, o_ref,
                   qs_sc, vt_sc, sa_sc, sb_sc, m_sc, l_sc, acc_sc):
    t = pl.program_id(1)
    n_tiles = B_TQ // TQ
    lane = lax.broadcasted_iota(jnp.int32, (TQ, 128), 1)
    q_refs = (q0_ref, q1_ref)
    k_refs = (k0_ref, k1_ref)

    def stack_queries(c):
        for pr in range(2):
            q = q_refs[pr][0, TQ * c:TQ * (c + 1), :].astype(F32)
            nope, rope = q[:, 0:128], q[:, 128:256]
            head_a = jnp.concatenate([jnp.where(lane < 64, nope, 0.0), jnp.where(lane < 32, rope, 0.0)], axis=1)
            head_b = jnp.concatenate([jnp.where(lane >= 64, nope, 0.0), jnp.where(lane >= 32, rope, 0.0)], axis=1)
            qs_sc[2 * c + pr] = jnp.concatenate([head_a, head_b], axis=0).astype(BF16)

    lane2 = lax.broadcasted_iota(jnp.int32, (1, 2 * TQ), 1)
    hide_first_chunk = jnp.where((lane2 & (TQ - 1)) < CHUNK, NEG_INF, 0.0)

    def scores_into(kb, c, s_sc, nk=B_TK):
        start = pl.multiple_of(kb * B_TK, B_TK)
        for pr in range(2):
            s_sc[pr, 0:nk, :] = _dot_nt(k_refs[pr][0, pl.ds(start, nk), :], qs_sc[2 * c + pr])

    def transpose_values(kb, slot=0):
        vt_sc[slot] = v_ref[0, pl.ds(pl.multiple_of(kb * B_TK, B_TK), B_TK), :].T

    def consume(kb, c, s_sc, nk=B_TK, diagonal=False, slot=0):
        for pr in range(2):
            u = 2 * c + pr
            s = s_sc[pr, 0:nk, :]
            if diagonal:
                s = jnp.concatenate([s[:nk - CHUNK], s[nk - CHUNK:] + hide_first_chunk], axis=0)
            m_prev = m_sc[u]
            m_new = jnp.maximum(m_prev, jnp.max(s, axis=0, keepdims=True))
            alpha = jnp.exp2(m_prev - m_new)
            p = jnp.exp2(s - m_new)
            l_sc[u] = alpha * l_sc[u] + jnp.sum(p, axis=0, keepdims=True)
            m_sc[u] = m_new
            acc_sc[u] = alpha * acc_sc[u] + _dot(vt_sc[slot, 128 * pr:128 * (pr + 1), 0:nk], p.astype(BF16))

    bufs = (sa_sc, sb_sc)
    for c in range(n_tiles):
        stack_queries(c)
    scores_into(0, 0, bufs[0])
    m_sc[...] = jnp.full(m_sc.shape, NEG_INF, F32)
    l_sc[...] = jnp.zeros(l_sc.shape, F32)
    acc_sc[...] = jnp.zeros(acc_sc.shape, F32)

    def past_block(kb, slot):
        for c in range(n_tiles):
            if c + 1 < n_tiles:
                scores_into(kb, c + 1, bufs[(c + 1) % 2])
            else:
                scores_into(kb + 1, 0, bufs[0])
            consume(kb, c, bufs[c % 2], slot=slot)

    def two_blocks(i, carry):
        transpose_values(2 * i, 0)
        transpose_values(2 * i + 1, 1)
        past_block(2 * i, 0)
        past_block(2 * i + 1, 1)
        return carry

    def one_block(kb, carry):
        transpose_values(kb, 0)
        past_block(kb, 0)
        return carry

    lax.fori_loop(0, t >> 1, two_blocks, 0)
    lax.fori_loop(t & ~1, t, one_block, 0)

    transpose_values(t)
    for c in range(n_tiles):
        if c + 1 < n_tiles:
            scores_into(t, c + 1, bufs[(c + 1) % 2], nk=TQ * (c + 2))
        consume(t, c, bufs[c % 2], nk=TQ * (c + 1), diagonal=True)
        parts = []
        for pr in range(2):
            u = 2 * c + pr
            inv = 1.0 / l_sc[u]
            for e in range(2):
                parts.append(acc_sc[u, 64 * e:64 * (e + 1), 128 * e:128 * (e + 1)] * inv[:, 128 * e:128 * (e + 1)])
        o = jnp.concatenate(parts, axis=0).T
        o_ref[0, TQ * c:TQ * (c + 1), :] = (o * g_ref[0, TQ * c:TQ * (c + 1), :].astype(F32)).astype(BF16)


def _attn_b(p3):
    b, s, _ = p3.shape
    n_units = 2 * (B_TQ // TQ)
    return pl.pallas_call(
        _attn_b_kernel,
        grid=(b, s // B_TQ),
        in_specs=[
            pl.BlockSpec((1, B_TQ, 256), lambda i, t: (i, t, P_BQ0 // 256)),
            pl.BlockSpec((1, B_TQ, 256), lambda i, t: (i, t, P_BQ1 // 256)),
            pl.BlockSpec((1, s, 256), lambda i, t: (i, 0, P_BK0 // 256)),
            pl.BlockSpec((1, s, 256), lambda i, t: (i, 0, P_BK1 // 256)),
            pl.BlockSpec((1, s, 256), lambda i, t: (i, 0, P_BV // 256)),
            pl.BlockSpec((1, B_TQ, 256), lambda i, t: (i, t, P_GB // 256)),
        ],
        out_specs=pl.BlockSpec((1, B_TQ, 256), lambda i, t: (i, t, 0)),
        out_shape=jax.ShapeDtypeStruct((b, s, GROUP), BF16),
        scratch_shapes=[pltpu.VMEM((n_units, 2 * TQ, 256), BF16),
                        pltpu.VMEM((2, GROUP, B_TK), BF16),
                        pltpu.VMEM((2, B_TK, 2 * TQ), F32),
                        pltpu.VMEM((2, B_TK, 2 * TQ), F32),
                        pltpu.VMEM((n_units, 1, 2 * TQ), F32),
                        pltpu.VMEM((n_units, 1, 2 * TQ), F32),
                        pltpu.VMEM((n_units, 128, 2 * TQ), F32)],
        compiler_params=pltpu.CompilerParams(
            dimension_semantics=("parallel", "arbitrary"), vmem_limit_bytes=VMEM_LIMIT),
        name="attn_b",
    )(p3, p3, p3, p3, p3, p3)


def _residual_norm(ya_ref, yb_ref, yc_ref, ym_ref, x_ref, w_ref, g_ref, b_ref, rows=slice(None)):
    y = (_dot(ya_ref[rows, :], w_ref[0:256, :]) + _dot(yb_ref[rows, :], w_ref[256:512, :])
         + _dot(yc_ref[rows, :], w_ref[512:768, :]) + _dot(ym_ref[rows, :], w_ref[768:1024, :]))
    z = ALPHA * x_ref[rows, :] + y
    mu = jnp.mean(z, axis=-1, keepdims=True)
    zc = z - mu
    var = jnp.mean(zc * zc, axis=-1, keepdims=True)
    return zc * lax.rsqrt(var + 1e-5) * g_ref[...] + b_ref[...]


def _outproj_kernel(ya_ref, yb_ref, yc_ref, ym_ref, x_ref, w_ref, g_ref, b_ref, o_ref):
    for h in range(TM_OUT // 256):
        rows = pl.ds(h * 256, 256)
        o_ref[rows, :] = _residual_norm(ya_ref, yb_ref, yc_ref, ym_ref, x_ref, w_ref, g_ref, b_ref, rows)


def _out_in_proj_kernel(ya_ref, yb_ref, yc_ref, ym_ref, x_ref, wo_ref, g_ref, b_ref,
                        tab_ref, w_ref, wuq_ref, wukv_ref, gq_ref, gkv_ref, o_ref, p_ref):
    halves = [pl.ds(h * (TM // 2), TM // 2) for h in range(2)]
    xb = []
    for rows in halves:
        xn = _residual_norm(ya_ref, yb_ref, yc_ref, ym_ref, x_ref, wo_ref, g_ref, b_ref, rows)
        o_ref[rows, :] = xn
        xb.append(xn.astype(BF16))
    for rows, x in zip(halves, xb):
        _project(x, tab_ref, w_ref, wuq_ref, wukv_ref, gq_ref, gkv_ref, p_ref, rows)


def _outproj(ya, yb, yc, ym, x2d, w, g, bias):
    n = x2d.shape[0]
    ytile = pl.BlockSpec((TM_OUT, GROUP), lambda i: (i, 0))
    const = lambda shape: pl.BlockSpec(shape, lambda i: (0,) * len(shape))
    return pl.pallas_call(
        _outproj_kernel,
        grid=(n // TM_OUT,),
        in_specs=[ytile, ytile, ytile, ytile,
                  pl.BlockSpec((TM_OUT, D_MODEL), lambda i: (i, 0)),
                  const((D_MODEL, D_MODEL)), const((1, D_MODEL)), const((1, D_MODEL))],
        out_specs=pl.BlockSpec((TM_OUT, D_MODEL), lambda i: (i, 0)),
        out_shape=jax.ShapeDtypeStruct((n, D_MODEL), F32),
        compiler_params=pltpu.CompilerParams(
            dimension_semantics=("parallel",), vmem_limit_bytes=VMEM_LIMIT),
        name="outproj",
    )(ya, yb, yc, ym, x2d, w, g, bias)


def _out_in_proj(ya, yb, yc, ym, x2d, wo, g, bias, tab, w, wuq, wukv, gq, gkv):
    n = x2d.shape[0]
    ytile = pl.BlockSpec((TM, GROUP), lambda i: (i, 0))
    const = lambda shape: pl.BlockSpec(shape, lambda i: (0,) * len(shape))
    return pl.pallas_call(
        _out_in_proj_kernel,
        grid=(n // TM,),
        in_specs=[ytile, ytile, ytile, ytile,
                  pl.BlockSpec((TM, D_MODEL), lambda i: (i, 0)),
                  const((D_MODEL, D_MODEL)), const((1, D_MODEL)), const((1, D_MODEL)),
                  pl.BlockSpec((TM, 512), lambda i: (i, 0)),
                  const((D_MODEL, W_WIDTH)), const((256, 512)), const((128, 512)),
                  const((1, 256)), const((1, 128))],
        out_specs=[pl.BlockSpec((TM, D_MODEL), lambda i: (i, 0)),
                   pl.BlockSpec((TM, P_WIDTH), lambda i: (i, 0))],
        out_shape=[jax.ShapeDtypeStruct((n, D_MODEL), F32),
                   jax.ShapeDtypeStruct((n, P_WIDTH), BF16)],
        compiler_params=pltpu.CompilerParams(
            dimension_semantics=("parallel",), vmem_limit_bytes=VMEM_LIMIT),
        name="out_in_proj",
    )(ya, yb, yc, ym, x2d, wo, g, bias, tab, w, wuq, wukv, gq, gkv)


def kernel(x, mem, positions, w_in, rel_bias, mla_q_norm, w_uq, mla_kv_norm, w_ukv,
           swa_sinks, w_mem_kv, w_out, ln_gain, ln_bias):
    b, s, d = x.shape
    depth = w_in.shape[0]
    assert d == D_MODEL and depth == DEPTH and s % B_TK == 0 and s >= A_WIN and (b * s) % TM_OUT == 0 and TM_OUT % TM == 0

    cols, cperm = _inproj_cols()
    w_in_p = _take_cols(w_in, cols).astype(BF16)
    wuq_p = jnp.pad(_take_cols(w_uq, _uq_cols()), ((0, 0), (0, 256 - MLA_Q_RANK), (0, 0))).astype(BF16)
    wukv_p = _take_cols(w_ukv, _ukv_cols()).astype(BF16)
    gq = jnp.pad(mla_q_norm, ((0, 0), (0, 256 - MLA_Q_RANK)))[:, None, :]
    gkv = mla_kv_norm[:, None, :]
    rows = np.concatenate([np.arange(512), 512 + cperm, np.arange(768, 1024)])
    w_out_p = _take_cols(w_out, rows, axis=1).astype(BF16)
    w_mem_all = jnp.transpose(w_mem_kv, (1, 0, 2)).reshape(D_MODEL, depth * 512).astype(BF16)
    e_a = _bias_table_a(rel_bias)
    e_c = jnp.asarray(_mask_table_c())
    pieces, expand3 = _rope_pieces(positions)

    memkv = _memkv(mem, w_mem_all)
    h = x.reshape(b * s, d)
    p2, tab = _inproj(h, pieces, expand3, w_in_p[0], wuq_p[0], wukv_p[0], gq[0], gkv[0])
    for l in range(depth):
        p3 = p2.reshape(b, s, P_WIDTH)
        ya = _window_attn(p3, P_AQ, p3, P_AK, p3, P_AV, P_GA, win=A_WIN, prev=A_PREV * CHUNK, dk=256, table=e_a[l])
        yb = _attn_b(p3)
        yc = _window_attn(p3, P_CQ, p3, P_CK, p3, P_CV, P_GC, win=C_WIN, prev=SWA_PREV * CHUNK, dk=128,
                          table=e_c, sinks=swa_sinks[l])
        ym = _window_attn(p3, P_MQ, memkv, 512 * l, memkv, 512 * l + 256, P_GM, win=MEM_LEN, prev=None, dk=256)
        ys = [y.reshape(b * s, GROUP) for y in (ya, yb, yc, ym)]
        ln = (ln_gain[l][None, :], ln_bias[l][None, :])
        if l + 1 < depth:
            h, p2 = _out_in_proj(*ys, h, w_out_p[l], *ln, tab, w_in_p[l + 1], wuq_p[l + 1], wukv_p[l + 1],
                                 gq[l + 1], gkv[l + 1])
        else:
            h = _outproj(*ys, h, w_out_p[l], *ln)
    return h.reshape(b, s, d)
```

```python
import functools

import numpy as np
import jax
import jax.numpy as jnp
from jax import lax
from jax.experimental import pallas as pl
from jax.experimental.pallas import tpu as pltpu

F32 = jnp.float32
BF16 = jnp.bfloat16

D_MODEL = 1024
DEPTH = 4
CHUNK = 64
HEAD_DIM = 64
GROUP = 256
N_HEADS = 4
ROPE_THETA = 10000.0
NEG_INF = -1e30
A_PREV = 8
REL_CLIP = 128
MLA_NOPE = 64
MLA_ROPE = 32
MLA_Q_RANK = 192
MLA_KV_RANK = 128
SWA_PREV = 2
MEM_LEN = 256
ALPHA = (2.0 * DEPTH) ** 0.25

TQ = 128
A_WIN = TQ + A_PREV * CHUNK
C_WIN = TQ + SWA_PREV * CHUNK
WIN_UNROLL = 16
B_TK = 512
B_TQ = 512
TM = 512
TM_OUT = 1024
VMEM_LIMIT = 56 * 1024 * 1024
LOG2E = 1.4426950408889634
QSCALE = HEAD_DIM ** -0.5 * LOG2E
B_QSCALE = (MLA_NOPE + MLA_ROPE) ** -0.5 * LOG2E

P_AQ, P_AK, P_AV = 0, 256, 512
P_BQ0, P_BQ1, P_BK0, P_BK1, P_BV = 768, 1024, 1280, 1536, 1792
P_CQ, P_CK, P_CV = 2048, 2304, 2432
P_MQ = 2560
P_GA, P_GB, P_GC, P_GM = 2816, 3072, 3328, 3584
P_WIDTH = 3840

W_A, W_C, W_M, W_G, W_B, W_WIDTH = 0, 768, 1280, 1536, 2560, 3072

C_HEAD_ORDER = (0, 2, 1, 3)


def _inproj_cols():
    r = np.arange
    aq, ak, av, ag = 0, 256, 512, 768
    bcq, bckv, bkr, bg = 1024, 1216, 1344, 1376
    cq, ck, cv, cg = 1632, 1888, 2016, 2144
    mq, mg = 2400, 2656
    cperm = np.concatenate([r(64) + 64 * h for h in C_HEAD_ORDER])
    pad = lambda n: np.full(n, -1)
    cols = np.concatenate([
        aq + r(256), ak + r(256), av + r(256),
        cq + cperm, ck + r(128), cv + r(128),
        mq + r(256),
        ag + r(256), bg + r(256), cg + cperm, mg + r(256),
        bcq + r(192), pad(64), bckv + r(128), bkr + r(32), bkr + r(32), pad(64),
    ])
    assert cols.shape[0] == W_WIDTH
    return cols, cperm


def _take_cols(w, cols, axis=-1):
    axis = axis % w.ndim
    pieces, i = [], 0
    while i < len(cols):
        j = i + 1
        if cols[i] < 0:
            while j < len(cols) and cols[j] < 0:
                j += 1
            shape = w.shape[:axis] + (j - i,) + w.shape[axis + 1:]
            pieces.append(jnp.zeros(shape, w.dtype))
        else:
            while j < len(cols) and cols[j] == cols[j - 1] + 1:
                j += 1
            pieces.append(lax.slice_in_dim(w, int(cols[i]), int(cols[i]) + (j - i), axis=axis))
        i = j
    return jnp.concatenate(pieces, axis=axis)


def _uq_cols():
    r = np.arange
    per = MLA_NOPE + MLA_ROPE
    out = []
    for p in range(2):
        h0, h1 = 2 * p, 2 * p + 1
        out += [per * h0 + r(64), per * h1 + r(64),
                per * h0 + 64 + r(32), per * h1 + 64 + r(32), np.full(64, -1)]
    return np.concatenate(out)


def _ukv_cols():
    r = np.arange
    return np.concatenate([128 * h + r(64) for h in range(4)] + [128 * h + 64 + r(64) for h in range(4)])


def _rope_pieces(positions):
    pos = positions.astype(F32).reshape(1, -1)
    narrow, col0, expand = [], 0, np.zeros((128, 512), np.float32)
    for t, d in enumerate((HEAD_DIM, MLA_ROPE)):
        half = d // 2
        inv = ROPE_THETA ** (-jnp.arange(0, d, 2, dtype=F32) / d)
        ang = inv[:, None] * pos
        narrow += [jnp.cos(ang), jnp.sin(ang)]
        lane = np.arange(128)
        k = (lane % d) % half
        expand[col0 + k, 256 * t + lane] = 1.0
        expand[col0 + half + k, 256 * t + 128 + lane] = np.where(lane % d < half, -1.0, 1.0)
        col0 += d
    x = jnp.pad(jnp.concatenate(narrow, axis=0), ((0, 128 - col0), (0, 0)))
    x = lax.optimization_barrier(x)
    hi = x.astype(BF16)
    rest = x - hi.astype(F32)
    mid = rest.astype(BF16)
    lo = (rest - mid.astype(F32)).astype(BF16)
    return jnp.concatenate([hi, mid, lo], axis=0), jnp.asarray(np.concatenate([expand] * 3, axis=0), BF16)


def _bias_table_a(rel_bias):
    width, period = 9 * 128, 9 * 128 + TQ + 1
    k = np.arange(period)
    d = np.where(k < width, A_PREV * CHUNK - k, A_PREV * CHUNK + period - k)
    idx = np.clip(d, -REL_CLIP, REL_CLIP) + REL_CLIP
    n_hi = A_PREV * CHUNK - REL_CLIP + 1
    n_lo = width - n_hi - (2 * REL_CLIP - 1)
    expect = np.concatenate([np.full(n_hi, 2 * REL_CLIP), np.arange(2 * REL_CLIP - 1, 0, -1),
                             np.zeros(n_lo, np.int64), np.full(period - width, 2 * REL_CLIP)])
    assert np.array_equal(idx, expect)
    rep = lambda col, n: jnp.broadcast_to(rel_bias[:, :, col:col + 1], rel_bias.shape[:2] + (n,))
    gp = jnp.concatenate([rep(2 * REL_CLIP, n_hi), jnp.flip(rel_bias[:, :, 1:2 * REL_CLIP], axis=-1),
                          rep(0, n_lo), rep(2 * REL_CLIP, period - width)], axis=-1) * LOG2E
    flat = jnp.tile(gp, (1, 1, TQ))[:, :, :TQ * (period - 1)]
    skew = flat.reshape(gp.shape[0], N_HEADS, TQ, period - 1)[..., :width]
    i = np.arange(TQ)[:, None]
    m = np.arange(width)[None, :]
    dchunk = i // CHUNK + A_PREV - m // CHUNK
    valid = (dchunk >= 0) & (dchunk <= A_PREV)
    t = jnp.where(jnp.asarray(valid)[None, None], skew, NEG_INF)
    return jnp.transpose(t, (0, 3, 1, 2)).reshape(gp.shape[0], 9, 128, N_HEADS * TQ)


def _mask_table_c():
    m = np.arange(3 * 128)[:, None]
    i = np.arange(TQ)[None, :]
    dchunk = i // CHUNK + SWA_PREV - m // CHUNK
    valid = (dchunk >= 0) & (dchunk <= SWA_PREV)
    t = np.where(valid, 0.0, NEG_INF).astype(np.float32)
    return np.tile(t, (1, N_HEADS)).reshape(3, 128, N_HEADS * TQ)


def _dot(a, b):
    return jnp.dot(a, b, preferred_element_type=F32)


def _dot_nt(a, b):
    return lax.dot_general(a, b, (((1,), (1,)), ((), ())), preferred_element_type=F32)


def _dot_tn(a, b):
    return lax.dot_general(a, b, (((0,), (0,)), ((), ())), preferred_element_type=F32)


def _rope(x, cos, sin_signed, half):
    lane = lax.broadcasted_iota(jnp.int32, x.shape, 1)
    first = (lane & (2 * half - 1)) < half
    swapped = jnp.where(first, pltpu.roll(x, 128 - half, 1), pltpu.roll(x, half, 1))
    return x * cos + swapped * sin_signed


def _project(xb, tab_ref, w_ref, wuq_ref, wukv_ref, gq_ref, gkv_ref, p_ref, rows=slice(None)):
    tab_ref, p_ref = tab_ref.at[rows], p_ref.at[rows]
    cos64, sin64 = tab_ref[:, 0:128], tab_ref[:, 128:256]
    cos32, sin32 = tab_ref[:, 256:384], tab_ref[:, 384:512]

    def mm(lo, hi):
        return _dot(xb, w_ref[:, lo:hi])

    rb = mm(W_B, W_B + 512)
    cq = rb[:, 0:256]
    ms = jnp.sum(cq * cq, axis=-1, keepdims=True) * (1.0 / MLA_Q_RANK)
    qn = (cq * lax.rsqrt(ms + 1e-6) * gq_ref[...]).astype(BF16)
    ckv = rb[:, 256:384]
    ms = jnp.mean(ckv * ckv, axis=-1, keepdims=True)
    kvn = (ckv * lax.rsqrt(ms + 1e-6) * gkv_ref[...]).astype(BF16)
    krb = _rope(rb[:, 384:512], cos32, sin32, 16).astype(BF16)

    r = mm(W_A, W_A + 768)
    p_ref[:, P_AQ:P_AQ + 256] = (r[:, 0:256] * QSCALE).astype(BF16)
    p_ref[:, P_AK:P_AK + 512] = r[:, 256:768].astype(BF16)

    q = _dot(qn, wuq_ref[...]) * B_QSCALE
    for p in range(2):
        base = P_BQ0 + 256 * p
        p_ref[:, base:base + 128] = q[:, 256 * p:256 * p + 128].astype(BF16)
        p_ref[:, base + 128:base + 256] = _rope(q[:, 256 * p + 128:256 * p + 256], cos32, sin32, 16).astype(BF16)
    kv = _dot(kvn, wukv_ref[...])
    for p in range(2):
        base = P_BK0 + 256 * p
        p_ref[:, base:base + 128] = kv[:, 128 * p:128 * (p + 1)].astype(BF16)
        p_ref[:, base + 128:base + 256] = krb
    p_ref[:, P_BV:P_BV + 256] = kv[:, 256:512].astype(BF16)

    r = mm(W_G, W_G + 1024)
    p_ref[:, P_GA:P_GA + 1024] = (r * (1.0 / (1.0 + jnp.exp(-r)))).astype(BF16)

    r = mm(W_C, W_C + 512)
    for j in range(2):
        qj = _rope(r[:, 128 * j:128 * (j + 1)], cos64, sin64, 32)
        p_ref[:, P_CQ + 128 * j:P_CQ + 128 * (j + 1)] = (qj * QSCALE).astype(BF16)
    p_ref[:, P_CK:P_CK + 128] = _rope(r[:, 256:384], cos64, sin64, 32).astype(BF16)
    p_ref[:, P_CV:P_CV + 128] = r[:, 384:512].astype(BF16)

    r = mm(W_M, W_M + 256)
    p_ref[:, P_MQ:P_MQ + 256] = (r * QSCALE).astype(BF16)


def _inproj_kernel(x_ref, pieces_ref, expand_ref, w_ref, wuq_ref, wukv_ref, gq_ref, gkv_ref, p_ref, tab_ref):
    tab_ref[...] = _dot_tn(pieces_ref[...], expand_ref[...])
    _project(x_ref[...].astype(BF16), tab_ref, w_ref, wuq_ref, wukv_ref, gq_ref, gkv_ref, p_ref)


def _inproj(x2d, pieces, expand3, w, wuq, wukv, gq, gkv):
    n = x2d.shape[0]
    const = lambda shape: pl.BlockSpec(shape, lambda i: (0,) * len(shape))
    return pl.pallas_call(
        _inproj_kernel,
        grid=(n // TM,),
        in_specs=[
            pl.BlockSpec((TM, D_MODEL), lambda i: (i, 0)),
            pl.BlockSpec((384, TM), lambda i: (0, i)),
            const((384, 512)),
            const((D_MODEL, W_WIDTH)),
            const((256, 512)),
            const((128, 512)),
            const((1, 256)),
            const((1, 128)),
        ],
        out_specs=[pl.BlockSpec((TM, P_WIDTH), lambda i: (i, 0)),
                   pl.BlockSpec((TM, 512), lambda i: (i, 0))],
        out_shape=[jax.ShapeDtypeStruct((n, P_WIDTH), BF16),
                   jax.ShapeDtypeStruct((n, 512), F32)],
        compiler_params=pltpu.CompilerParams(
            dimension_semantics=("parallel",), vmem_limit_bytes=VMEM_LIMIT),
        name="inproj",
    )(x2d, pieces, expand3, w, wuq, wukv, gq, gkv)


def _memkv_kernel(mem_ref, w_ref, o_ref):
    o_ref[0] = _dot(mem_ref[0].astype(BF16), w_ref[...]).astype(BF16)


def _memkv(mem, w_all):
    b = mem.shape[0]
    n = w_all.shape[1]
    return pl.pallas_call(
        _memkv_kernel,
        grid=(b,),
        in_specs=[pl.BlockSpec((1, MEM_LEN, D_MODEL), lambda i: (i, 0, 0)),
                  pl.BlockSpec((D_MODEL, n), lambda i: (0, 0))],
        out_specs=pl.BlockSpec((1, MEM_LEN, n), lambda i: (i, 0, 0)),
        out_shape=jax.ShapeDtypeStruct((b, MEM_LEN, n), BF16),
        compiler_params=pltpu.CompilerParams(
            dimension_semantics=("parallel",), vmem_limit_bytes=VMEM_LIMIT),
        name="memkv",
    )(mem, w_all)


def _window_attn_kernel(*refs, win, prev, dk, has_table, has_sink):
    refs = list(refs)
    sink_ref = refs.pop(0) if has_sink else None
    q_ref, k_ref, v_ref, g_ref = refs[:4]
    e_ref = refs[4] if has_table else None
    o_ref, sa_sc, sb_sc = refs[-3:]
    n_items = q_ref.shape[1] // TQ
    lanes = N_HEADS * TQ

    lane128 = lax.broadcasted_iota(jnp.int32, (TQ, 128), 1)
    lo, hi = lane128 < HEAD_DIM, lane128 >= HEAD_DIM
    if has_sink:
        col = lax.broadcasted_iota(jnp.int32, (1, lanes), 1)
        order = C_HEAD_ORDER if dk == 128 else tuple(range(N_HEADS))
        sink = jnp.where(col < TQ, sink_ref[order[0]],
                         jnp.where(col < 2 * TQ, sink_ref[order[1]],
                                   jnp.where(col < 3 * TQ, sink_ref[order[2]], sink_ref[order[3]]))) * LOG2E

    def window_start(item):
        if prev is None:
            return 0
        return pl.multiple_of(jnp.maximum(item * TQ - prev, 0), 128)

    def scores_into(item, s_sc):
        q = q_ref[0, pl.ds(pl.multiple_of(item * TQ, TQ), TQ), :].astype(F32)
        if dk == 256:
            zero = jnp.zeros((TQ, 128), F32)
            blocks = [jnp.concatenate([jnp.where(lo, q[:, 0:128], 0.0), zero], axis=1),
                      jnp.concatenate([jnp.where(hi, q[:, 0:128], 0.0), zero], axis=1),
                      jnp.concatenate([zero, jnp.where(lo, q[:, 128:256], 0.0)], axis=1),
                      jnp.concatenate([zero, jnp.where(hi, q[:, 128:256], 0.0)], axis=1)]
        else:
            blocks = [jnp.where(lo, q[:, 0:128], 0.0), jnp.where(hi, q[:, 0:128], 0.0),
                      jnp.where(lo, q[:, 128:256], 0.0), jnp.where(hi, q[:, 128:256], 0.0)]
        qs = jnp.concatenate(blocks, axis=0).astype(BF16)
        s = _dot_nt(k_ref[0, pl.ds(window_start(item), win), :], qs)
        if has_table:
            mb0 = jnp.maximum(prev // 128 - item, 0)
            s = s + jnp.concatenate([e_ref[mb0 + jb] for jb in range(win // 128)], axis=0)
        s_sc[...] = s

    def consume(item, s_sc):
        start = window_start(item)
        v = v_ref[0, pl.ds(start, win), :]
        parts = []
        for pr in range(2):
            cols = slice(256 * pr, 256 * (pr + 1))
            m = jnp.max(s_sc[:, cols], axis=0, keepdims=True)
            if has_sink:
                m = jnp.maximum(m, sink[:, cols])
            p = jnp.exp2(s_sc[:, cols] - m)
            l = jnp.sum(p, axis=0, keepdims=True)
            if has_sink:
                l = l + jnp.exp2(sink[:, cols] - m)
            inv = 1.0 / l
            vp = v[:, 128 * pr:128 * (pr + 1)] if dk == 256 else v
            ot = _dot_tn(vp, p.astype(BF16))
            for e in range(2):
                parts.append(ot[64 * e:64 * (e + 1), 128 * e:128 * (e + 1)] * inv[:, TQ * e:TQ * (e + 1)])
        o = jnp.concatenate(parts, axis=0).T
        rows = pl.ds(pl.multiple_of(item * TQ, TQ), TQ)
        o_ref[0, rows, :] = (o * g_ref[0, rows, :].astype(F32)).astype(BF16)

    bufs = (sa_sc, sb_sc)
    scores_into(0, bufs[0])

    def body(i, carry):
        for j in range(WIN_UNROLL):
            item = WIN_UNROLL * i + j
            scores_into(jnp.minimum(item + 1, n_items - 1), bufs[(j + 1) % 2])
            consume(item, bufs[j % 2])
        return carry

    lax.fori_loop(0, n_items // WIN_UNROLL, body, 0)


def _window_attn(q_src, q_col, k_src, k_col, v_src, v_col, g_col, *, win, prev, dk, table=None, sinks=None):
    b, s, _ = q_src.shape
    skv = k_src.shape[1]
    assert s % (WIN_UNROLL * TQ) == 0 and skv >= win
    kern = functools.partial(_window_attn_kernel, win=win, prev=prev, dk=dk,
                             has_table=table is not None, has_sink=sinks is not None)
    in_specs, args = [], []
    if sinks is not None:
        in_specs.append(pl.BlockSpec(memory_space=pltpu.SMEM)); args.append(sinks)
    in_specs += [pl.BlockSpec((1, s, 256), lambda i: (i, 0, q_col // 256)),
                 pl.BlockSpec((1, skv, dk), lambda i: (i, 0, k_col // dk)),
                 pl.BlockSpec((1, skv, dk), lambda i: (i, 0, v_col // dk)),
                 pl.BlockSpec((1, s, 256), lambda i: (i, 0, g_col // 256))]
    args += [q_src, k_src, v_src, q_src]
    if table is not None:
        in_specs.append(pl.BlockSpec(table.shape, lambda i: (0, 0, 0))); args.append(table)
    return pl.pallas_call(
        kern,
        grid=(b,),
        in_specs=in_specs,
        out_specs=pl.BlockSpec((1, s, 256), lambda i: (i, 0, 0)),
        out_shape=jax.ShapeDtypeStruct((b, s, GROUP), BF16),
        scratch_shapes=[pltpu.VMEM((win, N_HEADS * TQ), F32), pltpu.VMEM((win, N_HEADS * TQ), F32)],
        compiler_params=pltpu.CompilerParams(
            dimension_semantics=("parallel",), vmem_limit_bytes=VMEM_LIMIT),
        name="attn_win%d" % win,
    )(*args)


def _attn_b_kernel(q0_ref, q1_ref, k0_ref, k1_ref, v_ref, g_ref, o_ref,
                   qs_sc, vt_sc, sa_sc, sb_sc, m_sc, l_sc, acc_sc):
    t = pl.program_id(1)
    n_tiles = B_TQ // TQ
    lane = lax.broadcasted_iota(jnp.int32, (TQ, 128), 1)
    q_refs = (q0_ref, q1_ref)
    k_refs = (k0_ref, k1_ref)

    def stack_queries(c):
        for pr in range(2):
            q = q_refs[pr][0, TQ * c:TQ * (c + 1), :].astype(F32)
            nope, rope = q[:, 0:128], q[:, 128:256]
            head_a = jnp.concatenate([jnp.where(lane < 64, nope, 0.0), jnp.where(lane < 32, rope, 0.0)], axis=1)
            head_b = jnp.concatenate([jnp.where(lane >= 64, nope, 0.0), jnp.where(lane >= 32, rope, 0.0)], axis=1)
            qs_sc[2 * c + pr] = jnp.concatenate([head_a, head_b], axis=0).astype(BF16)

    lane2 = lax.broadcasted_iota(jnp.int32, (1, 2 * TQ), 1)
    hide_first_chunk = jnp.where((lane2 & (TQ - 1)) < CHUNK, NEG_INF, 0.0)

    def scores_into(kb, c, s_sc, nk=B_TK):
        start = pl.multiple_of(kb * B_TK, B_TK)
        for pr in range(2):
            s_sc[pr, 0:nk, :] = _dot_nt(k_refs[pr][0, pl.ds(start, nk), :], qs_sc[2 * c + pr])

    def transpose_values(kb, slot=0):
        vt_sc[slot] = v_ref[0, pl.ds(pl.multiple_of(kb * B_TK, B_TK), B_TK), :].T

    def consume(kb, c, s_sc, nk=B_TK, diagonal=False, slot=0):
        for pr in range(2):
            u = 2 * c + pr
            s = s_sc[pr, 0:nk, :]
            if diagonal:
                s = jnp.concatenate([s[:nk - CHUNK], s[nk - CHUNK:] + hide_first_chunk], axis=0)
            m_prev = m_sc[u]
            m_new = jnp.maximum(m_prev, jnp.max(s, axis=0, keepdims=True))
            alpha = jnp.exp2(m_prev - m_new)
            p = jnp.exp2(s - m_new)
            l_sc[u] = alpha * l_sc[u] + jnp.sum(p, axis=0, keepdims=True)
            m_sc[u] = m_new
            acc_sc[u] = alpha * acc_sc[u] + _dot(vt_sc[slot, 128 * pr:128 * (pr + 1), 0:nk], p.astype(BF16))

    bufs = (sa_sc, sb_sc)
    for c in range(n_tiles):
        stack_queries(c)
    scores_into(0, 0, bufs[0])
    m_sc[...] = jnp.full(m_sc.shape, NEG_INF, F32)
    l_sc[...] = jnp.zeros(l_sc.shape, F32)
    acc_sc[...] = jnp.zeros(acc_sc.shape, F32)

    def past_block(kb, slot):
        for c in range(n_tiles):
            if c + 1 < n_tiles:
                scores_into(kb, c + 1, bufs[(c + 1) % 2])
            else:
                scores_into(kb + 1, 0, bufs[0])
            consume(kb, c, bufs[c % 2], slot=slot)

    def two_blocks(i, carry):
        transpose_values(2 * i, 0)
        transpose_values(2 * i + 1, 1)
        past_block(2 * i, 0)
        past_block(2 * i + 1, 1)
        return carry

    def one_block(kb, carry):
        transpose_values(kb, 0)
        past_block(kb, 0)
        return carry

    lax.fori_loop(0, t >> 1, two_blocks, 0)
    lax.fori_loop(t & ~1, t, one_block, 0)

    transpose_values(t)
    for c in range(n_tiles):
        if c + 1 < n_tiles:
            scores_into(t, c + 1, bufs[(c + 1) % 2], nk=TQ * (c + 2))
        consume(t, c, bufs[c % 2], nk=TQ * (c + 1), diagonal=True)
        parts = []
        for pr in range(2):
            u = 2 * c + pr
            inv = 1.0 / l_sc[u]
            for e in range(2):
                parts.append(acc_sc[u, 64 * e:64 * (e + 1), 128 * e:128 * (e + 1)] * inv[:, 128 * e:128 * (e + 1)])
        o = jnp.concatenate(parts, axis=0).T
        o_ref[0, TQ * c:TQ * (c + 1), :] = (o * g_ref[0, TQ * c:TQ * (c + 1), :].astype(F32)).astype(BF16)


def _attn_b(p3):
    b, s, _ = p3.shape
    n_units = 2 * (B_TQ // TQ)
    return pl.pallas_call(
        _attn_b_kernel,
        grid=(b, s // B_TQ),
        in_specs=[
            pl.BlockSpec((1, B_TQ, 256), lambda i, t: (i, t, P_BQ0 // 256)),
            pl.BlockSpec((1, B_TQ, 256), lambda i, t: (i, t, P_BQ1 // 256)),
            pl.BlockSpec((1, s, 256), lambda i, t: (i, 0, P_BK0 // 256)),
            pl.BlockSpec((1, s, 256), lambda i, t: (i, 0, P_BK1 // 256)),
            pl.BlockSpec((1, s, 256), lambda i, t: (i, 0, P_BV // 256)),
            pl.BlockSpec((1, B_TQ, 256), lambda i, t: (i, t, P_GB // 256)),
        ],
        out_specs=pl.BlockSpec((1, B_TQ, 256), lambda i, t: (i, t, 0)),
        out_shape=jax.ShapeDtypeStruct((b, s, GROUP), BF16),
        scratch_shapes=[pltpu.VMEM((n_units, 2 * TQ, 256), BF16),
                        pltpu.VMEM((2, GROUP, B_TK), BF16),
                        pltpu.VMEM((2, B_TK, 2 * TQ), F32),
                        pltpu.VMEM((2, B_TK, 2 * TQ), F32),
                        pltpu.VMEM((n_units, 1, 2 * TQ), F32),
                        pltpu.VMEM((n_units, 1, 2 * TQ), F32),
                        pltpu.VMEM((n_units, 128, 2 * TQ), F32)],
        compiler_params=pltpu.CompilerParams(
            dimension_semantics=("parallel", "arbitrary"), vmem_limit_bytes=VMEM_LIMIT),
        name="attn_b",
    )(p3, p3, p3, p3, p3, p3)


def _residual_norm(ya_ref, yb_ref, yc_ref, ym_ref, x_ref, w_ref, g_ref, b_ref, rows=slice(None)):
    y = (_dot(ya_ref[rows, :], w_ref[0:256, :]) + _dot(yb_ref[rows, :], w_ref[256:512, :])
         + _dot(yc_ref[rows, :], w_ref[512:768, :]) + _dot(ym_ref[rows, :], w_ref[768:1024, :]))
    z = ALPHA * x_ref[rows, :] + y
    mu = jnp.mean(z, axis=-1, keepdims=True)
    zc = z - mu
    var = jnp.mean(zc * zc, axis=-1, keepdims=True)
    return zc * lax.rsqrt(var + 1e-5) * g_ref[...] + b_ref[...]


def _outproj_kernel(ya_ref, yb_ref, yc_ref, ym_ref, x_ref, w_ref, g_ref, b_ref, o_ref):
    for h in range(TM_OUT // 256):
        rows = pl.ds(h * 256, 256)
        o_ref[rows, :] = _residual_norm(ya_ref, yb_ref, yc_ref, ym_ref, x_ref, w_ref, g_ref, b_ref, rows)


def _out_in_proj_kernel(ya_ref, yb_ref, yc_ref, ym_ref, x_ref, wo_ref, g_ref, b_ref,
                        tab_ref, w_ref, wuq_ref, wukv_ref, gq_ref, gkv_ref, o_ref, p_ref):
    halves = [pl.ds(h * (TM // 2), TM // 2) for h in range(2)]
    xb = []
    for rows in halves:
        xn = _residual_norm(ya_ref, yb_ref, yc_ref, ym_ref, x_ref, wo_ref, g_ref, b_ref, rows)
        o_ref[rows, :] = xn
        xb.append(xn.astype(BF16))
    for rows, x in zip(halves, xb):
        _project(x, tab_ref, w_ref, wuq_ref, wukv_ref, gq_ref, gkv_ref, p_ref, rows)


def _outproj(ya, yb, yc, ym, x2d, w, g, bias):
    n = x2d.shape[0]
    ytile = pl.BlockSpec((TM_OUT, GROUP), lambda i: (i, 0))
    const = lambda shape: pl.BlockSpec(shape, lambda i: (0,) * len(shape))
    return pl.pallas_call(
        _outproj_kernel,
        grid=(n // TM_OUT,),
        in_specs=[ytile, ytile, ytile, ytile,
                  pl.BlockSpec((TM_OUT, D_MODEL), lambda i: (i, 0)),
                  const((D_MODEL, D_MODEL)), const((1, D_MODEL)), const((1, D_MODEL))],
        out_specs=pl.BlockSpec((TM_OUT, D_MODEL), lambda i: (i, 0)),
        out_shape=jax.ShapeDtypeStruct((n, D_MODEL), F32),
        compiler_params=pltpu.CompilerParams(
            dimension_semantics=("parallel",), vmem_limit_bytes=VMEM_LIMIT),
        name="outproj",
    )(ya, yb, yc, ym, x2d, w, g, bias)


def _out_in_proj(ya, yb, yc, ym, x2d, wo, g, bias, tab, w, wuq, wukv, gq, gkv):
    n = x2d.shape[0]
    ytile = pl.BlockSpec((TM, GROUP), lambda i: (i, 0))
    const = lambda shape: pl.BlockSpec(shape, lambda i: (0,) * len(shape))
    return pl.pallas_call(
        _out_in_proj_kernel,
        grid=(n // TM,),
        in_specs=[ytile, ytile, ytile, ytile,
                  pl.BlockSpec((TM, D_MODEL), lambda i: (i, 0)),
                  const((D_MODEL, D_MODEL)), const((1, D_MODEL)), const((1, D_MODEL)),
                  pl.BlockSpec((TM, 512), lambda i: (i, 0)),
                  const((D_MODEL, W_WIDTH)), const((256, 512)), const((128, 512)),
                  const((1, 256)), const((1, 128))],
        out_specs=[pl.BlockSpec((TM, D_MODEL), lambda i: (i, 0)),
                   pl.BlockSpec((TM, P_WIDTH), lambda i: (i, 0))],
        out_shape=[jax.ShapeDtypeStruct((n, D_MODEL), F32),
                   jax.ShapeDtypeStruct((n, P_WIDTH), BF16)],
        compiler_params=pltpu.CompilerParams(
            dimension_semantics=("parallel",), vmem_limit_bytes=VMEM_LIMIT),
        name="out_in_proj",
    )(ya, yb, yc, ym, x2d, wo, g, bias, tab, w, wuq, wukv, gq, gkv)


def kernel(x, mem, positions, w_in, rel_bias, mla_q_norm, w_uq, mla_kv_norm, w_ukv,
           swa_sinks, w_mem_kv, w_out, ln_gain, ln_bias):
    b, s, d = x.shape
    depth = w_in.shape[0]
    assert d == D_MODEL and depth == DEPTH and s % B_TK == 0 and s >= A_WIN and (b * s) % TM_OUT == 0 and TM_OUT % TM == 0

    cols, cperm = _inproj_cols()
    w_in_p = _take_cols(w_in, cols).astype(BF16)
    wuq_p = jnp.pad(_take_cols(w_uq, _uq_cols()), ((0, 0), (0, 256 - MLA_Q_RANK), (0, 0))).astype(BF16)
    wukv_p = _take_cols(w_ukv, _ukv_cols()).astype(BF16)
    gq = jnp.pad(mla_q_norm, ((0, 0), (0, 256 - MLA_Q_RANK)))[:, None, :]
    gkv = mla_kv_norm[:, None, :]
    rows = np.concatenate([np.arange(512), 512 + cperm, np.arange(768, 1024)])
    w_out_p = _take_cols(w_out, rows, axis=1).astype(BF16)
    w_mem_all = jnp.transpose(w_mem_kv, (1, 0, 2)).reshape(D_MODEL, depth * 512).astype(BF16)
    e_a = _bias_table_a(rel_bias)
    e_c = jnp.asarray(_mask_table_c())
    pieces, expand3 = _rope_pieces(positions)

    memkv = _memkv(mem, w_mem_all)
    h = x.reshape(b * s, d)
    p2, tab = _inproj(h, pieces, expand3, w_in_p[0], wuq_p[0], wukv_p[0], gq[0], gkv[0])
    for l in range(depth):
        p3 = p2.reshape(b, s, P_WIDTH)
        ya = _window_attn(p3, P_AQ, p3, P_AK, p3, P_AV, P_GA, win=A_WIN, prev=A_PREV * CHUNK, dk=256, table=e_a[l])
        yb = _attn_b(p3)
        yc = _window_attn(p3, P_CQ, p3, P_CK, p3, P_CV, P_GC, win=C_WIN, prev=SWA_PREV * CHUNK, dk=128,
                          table=e_c, sinks=swa_sinks[l])
        ym = _window_attn(p3, P_MQ, memkv, 512 * l, memkv, 512 * l + 256, P_GM, win=MEM_LEN, prev=None, dk=256)
        ys = [y.reshape(b * s, GROUP) for y in (ya, yb, yc, ym)]
        ln = (ln_gain[l][None, :], ln_bias[l][None, :])
        if l + 1 < depth:
            h, p2 = _out_in_proj(*ys, h, w_out_p[l], *ln, tab, w_in_p[l + 1], wuq_p[l + 1], wukv_p[l + 1],
                                 gq[l + 1], gkv[l + 1])
        else:
            h = _outproj(*ys, h, w_out_p[l], *ln)
    return h.reshape(b, s, d)
```

```python
import functools

import numpy as np
import jax
import jax.numpy as jnp
from jax import lax
from jax.experimental import pallas as pl
from jax.experimental.pallas import tpu as pltpu

F32 = jnp.float32
BF16 = jnp.bfloat16

D_MODEL = 1024
DEPTH = 4
CHUNK = 64
HEAD_DIM = 64
GROUP = 256
N_HEADS = 4
ROPE_THETA = 10000.0
NEG_INF = -1e30
A_PREV = 8
REL_CLIP = 128
MLA_NOPE = 64
MLA_ROPE = 32
MLA_Q_RANK = 192
MLA_KV_RANK = 128
SWA_PREV = 2
MEM_LEN = 256
ALPHA = (2.0 * DEPTH) ** 0.25

TQ = 128
A_WIN = TQ + A_PREV * CHUNK
C_WIN = TQ + SWA_PREV * CHUNK
WIN_UNROLL = 16
B_TK = 512
B_TQ = 512
TM = 512
TM_OUT = 1024
VMEM_LIMIT = 56 * 1024 * 1024
LOG2E = 1.4426950408889634
QSCALE = HEAD_DIM ** -0.5 * LOG2E
B_QSCALE = (MLA_NOPE + MLA_ROPE) ** -0.5 * LOG2E

P_AQ, P_AK, P_AV = 0, 256, 512
P_BQ0, P_BQ1, P_BK0, P_BK1, P_BV = 768, 1024, 1280, 1536, 1792
P_CQ, P_CK, P_CV = 2048, 2304, 2432
P_MQ = 2560
P_GA, P_GB, P_GC, P_GM = 2816, 3072, 3328, 3584
P_WIDTH = 3840

W_A, W_C, W_M, W_G, W_B, W_WIDTH = 0, 768, 1280, 1536, 2560, 3072

C_HEAD_ORDER = (0, 2, 1, 3)


def _inproj_cols():
    r = np.arange
    aq, ak, av, ag = 0, 256, 512, 768
    bcq, bckv, bkr, bg = 1024, 1216, 1344, 1376
    cq, ck, cv, cg = 1632, 1888, 2016, 2144
    mq, mg = 2400, 2656
    cperm = np.concatenate([r(64) + 64 * h for h in C_HEAD_ORDER])
    pad = lambda n: np.full(n, -1)
    cols = np.concatenate([
        aq + r(256), ak + r(256), av + r(256),
        cq + cperm, ck + r(128), cv + r(128),
        mq + r(256),
        ag + r(256), bg + r(256), cg + cperm, mg + r(256),
        bcq + r(192), pad(64), bckv + r(128), bkr + r(32), bkr + r(32), pad(64),
    ])
    assert cols.shape[0] == W_WIDTH
    return cols, cperm


def _take_cols(w, cols, axis=-1):
    axis = axis % w.ndim
    pieces, i = [], 0
    while i < len(cols):
        j = i + 1
        if cols[i] < 0:
            while j < len(cols) and cols[j] < 0:
                j += 1
            shape = w.shape[:axis] + (j - i,) + w.shape[axis + 1:]
            pieces.append(jnp.zeros(shape, w.dtype))
        else:
            while j < len(cols) and cols[j] == cols[j - 1] + 1:
                j += 1
            pieces.append(lax.slice_in_dim(w, int(cols[i]), int(cols[i]) + (j - i), axis=axis))
        i = j
    return jnp.concatenate(pieces, axis=axis)


def _uq_cols():
    r = np.arange
    per = MLA_NOPE + MLA_ROPE
    out = []
    for p in range(2):
        h0, h1 = 2 * p, 2 * p + 1
        out += [per * h0 + r(64), per * h1 + r(64),
                per * h0 + 64 + r(32), per * h1 + 64 + r(32), np.full(64, -1)]
    return np.concatenate(out)


def _ukv_cols():
    r = np.arange
    return np.concatenate([128 * h + r(64) for h in range(4)] + [128 * h + 64 + r(64) for h in range(4)])


def _rope_pieces(positions):
    pos = positions.astype(F32).reshape(1, -1)
    half = HEAD_DIM // 2
    inv = ROPE_THETA ** (-jnp.arange(0, HEAD_DIM, 2, dtype=F32) / HEAD_DIM)
    ang = inv[:, None] * pos
    x = jnp.concatenate([jnp.cos(ang), jnp.sin(ang)], axis=0)
    expand = np.zeros((2 * half, 512), np.float32)
    lane = np.arange(128)
    for t, d in enumerate((HEAD_DIM, MLA_ROPE)):
        k = ((lane % d) % (d // 2)) * (HEAD_DIM // d)
        expand[k, 256 * t + lane] = 1.0
        expand[half + k, 256 * t + 128 + lane] = np.where(lane % d < d // 2, -1.0, 1.0)
    x = lax.optimization_barrier(x)
    hi = x.astype(BF16)
    rest = x - hi.astype(F32)
    mid = rest.astype(BF16)
    lo = (rest - mid.astype(F32)).astype(BF16)
    return jnp.concatenate([hi, mid, lo], axis=0), jnp.asarray(np.concatenate([expand] * 3, axis=0), BF16)


def _bias_table_a(rel_bias):
    width, period = 9 * 128, 9 * 128 + TQ + 1
    k = np.arange(period)
    d = np.where(k < width, A_PREV * CHUNK - k, A_PREV * CHUNK + period - k)
    idx = np.clip(d, -REL_CLIP, REL_CLIP) + REL_CLIP
    n_hi = A_PREV * CHUNK - REL_CLIP + 1
    n_lo = width - n_hi - (2 * REL_CLIP - 1)
    expect = np.concatenate([np.full(n_hi, 2 * REL_CLIP), np.arange(2 * REL_CLIP - 1, 0, -1),
                             np.zeros(n_lo, np.int64), np.full(period - width, 2 * REL_CLIP)])
    assert np.array_equal(idx, expect)
    rep = lambda col, n: jnp.broadcast_to(rel_bias[:, :, col:col + 1], rel_bias.shape[:2] + (n,))
    gp = jnp.concatenate([rep(2 * REL_CLIP, n_hi), jnp.flip(rel_bias[:, :, 1:2 * REL_CLIP], axis=-1),
                          rep(0, n_lo), rep(2 * REL_CLIP, period - width)], axis=-1) * LOG2E
    flat = jnp.tile(gp, (1, 1, TQ))[:, :, :TQ * (period - 1)]
    skew = flat.reshape(gp.shape[0], N_HEADS, TQ, period - 1)[..., :width]
    i = np.arange(TQ)[:, None]
    m = np.arange(width)[None, :]
    dchunk = i // CHUNK + A_PREV - m // CHUNK
    valid = (dchunk >= 0) & (dchunk <= A_PREV)
    t = jnp.where(jnp.asarray(valid)[None, None], skew, NEG_INF)
    return jnp.transpose(t, (0, 3, 1, 2)).reshape(gp.shape[0], 9, 128, N_HEADS * TQ)


def _mask_table_c():
    m = np.arange(3 * 128)[:, None]
    i = np.arange(TQ)[None, :]
    dchunk = i // CHUNK + SWA_PREV - m // CHUNK
    valid = (dchunk >= 0) & (dchunk <= SWA_PREV)
    t = np.where(valid, 0.0, NEG_INF).astype(np.float32)
    return np.tile(t, (1, N_HEADS)).reshape(3, 128, N_HEADS * TQ)


def _dot(a, b):
    return jnp.dot(a, b, preferred_element_type=F32)


def _dot_nt(a, b):
    return lax.dot_general(a, b, (((1,), (1,)), ((), ())), preferred_element_type=F32)


def _dot_tn(a, b):
    return lax.dot_general(a, b, (((0,), (0,)), ((), ())), preferred_element_type=F32)


def _rope(x, cos, sin_signed, half):
    lane = lax.broadcasted_iota(jnp.int32, x.shape, 1)
    first = (lane & (2 * half - 1)) < half
    swapped = jnp.where(first, pltpu.roll(x, 128 - half, 1), pltpu.roll(x, half, 1))
    return x * cos + swapped * sin_signed


def _project(xb, tab_ref, w_ref, wuq_ref, wukv_ref, gq_ref, gkv_ref, p_ref, rows=slice(None)):
    tab_ref, p_ref = tab_ref.at[rows], p_ref.at[rows]
    cos64, sin64 = tab_ref[:, 0:128], tab_ref[:, 128:256]
    cos32, sin32 = tab_ref[:, 256:384], tab_ref[:, 384:512]

    def mm(lo, hi):
        return _dot(xb, w_ref[:, lo:hi])

    rb = mm(W_B, W_B + 512)
    cq = rb[:, 0:256]
    ms = jnp.sum(cq * cq, axis=-1, keepdims=True) * (1.0 / MLA_Q_RANK)
    qn = (cq * lax.rsqrt(ms + 1e-6) * gq_ref[...]).astype(BF16)
    ckv = rb[:, 256:384]
    ms = jnp.mean(ckv * ckv, axis=-1, keepdims=True)
    kvn = (ckv * lax.rsqrt(ms + 1e-6) * gkv_ref[...]).astype(BF16)
    krb = _rope(rb[:, 384:512], cos32, sin32, 16).astype(BF16)

    r = mm(W_A, W_A + 768)
    p_ref[:, P_AQ:P_AQ + 256] = (r[:, 0:256] * QSCALE).astype(BF16)
    p_ref[:, P_AK:P_AK + 512] = r[:, 256:768].astype(BF16)

    q = _dot(qn, wuq_ref[...]) * B_QSCALE
    for p in range(2):
        base = P_BQ0 + 256 * p
        p_ref[:, base:base + 128] = q[:, 256 * p:256 * p + 128].astype(BF16)
        p_ref[:, base + 128:base + 256] = _rope(q[:, 256 * p + 128:256 * p + 256], cos32, sin32, 16).astype(BF16)
    kv = _dot(kvn, wukv_ref[...])
    for p in range(2):
        base = P_BK0 + 256 * p
        p_ref[:, base:base + 128] = kv[:, 128 * p:128 * (p + 1)].astype(BF16)
        p_ref[:, base + 128:base + 256] = krb
    p_ref[:, P_BV:P_BV + 256] = kv[:, 256:512].astype(BF16)

    r = mm(W_G, W_G + 1024)
    p_ref[:, P_GA:P_GA + 1024] = (r * (1.0 / (1.0 + jnp.exp(-r)))).astype(BF16)

    r = mm(W_C, W_C + 512)
    for j in range(2):
        qj = _rope(r[:, 128 * j:128 * (j + 1)], cos64, sin64, 32)
        p_ref[:, P_CQ + 128 * j:P_CQ + 128 * (j + 1)] = (qj * QSCALE).astype(BF16)
    p_ref[:, P_CK:P_CK + 128] = _rope(r[:, 256:384], cos64, sin64, 32).astype(BF16)
    p_ref[:, P_CV:P_CV + 128] = r[:, 384:512].astype(BF16)

    r = mm(W_M, W_M + 256)
    p_ref[:, P_MQ:P_MQ + 256] = (r * QSCALE).astype(BF16)


def _inproj_kernel(x_ref, pieces_ref, expand_ref, w_ref, wuq_ref, wukv_ref, gq_ref, gkv_ref, p_ref, tab_ref):
    tab_ref[...] = _dot_tn(pieces_ref[...], expand_ref[...])
    _project(x_ref[...].astype(BF16), tab_ref, w_ref, wuq_ref, wukv_ref, gq_ref, gkv_ref, p_ref)


def _inproj(x2d, pieces, expand3, w, wuq, wukv, gq, gkv):
    n = x2d.shape[0]
    const = lambda shape: pl.BlockSpec(shape, lambda i: (0,) * len(shape))
    return pl.pallas_call(
        _inproj_kernel,
        grid=(n // TM,),
        in_specs=[
            pl.BlockSpec((TM, D_MODEL), lambda i: (i, 0)),
            pl.BlockSpec((192, TM), lambda i: (0, i)),
            const((192, 512)),
            const((D_MODEL, W_WIDTH)),
            const((256, 512)),
            const((128, 512)),
            const((1, 256)),
            const((1, 128)),
        ],
        out_specs=[pl.BlockSpec((TM, P_WIDTH), lambda i: (i, 0)),
                   pl.BlockSpec((TM, 512), lambda i: (i, 0))],
        out_shape=[jax.ShapeDtypeStruct((n, P_WIDTH), BF16),
                   jax.ShapeDtypeStruct((n, 512), F32)],
        compiler_params=pltpu.CompilerParams(
            dimension_semantics=("parallel",), vmem_limit_bytes=VMEM_LIMIT),
        name="inproj",
    )(x2d, pieces, expand3, w, wuq, wukv, gq, gkv)


def _memkv_kernel(mem_ref, w_ref, o_ref):
    o_ref[0] = _dot(mem_ref[0].astype(BF16), w_ref[...]).astype(BF16)


def _memkv(mem, w_all):
    b = mem.shape[0]
    n = w_all.shape[1]
    return pl.pallas_call(
        _memkv_kernel,
        grid=(b,),
        in_specs=[pl.BlockSpec((1, MEM_LEN, D_MODEL), lambda i: (i, 0, 0)),
                  pl.BlockSpec((D_MODEL, n), lambda i: (0, 0))],
        out_specs=pl.BlockSpec((1, MEM_LEN, n), lambda i: (i, 0, 0)),
        out_shape=jax.ShapeDtypeStruct((b, MEM_LEN, n), BF16),
        compiler_params=pltpu.CompilerParams(
            dimension_semantics=("parallel",), vmem_limit_bytes=VMEM_LIMIT),
        name="memkv",
    )(mem, w_all)


def _window_attn_kernel(*refs, win, prev, dk, has_table, has_sink):
    refs = list(refs)
    sink_ref = refs.pop(0) if has_sink else None
    q_ref, k_ref, v_ref, g_ref = refs[:4]
    e_ref = refs[4] if has_table else None
    o_ref, sa_sc, sb_sc = refs[-3:]
    n_items = q_ref.shape[1] // TQ
    lanes = N_HEADS * TQ

    lane128 = lax.broadcasted_iota(jnp.int32, (TQ, 128), 1)
    lo, hi = lane128 < HEAD_DIM, lane128 >= HEAD_DIM
    if has_sink:
        col = lax.broadcasted_iota(jnp.int32, (1, lanes), 1)
        order = C_HEAD_ORDER if dk == 128 else tuple(range(N_HEADS))
        sink = jnp.where(col < TQ, sink_ref[order[0]],
                         jnp.where(col < 2 * TQ, sink_ref[order[1]],
                                   jnp.where(col < 3 * TQ, sink_ref[order[2]], sink_ref[order[3]]))) * LOG2E

    def window_start(item):
        if prev is None:
            return 0
        return pl.multiple_of(jnp.maximum(item * TQ - prev, 0), 128)

    def scores_into(item, s_sc):
        q = q_ref[0, pl.ds(pl.multiple_of(item * TQ, TQ), TQ), :].astype(F32)
        if dk == 256:
            zero = jnp.zeros((TQ, 128), F32)
            blocks = [jnp.concatenate([jnp.where(lo, q[:, 0:128], 0.0), zero], axis=1),
                      jnp.concatenate([jnp.where(hi, q[:, 0:128], 0.0), zero], axis=1),
                      jnp.concatenate([zero, jnp.where(lo, q[:, 128:256], 0.0)], axis=1),
                      jnp.concatenate([zero, jnp.where(hi, q[:, 128:256], 0.0)], axis=1)]
        else:
            blocks = [jnp.where(lo, q[:, 0:128], 0.0), jnp.where(hi, q[:, 0:128], 0.0),
                      jnp.where(lo, q[:, 128:256], 0.0), jnp.where(hi, q[:, 128:256], 0.0)]
        qs = jnp.concatenate(blocks, axis=0).astype(BF16)
        s = _dot_nt(k_ref[0, pl.ds(window_start(item), win), :], qs)
        if has_table:
            mb0 = jnp.maximum(prev // 128 - item, 0)
            s = s + jnp.concatenate([e_ref[mb0 + jb] for jb in range(win // 128)], axis=0)
        s_sc[...] = s

    def consume(item, s_sc):
        start = window_start(item)
        v = v_ref[0, pl.ds(start, win), :]
        parts = []
        for pr in range(2):
            cols = slice(256 * pr, 256 * (pr + 1))
            m = jnp.max(s_sc[:, cols], axis=0, keepdims=True)
            if has_sink:
                m = jnp.maximum(m, sink[:, cols])
            p = jnp.exp2(s_sc[:, cols] - m)
            l = jnp.sum(p, axis=0, keepdims=True)
            if has_sink:
                l = l + jnp.exp2(sink[:, cols] - m)
            inv = 1.0 / l
            vp = v[:, 128 * pr:128 * (pr + 1)] if dk == 256 else v
            ot = _dot_tn(vp, p.astype(BF16))
            for e in range(2):
                parts.append(ot[64 * e:64 * (e + 1), 128 * e:128 * (e + 1)] * inv[:, TQ * e:TQ * (e + 1)])
        o = jnp.concatenate(parts, axis=0).T
        rows = pl.ds(pl.multiple_of(item * TQ, TQ), TQ)
        o_ref[0, rows, :] = (o * g_ref[0, rows, :].astype(F32)).astype(BF16)

    bufs = (sa_sc, sb_sc)
    scores_into(0, bufs[0])

    def body(i, carry):
        for j in range(WIN_UNROLL):
            item = WIN_UNROLL * i + j
            scores_into(jnp.minimum(item + 1, n_items - 1), bufs[(j + 1) % 2])
            consume(item, bufs[j % 2])
        return carry

    lax.fori_loop(0, n_items // WIN_UNROLL, body, 0)


def _window_attn(q_src, q_col, k_src, k_col, v_src, v_col, g_col, *, win, prev, dk, table=None, sinks=None):
    b, s, _ = q_src.shape
    skv = k_src.shape[1]
    assert s % (WIN_UNROLL * TQ) == 0 and skv >= win
    kern = functools.partial(_window_attn_kernel, win=win, prev=prev, dk=dk,
                             has_table=table is not None, has_sink=sinks is not None)
    in_specs, args = [], []
    if sinks is not None:
        in_specs.append(pl.BlockSpec(memory_space=pltpu.SMEM)); args.append(sinks)
    in_specs += [pl.BlockSpec((1, s, 256), lambda i: (i, 0, q_col // 256)),
                 pl.BlockSpec((1, skv, dk), lambda i: (i, 0, k_col // dk)),
                 pl.BlockSpec((1, skv, dk), lambda i: (i, 0, v_col // dk)),
                 pl.BlockSpec((1, s, 256), lambda i: (i, 0, g_col // 256))]
    args += [q_src, k_src, v_src, q_src]
    if table is not None:
        in_specs.append(pl.BlockSpec(table.shape, lambda i: (0, 0, 0))); args.append(table)
    return pl.pallas_call(
        kern,
        grid=(b,),
        in_specs=in_specs,
        out_specs=pl.BlockSpec((1, s, 256), lambda i: (i, 0, 0)),
        out_shape=jax.ShapeDtypeStruct((b, s, GROUP), BF16),
        scratch_shapes=[pltpu.VMEM((win, N_HEADS * TQ), F32), pltpu.VMEM((win, N_HEADS * TQ), F32)],
        compiler_params=pltpu.CompilerParams(
            dimension_semantics=("parallel",), vmem_limit_bytes=VMEM_LIMIT),
        name="attn_win%d" % win,
    )(*args)


def _attn_b_kernel(q0_ref, q1_ref, k0_ref, k1_ref, v_ref, g_ref, o_ref,
                   qs_sc, vt_sc, sa_sc, sb_sc, m_sc, l_sc, acc_sc):
    t = pl.program_id(1)
    n_tiles = B_TQ // TQ
    lane = lax.broadcasted_iota(jnp.int32, (TQ, 128), 1)
    q_refs = (q0_ref, q1_ref)
    k_refs = (k0_ref, k1_ref)

    def stack_queries(c):
        for pr in range(2):
            q = q_refs[pr][0, TQ * c:TQ * (c + 1), :].astype(F32)
            nope, rope = q[:, 0:128], q[:, 128:256]
            head_a = jnp.concatenate([jnp.where(lane < 64, nope, 0.0), jnp.where(lane < 32, rope, 0.0)], axis=1)
            head_b = jnp.concatenate([jnp.where(lane >= 64, nope, 0.0), jnp.where(lane >= 32, rope, 0.0)], axis=1)
            qs_sc[2 * c + pr] = jnp.concatenate([head_a, head_b], axis=0).astype(BF16)

    lane2 = lax.broadcasted_iota(jnp.int32, (1, 2 * TQ), 1)
    hide_first_chunk = jnp.where((lane2 & (TQ - 1)) < CHUNK, NEG_INF, 0.0)

    def scores_into(kb, c, s_sc, nk=B_TK):
        start = pl.multiple_of(kb * B_TK, B_TK)
        for pr in range(2):
            s_sc[pr, 0:nk, :] = _dot_nt(k_refs[pr][0, pl.ds(start, nk), :], qs_sc[2 * c + pr])

    def transpose_values(kb, slot=0):
        vt_sc[slot] = v_ref[0, pl.ds(pl.multiple_of(kb * B_TK, B_TK), B_TK), :].T

    def consume(kb, c, s_sc, nk=B_TK, diagonal=False, slot=0):
        for pr in range(2):
            u = 2 * c + pr
            s = s_sc[pr, 0:nk, :]
            if diagonal:
                s = jnp.concatenate([s[:nk - CHUNK], s[nk - CHUNK:] + hide_first_chunk], axis=0)
            m_prev = m_sc[u]
            m_new = jnp.maximum(m_prev, jnp.max(s, axis=0, keepdims=True))
            alpha = jnp.exp2(m_prev - m_new)
            p = jnp.exp2(s - m_new)
            l_sc[u] = alpha * l_sc[u] + jnp.sum(p, axis=0, keepdims=True)
            m_sc[u] = m_new
            acc_sc[u] = alpha * acc_sc[u] + _dot(vt_sc[slot, 128 * pr:128 * (pr + 1), 0:nk], p.astype(BF16))

    bufs = (sa_sc, sb_sc)
    for c in range(n_tiles):
        stack_queries(c)
    scores_into(0, 0, bufs[0])
    m_sc[...] = jnp.full(m_sc.shape, NEG_INF, F32)
    l_sc[...] = jnp.zeros(l_sc.shape, F32)
    acc_sc[...] = jnp.zeros(acc_sc.shape, F32)

    def past_block(kb, slot):
        for c in range(n_tiles):
            if c + 1 < n_tiles:
                scores_into(kb, c + 1, bufs[(c + 1) % 2])
            else:
                scores_into(kb + 1, 0, bufs[0])
            consume(kb, c, bufs[c % 2], slot=slot)

    def two_blocks(i, carry):
        transpose_values(2 * i, 0)
        transpose_values(2 * i + 1, 1)
        past_block(2 * i, 0)
        past_block(2 * i + 1, 1)
        return carry

    def one_block(kb, carry):
        transpose_values(kb, 0)
        past_block(kb, 0)
        return carry

    lax.fori_loop(0, t >> 1, two_blocks, 0)
    lax.fori_loop(t & ~1, t, one_block, 0)

    transpose_values(t)
    for c in range(n_tiles):
        if c + 1 < n_tiles:
            scores_into(t, c + 1, bufs[(c + 1) % 2], nk=TQ * (c + 2))
        consume(t, c, bufs[c % 2], nk=TQ * (c + 1), diagonal=True)
        parts = []
        for pr in range(2):
            u = 2 * c + pr
            inv = 1.0 / l_sc[u]
            for e in range(2):
                parts.append(acc_sc[u, 64 * e:64 * (e + 1), 128 * e:128 * (e + 1)] * inv[:, 128 * e:128 * (e + 1)])
        o = jnp.concatenate(parts, axis=0).T
        o_ref[0, TQ * c:TQ * (c + 1), :] = (o * g_ref[0, TQ * c:TQ * (c + 1), :].astype(F32)).astype(BF16)


def _attn_b(p3):
    b, s, _ = p3.shape
    n_units = 2 * (B_TQ // TQ)
    return pl.pallas_call(
        _attn_b_kernel,
        grid=(b, s // B_TQ),
        in_specs=[
            pl.BlockSpec((1, B_TQ, 256), lambda i, t: (i, t, P_BQ0 // 256)),
            pl.BlockSpec((1, B_TQ, 256), lambda i, t: (i, t, P_BQ1 // 256)),
            pl.BlockSpec((1, s, 256), lambda i, t: (i, 0, P_BK0 // 256)),
            pl.BlockSpec((1, s, 256), lambda i, t: (i, 0, P_BK1 // 256)),
            pl.BlockSpec((1, s, 256), lambda i, t: (i, 0, P_BV // 256)),
            pl.BlockSpec((1, B_TQ, 256), lambda i, t: (i, t, P_GB // 256)),
        ],
        out_specs=pl.BlockSpec((1, B_TQ, 256), lambda i, t: (i, t, 0)),
        out_shape=jax.ShapeDtypeStruct((b, s, GROUP), BF16),
        scratch_shapes=[pltpu.VMEM((n_units, 2 * TQ, 256), BF16),
                        pltpu.VMEM((2, GROUP, B_TK), BF16),
                        pltpu.VMEM((2, B_TK, 2 * TQ), F32),
                        pltpu.VMEM((2, B_TK, 2 * TQ), F32),
                        pltpu.VMEM((n_units, 1, 2 * TQ), F32),
                        pltpu.VMEM((n_units, 1, 2 * TQ), F32),
                        pltpu.VMEM((n_units, 128, 2 * TQ), F32)],
        compiler_params=pltpu.CompilerParams(
            dimension_semantics=("parallel", "arbitrary"), vmem_limit_bytes=VMEM_LIMIT),
        name="attn_b",
    )(p3, p3, p3, p3, p3, p3)


def _residual_norm(ya_ref, yb_ref, yc_ref, ym_ref, x_ref, w_ref, g_ref, b_ref, rows=slice(None)):
    y = (_dot(ya_ref[rows, :], w_ref[0:256, :]) + _dot(yb_ref[rows, :], w_ref[256:512, :])
         + _dot(yc_ref[rows, :], w_ref[512:768, :]) + _dot(ym_ref[rows, :], w_ref[768:1024, :]))
    z = ALPHA * x_ref[rows, :] + y
    mu = jnp.mean(z, axis=-1, keepdims=True)
    zc = z - mu
    var = jnp.mean(zc * zc, axis=-1, keepdims=True)
    return zc * lax.rsqrt(var + 1e-5) * g_ref[...] + b_ref[...]


def _outproj_kernel(ya_ref, yb_ref, yc_ref, ym_ref, x_ref, w_ref, g_ref, b_ref, o_ref):
    for h in range(TM_OUT // 256):
        rows = pl.ds(h * 256, 256)
        o_ref[rows, :] = _residual_norm(ya_ref, yb_ref, yc_ref, ym_ref, x_ref, w_ref, g_ref, b_ref, rows)


def _out_in_proj_kernel(ya_ref, yb_ref, yc_ref, ym_ref, x_ref, wo_ref, g_ref, b_ref,
                        tab_ref, w_ref, wuq_ref, wukv_ref, gq_ref, gkv_ref, o_ref, p_ref):
    halves = [pl.ds(h * (TM // 2), TM // 2) for h in range(2)]
    xb = []
    for rows in halves:
        xn = _residual_norm(ya_ref, yb_ref, yc_ref, ym_ref, x_ref, wo_ref, g_ref, b_ref, rows)
        o_ref[rows, :] = xn
        xb.append(xn.astype(BF16))
    for rows, x in zip(halves, xb):
        _project(x, tab_ref, w_ref, wuq_ref, wukv_ref, gq_ref, gkv_ref, p_ref, rows)


def _outproj(ya, yb, yc, ym, x2d, w, g, bias):
    n = x2d.shape[0]
    ytile = pl.BlockSpec((TM_OUT, GROUP), lambda i: (i, 0))
    const = lambda shape: pl.BlockSpec(shape, lambda i: (0,) * len(shape))
    return pl.pallas_call(
        _outproj_kernel,
        grid=(n // TM_OUT,),
        in_specs=[ytile, ytile, ytile, ytile,
                  pl.BlockSpec((TM_OUT, D_MODEL), lambda i: (i, 0)),
                  const((D_MODEL, D_MODEL)), const((1, D_MODEL)), const((1, D_MODEL))],
        out_specs=pl.BlockSpec((TM_OUT, D_MODEL), lambda i: (i, 0)),
        out_shape=jax.ShapeDtypeStruct((n, D_MODEL), F32),
        compiler_params=pltpu.CompilerParams(
            dimension_semantics=("parallel",), vmem_limit_bytes=VMEM_LIMIT),
        name="outproj",
    )(ya, yb, yc, ym, x2d, w, g, bias)


def _out_in_proj(ya, yb, yc, ym, x2d, wo, g, bias, tab, w, wuq, wukv, gq, gkv):
    n = x2d.shape[0]
    ytile = pl.BlockSpec((TM, GROUP), lambda i: (i, 0))
    const = lambda shape: pl.BlockSpec(shape, lambda i: (0,) * len(shape))
    return pl.pallas_call(
        _out_in_proj_kernel,
        grid=(n // TM,),
        in_specs=[ytile, ytile, ytile, ytile,
                  pl.BlockSpec((TM, D_MODEL), lambda i: (i, 0)),
                  const((D_MODEL, D_MODEL)), const((1, D_MODEL)), const((1, D_MODEL)),
                  pl.BlockSpec((TM, 512), lambda i: (i, 0)),
                  const((D_MODEL, W_WIDTH)), const((256, 512)), const((128, 512)),
                  const((1, 256)), const((1, 128))],
        out_specs=[pl.BlockSpec((TM, D_MODEL), lambda i: (i, 0)),
                   pl.BlockSpec((TM, P_WIDTH), lambda i: (i, 0))],
        out_shape=[jax.ShapeDtypeStruct((n, D_MODEL), F32),
                   jax.ShapeDtypeStruct((n, P_WIDTH), BF16)],
        compiler_params=pltpu.CompilerParams(
            dimension_semantics=("parallel",), vmem_limit_bytes=VMEM_LIMIT),
        name="out_in_proj",
    )(ya, yb, yc, ym, x2d, wo, g, bias, tab, w, wuq, wukv, gq, gkv)


def kernel(x, mem, positions, w_in, rel_bias, mla_q_norm, w_uq, mla_kv_norm, w_ukv,
           swa_sinks, w_mem_kv, w_out, ln_gain, ln_bias):
    b, s, d = x.shape
    depth = w_in.shape[0]
    assert d == D_MODEL and depth == DEPTH and s % B_TK == 0 and s >= A_WIN and (b * s) % TM_OUT == 0 and TM_OUT % TM == 0

    cols, cperm = _inproj_cols()
    w_in_p = _take_cols(w_in.astype(BF16), cols)
    wuq_p = jnp.pad(_take_cols(w_uq, _uq_cols()), ((0, 0), (0, 256 - MLA_Q_RANK), (0, 0))).astype(BF16)
    wukv_p = _take_cols(w_ukv, _ukv_cols()).astype(BF16)
    gq = jnp.pad(mla_q_norm, ((0, 0), (0, 256 - MLA_Q_RANK)))[:, None, :]
    gkv = mla_kv_norm[:, None, :]
    rows = np.concatenate([np.arange(512), 512 + cperm, np.arange(768, 1024)])
    w_out_p = _take_cols(w_out.astype(BF16), rows, axis=1)
    w_mem_all = jnp.transpose(w_mem_kv.astype(BF16), (1, 0, 2)).reshape(D_MODEL, depth * 512)
    e_a = _bias_table_a(rel_bias)
    e_c = jnp.asarray(_mask_table_c())
    pieces, expand3 = _rope_pieces(positions)

    memkv = _memkv(mem, w_mem_all)
    h = x.reshape(b * s, d)
    p2, tab = _inproj(h, pieces, expand3, w_in_p[0], wuq_p[0], wukv_p[0], gq[0], gkv[0])
    for l in range(depth):
        p3 = p2.reshape(b, s, P_WIDTH)
        ya = _window_attn(p3, P_AQ, p3, P_AK, p3, P_AV, P_GA, win=A_WIN, prev=A_PREV * CHUNK, dk=256, table=e_a[l])
        yb = _attn_b(p3)
        yc = _window_attn(p3, P_CQ, p3, P_CK, p3, P_CV, P_GC, win=C_WIN, prev=SWA_PREV * CHUNK, dk=128,
                          table=e_c, sinks=swa_sinks[l])
        ym = _window_attn(p3, P_MQ, memkv, 512 * l, memkv, 512 * l + 256, P_GM, win=MEM_LEN, prev=None, dk=256)
        ys = [y.reshape(b * s, GROUP) for y in (ya, yb, yc, ym)]
        ln = (ln_gain[l][None, :], ln_bias[l][None, :])
        if l + 1 < depth:
            h, p2 = _out_in_proj(*ys, h, w_out_p[l], *ln, tab, w_in_p[l + 1], wuq_p[l + 1], wukv_p[l + 1],
                                 gq[l + 1], gkv[l + 1])
        else:
            h = _outproj(*ys, h, w_out_p[l], *ln)
    return h.reshape(b, s, d)
```

```python
import functools

import numpy as np
import jax
import jax.numpy as jnp
from jax import lax
from jax.experimental import pallas as pl
from jax.experimental.pallas import tpu as pltpu

F32 = jnp.float32
BF16 = jnp.bfloat16

D_MODEL = 1024
DEPTH = 4
CHUNK = 64
HEAD_DIM = 64
GROUP = 256
N_HEADS = 4
ROPE_THETA = 10000.0
NEG_INF = -1e30
A_PREV = 8
REL_CLIP = 128
MLA_NOPE = 64
MLA_ROPE = 32
MLA_Q_RANK = 192
MLA_KV_RANK = 128
SWA_PREV = 2
MEM_LEN = 256
ALPHA = (2.0 * DEPTH) ** 0.25

TQ = 128
A_WIN = TQ + A_PREV * CHUNK
C_WIN = TQ + SWA_PREV * CHUNK
WIN_UNROLL = 16
B_TK = 512
B_TQ = 512
TM = 512
TM_OUT = 1024
VMEM_LIMIT = 56 * 1024 * 1024
LOG2E = 1.4426950408889634
QSCALE = HEAD_DIM ** -0.5 * LOG2E
B_QSCALE = (MLA_NOPE + MLA_ROPE) ** -0.5 * LOG2E

P_AQ, P_AK, P_AV = 0, 256, 512
P_BQ0, P_BQ1, P_BK0, P_BK1, P_BV = 768, 1024, 1280, 1536, 1792
P_CQ, P_CK, P_CV = 2048, 2304, 2432
P_MQ = 2560
P_GA, P_GB, P_GC, P_GM = 2816, 3072, 3328, 3584
P_WIDTH = 3840

W_A, W_C, W_M, W_G, W_B, W_WIDTH = 0, 768, 1280, 1536, 2560, 3072

C_HEAD_ORDER = (0, 2, 1, 3)


def _inproj_cols():
    r = np.arange
    aq, ak, av, ag = 0, 256, 512, 768
    bcq, bckv, bkr, bg = 1024, 1216, 1344, 1376
    cq, ck, cv, cg = 1632, 1888, 2016, 2144
    mq, mg = 2400, 2656
    cperm = np.concatenate([r(64) + 64 * h for h in C_HEAD_ORDER])
    pad = lambda n: np.full(n, -1)
    cols = np.concatenate([
        aq + r(256), ak + r(256), av + r(256),
        cq + cperm, ck + r(128), cv + r(128),
        mq + r(256),
        ag + r(256), bg + r(256), cg + cperm, mg + r(256),
        bcq + r(192), pad(64), bckv + r(128), bkr + r(32), bkr + r(32), pad(64),
    ])
    assert cols.shape[0] == W_WIDTH
    return cols, cperm


def _take_cols(w, cols, axis=-1):
    axis = axis % w.ndim
    pieces, i = [], 0
    while i < len(cols):
        j = i + 1
        if cols[i] < 0:
            while j < len(cols) and cols[j] < 0:
                j += 1
            shape = w.shape[:axis] + (j - i,) + w.shape[axis + 1:]
            pieces.append(jnp.zeros(shape, w.dtype))
        else:
            while j < len(cols) and cols[j] == cols[j - 1] + 1:
                j += 1
            pieces.append(lax.slice_in_dim(w, int(cols[i]), int(cols[i]) + (j - i), axis=axis))
        i = j
    return jnp.concatenate(pieces, axis=axis)


def _uq_cols():
    r = np.arange
    per = MLA_NOPE + MLA_ROPE
    out = []
    for p in range(2):
        h0, h1 = 2 * p, 2 * p + 1
        out += [per * h0 + r(64), per * h1 + r(64),
                per * h0 + 64 + r(32), per * h1 + 64 + r(32), np.full(64, -1)]
    return np.concatenate(out)


def _ukv_cols():
    r = np.arange
    return np.concatenate([128 * h + r(64) for h in range(4)] + [128 * h + 64 + r(64) for h in range(4)])


def _rope_pieces(positions):
    pos = positions.astype(F32).reshape(1, -1)
    half = HEAD_DIM // 2
    inv = ROPE_THETA ** (-jnp.arange(0, HEAD_DIM, 2, dtype=F32) / HEAD_DIM)
    ang = inv[:, None] * pos
    x = jnp.concatenate([jnp.cos(ang), jnp.sin(ang)], axis=0)
    expand = np.zeros((2 * half, 512), np.float32)
    lane = np.arange(128)
    for t, d in enumerate((HEAD_DIM, MLA_ROPE)):
        k = ((lane % d) % (d // 2)) * (HEAD_DIM // d)
        expand[k, 256 * t + lane] = 1.0
        expand[half + k, 256 * t + 128 + lane] = np.where(lane % d < d // 2, -1.0, 1.0)
    x = lax.optimization_barrier(x)
    hi = x.astype(BF16)
    rest = x - hi.astype(F32)
    mid = rest.astype(BF16)
    lo = (rest - mid.astype(F32)).astype(BF16)
    return jnp.concatenate([hi, mid, lo], axis=0), jnp.asarray(np.concatenate([expand] * 3, axis=0), BF16)


def _bias_table_a(rel_bias):
    width, period = 9 * 128, 9 * 128 + TQ + 1
    k = np.arange(period)
    d = np.where(k < width, A_PREV * CHUNK - k, A_PREV * CHUNK + period - k)
    idx = np.clip(d, -REL_CLIP, REL_CLIP) + REL_CLIP
    n_hi = A_PREV * CHUNK - REL_CLIP + 1
    n_lo = width - n_hi - (2 * REL_CLIP - 1)
    expect = np.concatenate([np.full(n_hi, 2 * REL_CLIP), np.arange(2 * REL_CLIP - 1, 0, -1),
                             np.zeros(n_lo, np.int64), np.full(period - width, 2 * REL_CLIP)])
    assert np.array_equal(idx, expect)
    rep = lambda col, n: jnp.broadcast_to(rel_bias[:, :, col:col + 1], rel_bias.shape[:2] + (n,))
    gp = jnp.concatenate([rep(2 * REL_CLIP, n_hi), jnp.flip(rel_bias[:, :, 1:2 * REL_CLIP], axis=-1),
                          rep(0, n_lo), rep(2 * REL_CLIP, period - width)], axis=-1) * LOG2E
    flat = jnp.tile(gp, (1, 1, TQ))[:, :, :TQ * (period - 1)]
    skew = flat.reshape(gp.shape[0], N_HEADS, TQ, period - 1)[..., :width]
    i = np.arange(TQ)[:, None]
    m = np.arange(width)[None, :]
    dchunk = i // CHUNK + A_PREV - m // CHUNK
    valid = (dchunk >= 0) & (dchunk <= A_PREV)
    t = jnp.where(jnp.asarray(valid)[None, None], skew, NEG_INF)
    return jnp.transpose(t, (0, 3, 1, 2)).reshape(gp.shape[0], 9, 128, N_HEADS * TQ)


def _mask_table_c():
    m = np.arange(3 * 128)[:, None]
    i = np.arange(TQ)[None, :]
    dchunk = i // CHUNK + SWA_PREV - m // CHUNK
    valid = (dchunk >= 0) & (dchunk <= SWA_PREV)
    t = np.where(valid, 0.0, NEG_INF).astype(np.float32)
    return np.tile(t, (1, N_HEADS)).reshape(3, 128, N_HEADS * TQ)


def _dot(a, b):
    return jnp.dot(a, b, preferred_element_type=F32)


def _dot_nt(a, b):
    return lax.dot_general(a, b, (((1,), (1,)), ((), ())), preferred_element_type=F32)


def _dot_tn(a, b):
    return lax.dot_general(a, b, (((0,), (0,)), ((), ())), preferred_element_type=F32)


def _rope(x, cos, sin_signed, half):
    lane = lax.broadcasted_iota(jnp.int32, x.shape, 1)
    first = (lane & (2 * half - 1)) < half
    swapped = jnp.where(first, pltpu.roll(x, 128 - half, 1), pltpu.roll(x, half, 1))
    return x * cos + swapped * sin_signed


def _project(xb, tab_ref, w_ref, wuq_ref, wukv_ref, gq_ref, gkv_ref, p_ref, rows=slice(None)):
    tab_ref, p_ref = tab_ref.at[rows], p_ref.at[rows]
    cos64, sin64 = tab_ref[:, 0:128], tab_ref[:, 128:256]
    cos32, sin32 = tab_ref[:, 256:384], tab_ref[:, 384:512]

    def mm(lo, hi):
        return _dot(xb, w_ref[:, lo:hi])

    rb = mm(W_B, W_B + 512)
    cq = rb[:, 0:256]
    ms = jnp.sum(cq * cq, axis=-1, keepdims=True) * (1.0 / MLA_Q_RANK)
    qn = (cq * lax.rsqrt(ms + 1e-6) * gq_ref[...]).astype(BF16)
    ckv = rb[:, 256:384]
    ms = jnp.mean(ckv * ckv, axis=-1, keepdims=True)
    kvn = (ckv * lax.rsqrt(ms + 1e-6) * gkv_ref[...]).astype(BF16)
    krb = _rope(rb[:, 384:512], cos32, sin32, 16).astype(BF16)

    r = mm(W_A, W_A + 768)
    p_ref[:, P_AQ:P_AQ + 256] = (r[:, 0:256] * QSCALE).astype(BF16)
    p_ref[:, P_AK:P_AK + 512] = r[:, 256:768].astype(BF16)

    q = _dot(qn, wuq_ref[...]) * B_QSCALE
    for p in range(2):
        base = P_BQ0 + 256 * p
        p_ref[:, base:base + 128] = q[:, 256 * p:256 * p + 128].astype(BF16)
        p_ref[:, base + 128:base + 256] = _rope(q[:, 256 * p + 128:256 * p + 256], cos32, sin32, 16).astype(BF16)
    kv = _dot(kvn, wukv_ref[...])
    for p in range(2):
        base = P_BK0 + 256 * p
        p_ref[:, base:base + 128] = kv[:, 128 * p:128 * (p + 1)].astype(BF16)
        p_ref[:, base + 128:base + 256] = krb
    p_ref[:, P_BV:P_BV + 256] = kv[:, 256:512].astype(BF16)

    r = mm(W_G, W_G + 1024)
    p_ref[:, P_GA:P_GA + 1024] = (r * (1.0 / (1.0 + jnp.exp(-r)))).astype(BF16)

    r = mm(W_C, W_C + 512)
    for j in range(2):
        qj = _rope(r[:, 128 * j:128 * (j + 1)], cos64, sin64, 32)
        p_ref[:, P_CQ + 128 * j:P_CQ + 128 * (j + 1)] = (qj * QSCALE).astype(BF16)
    p_ref[:, P_CK:P_CK + 128] = _rope(r[:, 256:384], cos64, sin64, 32).astype(BF16)
    p_ref[:, P_CV:P_CV + 128] = r[:, 384:512].astype(BF16)

    r = mm(W_M, W_M + 256)
    p_ref[:, P_MQ:P_MQ + 256] = (r * QSCALE).astype(BF16)


def _inproj_kernel(x_ref, pieces_ref, expand_ref, w_ref, wuq_ref, wukv_ref, gq_ref, gkv_ref, p_ref, tab_ref):
    tab_ref[...] = _dot_tn(pieces_ref[...], expand_ref[...])
    _project(x_ref[...].astype(BF16), tab_ref, w_ref, wuq_ref, wukv_ref, gq_ref, gkv_ref, p_ref)


def _inproj(x2d, pieces, expand3, w, wuq, wukv, gq, gkv):
    n = x2d.shape[0]
    const = lambda shape: pl.BlockSpec(shape, lambda i: (0,) * len(shape))
    return pl.pallas_call(
        _inproj_kernel,
        grid=(n // TM,),
        in_specs=[
            pl.BlockSpec((TM, D_MODEL), lambda i: (i, 0)),
            pl.BlockSpec((192, TM), lambda i: (0, i)),
            const((192, 512)),
            const((D_MODEL, W_WIDTH)),
            const((256, 512)),
            const((128, 512)),
            const((1, 256)),
            const((1, 128)),
        ],
        out_specs=[pl.BlockSpec((TM, P_WIDTH), lambda i: (i, 0)),
                   pl.BlockSpec((TM, 512), lambda i: (i, 0))],
        out_shape=[jax.ShapeDtypeStruct((n, P_WIDTH), BF16),
                   jax.ShapeDtypeStruct((n, 512), F32)],
        compiler_params=pltpu.CompilerParams(
            dimension_semantics=("parallel",), vmem_limit_bytes=VMEM_LIMIT),
        name="inproj",
    )(x2d, pieces, expand3, w, wuq, wukv, gq, gkv)


def _memkv_kernel(mem_ref, w_ref, o_ref):
    o_ref[0] = _dot(mem_ref[0].astype(BF16), w_ref[...]).astype(BF16)


def _memkv(mem, w_all):
    b = mem.shape[0]
    n = w_all.shape[1]
    return pl.pallas_call(
        _memkv_kernel,
        grid=(b,),
        in_specs=[pl.BlockSpec((1, MEM_LEN, D_MODEL), lambda i: (i, 0, 0)),
                  pl.BlockSpec((D_MODEL, n), lambda i: (0, 0))],
        out_specs=pl.BlockSpec((1, MEM_LEN, n), lambda i: (i, 0, 0)),
        out_shape=jax.ShapeDtypeStruct((b, MEM_LEN, n), BF16),
        compiler_params=pltpu.CompilerParams(
            dimension_semantics=("parallel",), vmem_limit_bytes=VMEM_LIMIT),
        name="memkv",
    )(mem, w_all)


def _window_attn_kernel(*refs, win, prev, dk, has_table, has_sink):
    refs = list(refs)
    sink_ref = refs.pop(0) if has_sink else None
    q_ref, k_ref, v_ref, g_ref = refs[:4]
    e_ref = refs[4] if has_table else None
    o_ref, sa_sc, sb_sc = refs[-3:]
    n_items = q_ref.shape[1] // TQ
    lanes = N_HEADS * TQ

    lane128 = lax.broadcasted_iota(jnp.int32, (TQ, 128), 1)
    lo, hi = lane128 < HEAD_DIM, lane128 >= HEAD_DIM
    if has_sink:
        col = lax.broadcasted_iota(jnp.int32, (1, lanes), 1)
        order = C_HEAD_ORDER if dk == 128 else tuple(range(N_HEADS))
        sink = jnp.where(col < TQ, sink_ref[order[0]],
                         jnp.where(col < 2 * TQ, sink_ref[order[1]],
                                   jnp.where(col < 3 * TQ, sink_ref[order[2]], sink_ref[order[3]]))) * LOG2E

    def window_start(item):
        if prev is None:
            return 0
        return pl.multiple_of(jnp.maximum(item * TQ - prev, 0), 128)

    def scores_into(item, s_sc):
        q = q_ref[0, pl.ds(pl.multiple_of(item * TQ, TQ), TQ), :].astype(F32)
        if dk == 256:
            zero = jnp.zeros((TQ, 128), F32)
            blocks = [jnp.concatenate([jnp.where(lo, q[:, 0:128], 0.0), zero], axis=1),
                      jnp.concatenate([jnp.where(hi, q[:, 0:128], 0.0), zero], axis=1),
                      jnp.concatenate([zero, jnp.where(lo, q[:, 128:256], 0.0)], axis=1),
                      jnp.concatenate([zero, jnp.where(hi, q[:, 128:256], 0.0)], axis=1)]
        else:
            blocks = [jnp.where(lo, q[:, 0:128], 0.0), jnp.where(hi, q[:, 0:128], 0.0),
                      jnp.where(lo, q[:, 128:256], 0.0), jnp.where(hi, q[:, 128:256], 0.0)]
        qs = jnp.concatenate(blocks, axis=0).astype(BF16)
        s = _dot_nt(k_ref[0, pl.ds(window_start(item), win), :], qs)
        if has_table:
            mb0 = jnp.maximum(prev // 128 - item, 0)
            s = s + jnp.concatenate([e_ref[mb0 + jb] for jb in range(win // 128)], axis=0)
        s_sc[...] = s

    def consume(item, s_sc):
        start = window_start(item)
        v = v_ref[0, pl.ds(start, win), :]
        parts = []
        for pr in range(2):
            cols = slice(256 * pr, 256 * (pr + 1))
            m = jnp.max(s_sc[:, cols], axis=0, keepdims=True)
            if has_sink:
                m = jnp.maximum(m, sink[:, cols])
            p = jnp.exp2(s_sc[:, cols] - m)
            l = jnp.sum(p, axis=0, keepdims=True)
            if has_sink:
                l = l + jnp.exp2(sink[:, cols] - m)
            inv = 1.0 / l
            vp = v[:, 128 * pr:128 * (pr + 1)] if dk == 256 else v
            ot = _dot_tn(vp, p.astype(BF16))
            for e in range(2):
                parts.append(ot[64 * e:64 * (e + 1), 128 * e:128 * (e + 1)] * inv[:, TQ * e:TQ * (e + 1)])
        o = jnp.concatenate(parts, axis=0).T
        rows = pl.ds(pl.multiple_of(item * TQ, TQ), TQ)
        o_ref[0, rows, :] = (o * g_ref[0, rows, :].astype(F32)).astype(BF16)

    bufs = (sa_sc, sb_sc)
    scores_into(0, bufs[0])

    def body(i, carry):
        for j in range(WIN_UNROLL):
            item = WIN_UNROLL * i + j
            scores_into(jnp.minimum(item + 1, n_items - 1), bufs[(j + 1) % 2])
            consume(item, bufs[j % 2])
        return carry

    lax.fori_loop(0, n_items // WIN_UNROLL, body, 0)


def _window_attn(q_src, q_col, k_src, k_col, v_src, v_col, g_col, *, win, prev, dk, table=None, sinks=None):
    b, s, _ = q_src.shape
    skv = k_src.shape[1]
    assert s % (WIN_UNROLL * TQ) == 0 and skv >= win
    kern = functools.partial(_window_attn_kernel, win=win, prev=prev, dk=dk,
                             has_table=table is not None, has_sink=sinks is not None)
    in_specs, args = [], []
    if sinks is not None:
        in_specs.append(pl.BlockSpec(memory_space=pltpu.SMEM)); args.append(sinks)
    in_specs += [pl.BlockSpec((1, s, 256), lambda i: (i, 0, q_col // 256)),
                 pl.BlockSpec((1, skv, dk), lambda i: (i, 0, k_col // dk)),
                 pl.BlockSpec((1, skv, dk), lambda i: (i, 0, v_col // dk)),
                 pl.BlockSpec((1, s, 256), lambda i: (i, 0, g_col // 256))]
    args += [q_src, k_src, v_src, q_src]
    if table is not None:
        in_specs.append(pl.BlockSpec(table.shape, lambda i: (0, 0, 0))); args.append(table)
    return pl.pallas_call(
        kern,
        grid=(b,),
        in_specs=in_specs,
        out_specs=pl.BlockSpec((1, s, 256), lambda i: (i, 0, 0)),
        out_shape=jax.ShapeDtypeStruct((b, s, GROUP), BF16),
        scratch_shapes=[pltpu.VMEM((win, N_HEADS * TQ), F32), pltpu.VMEM((win, N_HEADS * TQ), F32)],
        compiler_params=pltpu.CompilerParams(
            dimension_semantics=("parallel",), vmem_limit_bytes=VMEM_LIMIT),
        name="attn_win%d" % win,
    )(*args)


def _attn_b_kernel(q0_ref, q1_ref, k0_ref, k1_ref, v_ref, g_ref, o_ref,
                   qs_sc, vt_sc, sa_sc, sb_sc, m_sc, l_sc, acc_sc):
    t = pl.program_id(1)
    n_tiles = B_TQ // TQ
    lane = lax.broadcasted_iota(jnp.int32, (TQ, 128), 1)
    q_refs = (q0_ref, q1_ref)
    k_refs = (k0_ref, k1_ref)

    def stack_queries(c):
        for pr in range(2):
            q = q_refs[pr][0, TQ * c:TQ * (c + 1), :].astype(F32)
            nope, rope = q[:, 0:128], q[:, 128:256]
            head_a = jnp.concatenate([jnp.where(lane < 64, nope, 0.0), jnp.where(lane < 32, rope, 0.0)], axis=1)
            head_b = jnp.concatenate([jnp.where(lane >= 64, nope, 0.0), jnp.where(lane >= 32, rope, 0.0)], axis=1)
            qs_sc[2 * c + pr] = jnp.concatenate([head_a, head_b], axis=0).astype(BF16)

    lane2 = lax.broadcasted_iota(jnp.int32, (1, 2 * TQ), 1)
    hide_first_chunk = jnp.where((lane2 & (TQ - 1)) < CHUNK, NEG_INF, 0.0)

    def scores_into(kb, c, s_sc, nk=B_TK):
        start = pl.multiple_of(kb * B_TK, B_TK)
        for pr in range(2):
            s_sc[pr, 0:nk, :] = _dot_nt(k_refs[pr][0, pl.ds(start, nk), :], qs_sc[2 * c + pr])

    def transpose_values(kb, slot=0):
        vt_sc[slot] = v_ref[0, pl.ds(pl.multiple_of(kb * B_TK, B_TK), B_TK), :].T

    def consume(kb, c, s_sc, nk=B_TK, diagonal=False, slot=0):
        for pr in range(2):
            u = 2 * c + pr
            s = s_sc[pr, 0:nk, :]
            if diagonal:
                s = jnp.concatenate([s[:nk - CHUNK], s[nk - CHUNK:] + hide_first_chunk], axis=0)
            m_prev = m_sc[u]
            m_new = jnp.maximum(m_prev, jnp.max(s, axis=0, keepdims=True))
            alpha = jnp.exp2(m_prev - m_new)
            p = jnp.exp2(s - m_new)
            l_sc[u] = alpha * l_sc[u] + jnp.sum(p, axis=0, keepdims=True)
            m_sc[u] = m_new
            acc_sc[u] = alpha * acc_sc[u] + _dot(vt_sc[slot, 128 * pr:128 * (pr + 1), 0:nk], p.astype(BF16))

    bufs = (sa_sc, sb_sc)
    for c in range(n_tiles):
        stack_queries(c)
    scores_into(0, 0, bufs[0])
    m_sc[...] = jnp.full(m_sc.shape, NEG_INF, F32)
    l_sc[...] = jnp.zeros(l_sc.shape, F32)
    acc_sc[...] = jnp.zeros(acc_sc.shape, F32)

    def past_block(kb, slot):
        for c in range(n_tiles):
            if c + 1 < n_tiles:
                scores_into(kb, c + 1, bufs[(c + 1) % 2])
            else:
                scores_into(kb + 1, 0, bufs[0])
            consume(kb, c, bufs[c % 2], slot=slot)

    def past_blocks(n):
        def body(i, carry):
            for j in range(n):
                transpose_values(n * i + j, j)
            for j in range(n):
                past_block(n * i + j, j)
            return carry
        return body

    fours, twos = t >> 2, (t >> 1) & 1
    lax.fori_loop(0, fours, past_blocks(4), 0)
    lax.fori_loop(2 * fours, 2 * fours + twos, past_blocks(2), 0)
    lax.fori_loop(t & ~1, t, past_blocks(1), 0)

    transpose_values(t)
    for c in range(n_tiles):
        if c + 1 < n_tiles:
            scores_into(t, c + 1, bufs[(c + 1) % 2], nk=TQ * (c + 2))
        consume(t, c, bufs[c % 2], nk=TQ * (c + 1), diagonal=True)
        parts = []
        for pr in range(2):
            u = 2 * c + pr
            inv = 1.0 / l_sc[u]
            for e in range(2):
                parts.append(acc_sc[u, 64 * e:64 * (e + 1), 128 * e:128 * (e + 1)] * inv[:, 128 * e:128 * (e + 1)])
        o = jnp.concatenate(parts, axis=0).T
        o_ref[0, TQ * c:TQ * (c + 1), :] = (o * g_ref[0, TQ * c:TQ * (c + 1), :].astype(F32)).astype(BF16)


def _attn_b(p3):
    b, s, _ = p3.shape
    n_units = 2 * (B_TQ // TQ)
    return pl.pallas_call(
        _attn_b_kernel,
        grid=(b, s // B_TQ),
        in_specs=[
            pl.BlockSpec((1, B_TQ, 256), lambda i, t: (i, t, P_BQ0 // 256)),
            pl.BlockSpec((1, B_TQ, 256), lambda i, t: (i, t, P_BQ1 // 256)),
            pl.BlockSpec((1, s, 256), lambda i, t: (i, 0, P_BK0 // 256)),
            pl.BlockSpec((1, s, 256), lambda i, t: (i, 0, P_BK1 // 256)),
            pl.BlockSpec((1, s, 256), lambda i, t: (i, 0, P_BV // 256)),
            pl.BlockSpec((1, B_TQ, 256), lambda i, t: (i, t, P_GB // 256)),
        ],
        out_specs=pl.BlockSpec((1, B_TQ, 256), lambda i, t: (i, t, 0)),
        out_shape=jax.ShapeDtypeStruct((b, s, GROUP), BF16),
        scratch_shapes=[pltpu.VMEM((n_units, 2 * TQ, 256), BF16),
                        pltpu.VMEM((4, GROUP, B_TK), BF16),
                        pltpu.VMEM((2, B_TK, 2 * TQ), F32),
                        pltpu.VMEM((2, B_TK, 2 * TQ), F32),
                        pltpu.VMEM((n_units, 1, 2 * TQ), F32),
                        pltpu.VMEM((n_units, 1, 2 * TQ), F32),
                        pltpu.VMEM((n_units, 128, 2 * TQ), F32)],
        compiler_params=pltpu.CompilerParams(
            dimension_semantics=("parallel", "arbitrary"), vmem_limit_bytes=VMEM_LIMIT),
        name="attn_b",
    )(p3, p3, p3, p3, p3, p3)


def _residual_norm(ya_ref, yb_ref, yc_ref, ym_ref, x_ref, w_ref, g_ref, b_ref, rows=slice(None)):
    y = (_dot(ya_ref[rows, :], w_ref[0:256, :]) + _dot(yb_ref[rows, :], w_ref[256:512, :])
         + _dot(yc_ref[rows, :], w_ref[512:768, :]) + _dot(ym_ref[rows, :], w_ref[768:1024, :]))
    z = ALPHA * x_ref[rows, :] + y
    mu = jnp.mean(z, axis=-1, keepdims=True)
    zc = z - mu
    var = jnp.mean(zc * zc, axis=-1, keepdims=True)
    return zc * lax.rsqrt(var + 1e-5) * g_ref[...] + b_ref[...]


def _outproj_kernel(ya_ref, yb_ref, yc_ref, ym_ref, x_ref, w_ref, g_ref, b_ref, o_ref):
    for h in range(TM_OUT // 256):
        rows = pl.ds(h * 256, 256)
        o_ref[rows, :] = _residual_norm(ya_ref, yb_ref, yc_ref, ym_ref, x_ref, w_ref, g_ref, b_ref, rows)


def _out_in_proj_kernel(ya_ref, yb_ref, yc_ref, ym_ref, x_ref, wo_ref, g_ref, b_ref,
                        tab_ref, w_ref, wuq_ref, wukv_ref, gq_ref, gkv_ref, o_ref, p_ref):
    halves = [pl.ds(h * (TM // 2), TM // 2) for h in range(2)]
    xb = []
    for rows in halves:
        xn = _residual_norm(ya_ref, yb_ref, yc_ref, ym_ref, x_ref, wo_ref, g_ref, b_ref, rows)
        o_ref[rows, :] = xn
        xb.append(xn.astype(BF16))
    for rows, x in zip(halves, xb):
        _project(x, tab_ref, w_ref, wuq_ref, wukv_ref, gq_ref, gkv_ref, p_ref, rows)


def _outproj(ya, yb, yc, ym, x2d, w, g, bias):
    n = x2d.shape[0]
    ytile = pl.BlockSpec((TM_OUT, GROUP), lambda i: (i, 0))
    const = lambda shape: pl.BlockSpec(shape, lambda i: (0,) * len(shape))
    return pl.pallas_call(
        _outproj_kernel,
        grid=(n // TM_OUT,),
        in_specs=[ytile, ytile, ytile, ytile,
                  pl.BlockSpec((TM_OUT, D_MODEL), lambda i: (i, 0)),
                  const((D_MODEL, D_MODEL)), const((1, D_MODEL)), const((1, D_MODEL))],
        out_specs=pl.BlockSpec((TM_OUT, D_MODEL), lambda i: (i, 0)),
        out_shape=jax.ShapeDtypeStruct((n, D_MODEL), F32),
        compiler_params=pltpu.CompilerParams(
            dimension_semantics=("parallel",), vmem_limit_bytes=VMEM_LIMIT),
        name="outproj",
    )(ya, yb, yc, ym, x2d, w, g, bias)


def _out_in_proj(ya, yb, yc, ym, x2d, wo, g, bias, tab, w, wuq, wukv, gq, gkv):
    n = x2d.shape[0]
    ytile = pl.BlockSpec((TM, GROUP), lambda i: (i, 0))
    const = lambda shape: pl.BlockSpec(shape, lambda i: (0,) * len(shape))
    return pl.pallas_call(
        _out_in_proj_kernel,
        grid=(n // TM,),
        in_specs=[ytile, ytile, ytile, ytile,
                  pl.BlockSpec((TM, D_MODEL), lambda i: (i, 0)),
                  const((D_MODEL, D_MODEL)), const((1, D_MODEL)), const((1, D_MODEL)),
                  pl.BlockSpec((TM, 512), lambda i: (i, 0)),
                  const((D_MODEL, W_WIDTH)), const((256, 512)), const((128, 512)),
                  const((1, 256)), const((1, 128))],
        out_specs=[pl.BlockSpec((TM, D_MODEL), lambda i: (i, 0)),
                   pl.BlockSpec((TM, P_WIDTH), lambda i: (i, 0))],
        out_shape=[jax.ShapeDtypeStruct((n, D_MODEL), F32),
                   jax.ShapeDtypeStruct((n, P_WIDTH), BF16)],
        compiler_params=pltpu.CompilerParams(
            dimension_semantics=("parallel",), vmem_limit_bytes=VMEM_LIMIT),
        name="out_in_proj",
    )(ya, yb, yc, ym, x2d, wo, g, bias, tab, w, wuq, wukv, gq, gkv)


def kernel(x, mem, positions, w_in, rel_bias, mla_q_norm, w_uq, mla_kv_norm, w_ukv,
           swa_sinks, w_mem_kv, w_out, ln_gain, ln_bias):
    b, s, d = x.shape
    depth = w_in.shape[0]
    assert d == D_MODEL and depth == DEPTH and s % B_TK == 0 and s >= A_WIN and (b * s) % TM_OUT == 0 and TM_OUT % TM == 0

    cols, cperm = _inproj_cols()
    w_in_p = _take_cols(w_in, cols).astype(BF16)
    wuq_p = jnp.pad(_take_cols(w_uq, _uq_cols()), ((0, 0), (0, 256 - MLA_Q_RANK), (0, 0))).astype(BF16)
    wukv_p = _take_cols(w_ukv, _ukv_cols()).astype(BF16)
    gq = jnp.pad(mla_q_norm, ((0, 0), (0, 256 - MLA_Q_RANK)))[:, None, :]
    gkv = mla_kv_norm[:, None, :]
    rows = np.concatenate([np.arange(512), 512 + cperm, np.arange(768, 1024)])
    w_out_p = _take_cols(w_out, rows, axis=1).astype(BF16)
    w_mem_all = jnp.transpose(w_mem_kv, (1, 0, 2)).reshape(D_MODEL, depth * 512).astype(BF16)
    e_a = _bias_table_a(rel_bias)
    e_c = jnp.asarray(_mask_table_c())
    pieces, expand3 = _rope_pieces(positions)

    memkv = _memkv(mem, w_mem_all)
    h = x.reshape(b * s, d)
    p2, tab = _inproj(h, pieces, expand3, w_in_p[0], wuq_p[0], wukv_p[0], gq[0], gkv[0])
    for l in range(depth):
        p3 = p2.reshape(b, s, P_WIDTH)
        ya = _window_attn(p3, P_AQ, p3, P_AK, p3, P_AV, P_GA, win=A_WIN, prev=A_PREV * CHUNK, dk=256, table=e_a[l])
        yb = _attn_b(p3)
        yc = _window_attn(p3, P_CQ, p3, P_CK, p3, P_CV, P_GC, win=C_WIN, prev=SWA_PREV * CHUNK, dk=128,
                          table=e_c, sinks=swa_sinks[l])
        ym = _window_attn(p3, P_MQ, memkv, 512 * l, memkv, 512 * l + 256, P_GM, win=MEM_LEN, prev=None, dk=256)
        ys = [y.reshape(b * s, GROUP) for y in (ya, yb, yc, ym)]
        ln = (ln_gain[l][None, :], ln_bias[l][None, :])
        if l + 1 < depth:
            h, p2 = _out_in_proj(*ys, h, w_out_p[l], *ln, tab, w_in_p[l + 1], wuq_p[l + 1], wukv_p[l + 1],
                                 gq[l + 1], gkv[l + 1])
        else:
            h = _outproj(*ys, h, w_out_p[l], *ln)
    return h.reshape(b, s, d)
```

```python
import functools

import numpy as np
import jax
import jax.numpy as jnp
from jax import lax
from jax.experimental import pallas as pl
from jax.experimental.pallas import tpu as pltpu

F32 = jnp.float32
BF16 = jnp.bfloat16

D_MODEL = 1024
DEPTH = 4
CHUNK = 64
HEAD_DIM = 64
GROUP = 256
N_HEADS = 4
ROPE_THETA = 10000.0
NEG_INF = -1e30
A_PREV = 8
REL_CLIP = 128
MLA_NOPE = 64
MLA_ROPE = 32
MLA_Q_RANK = 192
MLA_KV_RANK = 128
SWA_PREV = 2
MEM_LEN = 256
ALPHA = (2.0 * DEPTH) ** 0.25

TQ = 128
A_WIN = TQ + A_PREV * CHUNK
C_WIN = TQ + SWA_PREV * CHUNK
WIN_UNROLL = {640: 16, 256: 32}
B_TK = 512
B_TQ = 512
TM = 512
TM_OUT = 1024
VMEM_LIMIT = 56 * 1024 * 1024
LOG2E = 1.4426950408889634
QSCALE = HEAD_DIM ** -0.5 * LOG2E
B_QSCALE = (MLA_NOPE + MLA_ROPE) ** -0.5 * LOG2E

P_AQ, P_AK, P_AV = 0, 256, 512
P_BQ0, P_BQ1, P_BK0, P_BK1, P_BV = 768, 1024, 1280, 1536, 1792
P_CQ, P_CK, P_CV = 2048, 2304, 2432
P_MQ = 2560
P_GA, P_GB, P_GC, P_GM = 2816, 3072, 3328, 3584
P_WIDTH = 3840

W_A, W_C, W_M, W_G, W_B, W_WIDTH = 0, 768, 1280, 1536, 2560, 3072

C_HEAD_ORDER = (0, 2, 1, 3)


def _inproj_cols():
    r = np.arange
    aq, ak, av, ag = 0, 256, 512, 768
    bcq, bckv, bkr, bg = 1024, 1216, 1344, 1376
    cq, ck, cv, cg = 1632, 1888, 2016, 2144
    mq, mg = 2400, 2656
    cperm = np.concatenate([r(64) + 64 * h for h in C_HEAD_ORDER])
    pad = lambda n: np.full(n, -1)
    cols = np.concatenate([
        aq + r(256), ak + r(256), av + r(256),
        cq + cperm, ck + r(128), cv + r(128),
        mq + r(256),
        ag + r(256), bg + r(256), cg + cperm, mg + r(256),
        bcq + r(192), pad(64), bckv + r(128), bkr + r(32), bkr + r(32), pad(64),
    ])
    assert cols.shape[0] == W_WIDTH
    return cols, cperm


def _take_cols(w, cols, axis=-1):
    axis = axis % w.ndim
    pieces, i = [], 0
    while i < len(cols):
        j = i + 1
        if cols[i] < 0:
            while j < len(cols) and cols[j] < 0:
                j += 1
            shape = w.shape[:axis] + (j - i,) + w.shape[axis + 1:]
            pieces.append(jnp.zeros(shape, w.dtype))
        else:
            while j < len(cols) and cols[j] == cols[j - 1] + 1:
                j += 1
            pieces.append(lax.slice_in_dim(w, int(cols[i]), int(cols[i]) + (j - i), axis=axis))
        i = j
    return jnp.concatenate(pieces, axis=axis)


def _uq_cols():
    r = np.arange
    per = MLA_NOPE + MLA_ROPE
    out = []
    for p in range(2):
        h0, h1 = 2 * p, 2 * p + 1
        out += [per * h0 + r(64), per * h1 + r(64),
                per * h0 + 64 + r(32), per * h1 + 64 + r(32), np.full(64, -1)]
    return np.concatenate(out)


def _ukv_cols():
    r = np.arange
    return np.concatenate([128 * h + r(64) for h in range(4)] + [128 * h + 64 + r(64) for h in range(4)])


def _rope_pieces(positions):
    pos = positions.astype(F32).reshape(1, -1)
    half = HEAD_DIM // 2
    inv = ROPE_THETA ** (-jnp.arange(0, HEAD_DIM, 2, dtype=F32) / HEAD_DIM)
    ang = inv[:, None] * pos
    x = jnp.concatenate([jnp.cos(ang), jnp.sin(ang)], axis=0)
    expand = np.zeros((2 * half, 512), np.float32)
    lane = np.arange(128)
    for t, d in enumerate((HEAD_DIM, MLA_ROPE)):
        k = ((lane % d) % (d // 2)) * (HEAD_DIM // d)
        expand[k, 256 * t + lane] = 1.0
        expand[half + k, 256 * t + 128 + lane] = np.where(lane % d < d // 2, -1.0, 1.0)
    x = lax.optimization_barrier(x)
    hi = x.astype(BF16)
    rest = x - hi.astype(F32)
    mid = rest.astype(BF16)
    lo = (rest - mid.astype(F32)).astype(BF16)
    return jnp.concatenate([hi, mid, lo], axis=0), jnp.asarray(np.concatenate([expand] * 3, axis=0), BF16)


def _bias_table_a(rel_bias):
    width, period = 9 * 128, 9 * 128 + TQ + 1
    k = np.arange(period)
    d = np.where(k < width, A_PREV * CHUNK - k, A_PREV * CHUNK + period - k)
    idx = np.clip(d, -REL_CLIP, REL_CLIP) + REL_CLIP
    n_hi = A_PREV * CHUNK - REL_CLIP + 1
    n_lo = width - n_hi - (2 * REL_CLIP - 1)
    expect = np.concatenate([np.full(n_hi, 2 * REL_CLIP), np.arange(2 * REL_CLIP - 1, 0, -1),
                             np.zeros(n_lo, np.int64), np.full(period - width, 2 * REL_CLIP)])
    assert np.array_equal(idx, expect)
    rep = lambda col, n: jnp.broadcast_to(rel_bias[:, :, col:col + 1], rel_bias.shape[:2] + (n,))
    gp = jnp.concatenate([rep(2 * REL_CLIP, n_hi), jnp.flip(rel_bias[:, :, 1:2 * REL_CLIP], axis=-1),
                          rep(0, n_lo), rep(2 * REL_CLIP, period - width)], axis=-1) * LOG2E
    flat = jnp.tile(gp, (1, 1, TQ))[:, :, :TQ * (period - 1)]
    skew = flat.reshape(gp.shape[0], N_HEADS, TQ, period - 1)[..., :width]
    i = np.arange(TQ)[:, None]
    m = np.arange(width)[None, :]
    dchunk = i // CHUNK + A_PREV - m // CHUNK
    valid = (dchunk >= 0) & (dchunk <= A_PREV)
    t = jnp.where(jnp.asarray(valid)[None, None], skew, NEG_INF)
    return jnp.transpose(t, (0, 3, 1, 2)).reshape(gp.shape[0], 9, 128, N_HEADS * TQ)


def _mask_table_c():
    m = np.arange(3 * 128)[:, None]
    i = np.arange(TQ)[None, :]
    dchunk = i // CHUNK + SWA_PREV - m // CHUNK
    valid = (dchunk >= 0) & (dchunk <= SWA_PREV)
    t = np.where(valid, 0.0, NEG_INF).astype(np.float32)
    return np.tile(t, (1, N_HEADS)).reshape(3, 128, N_HEADS * TQ)


def _dot(a, b):
    return jnp.dot(a, b, preferred_element_type=F32)


def _dot_nt(a, b):
    return lax.dot_general(a, b, (((1,), (1,)), ((), ())), preferred_element_type=F32)


def _dot_tn(a, b):
    return lax.dot_general(a, b, (((0,), (0,)), ((), ())), preferred_element_type=F32)


def _rope(x, cos, sin_signed, half):
    lane = lax.broadcasted_iota(jnp.int32, x.shape, 1)
    first = (lane & (2 * half - 1)) < half
    swapped = jnp.where(first, pltpu.roll(x, 128 - half, 1), pltpu.roll(x, half, 1))
    return x * cos + swapped * sin_signed


def _project(xb, tab_ref, w_ref, wuq_ref, wukv_ref, gq_ref, gkv_ref, p_ref, rows=slice(None)):
    tab_ref, p_ref = tab_ref.at[rows], p_ref.at[rows]
    cos64, sin64 = tab_ref[:, 0:128], tab_ref[:, 128:256]
    cos32, sin32 = tab_ref[:, 256:384], tab_ref[:, 384:512]

    def mm(lo, hi):
        return _dot(xb, w_ref[:, lo:hi])

    rb = mm(W_B, W_B + 512)
    cq = rb[:, 0:256]
    ms = jnp.sum(cq * cq, axis=-1, keepdims=True) * (1.0 / MLA_Q_RANK)
    qn = (cq * lax.rsqrt(ms + 1e-6) * gq_ref[...]).astype(BF16)
    ckv = rb[:, 256:384]
    ms = jnp.mean(ckv * ckv, axis=-1, keepdims=True)
    kvn = (ckv * lax.rsqrt(ms + 1e-6) * gkv_ref[...]).astype(BF16)
    krb = _rope(rb[:, 384:512], cos32, sin32, 16).astype(BF16)

    r = mm(W_A, W_A + 768)
    p_ref[:, P_AQ:P_AQ + 256] = (r[:, 0:256] * QSCALE).astype(BF16)
    p_ref[:, P_AK:P_AK + 512] = r[:, 256:768].astype(BF16)

    q = _dot(qn, wuq_ref[...]) * B_QSCALE
    for p in range(2):
        base = P_BQ0 + 256 * p
        p_ref[:, base:base + 128] = q[:, 256 * p:256 * p + 128].astype(BF16)
        p_ref[:, base + 128:base + 256] = _rope(q[:, 256 * p + 128:256 * p + 256], cos32, sin32, 16).astype(BF16)
    kv = _dot(kvn, wukv_ref[...])
    for p in range(2):
        base = P_BK0 + 256 * p
        p_ref[:, base:base + 128] = kv[:, 128 * p:128 * (p + 1)].astype(BF16)
        p_ref[:, base + 128:base + 256] = krb
    p_ref[:, P_BV:P_BV + 256] = kv[:, 256:512].astype(BF16)

    r = mm(W_G, W_G + 1024)
    p_ref[:, P_GA:P_GA + 1024] = (r * (1.0 / (1.0 + jnp.exp(-r)))).astype(BF16)

    r = mm(W_C, W_C + 512)
    for j in range(2):
        qj = _rope(r[:, 128 * j:128 * (j + 1)], cos64, sin64, 32)
        p_ref[:, P_CQ + 128 * j:P_CQ + 128 * (j + 1)] = (qj * QSCALE).astype(BF16)
    p_ref[:, P_CK:P_CK + 128] = _rope(r[:, 256:384], cos64, sin64, 32).astype(BF16)
    p_ref[:, P_CV:P_CV + 128] = r[:, 384:512].astype(BF16)

    r = mm(W_M, W_M + 256)
    p_ref[:, P_MQ:P_MQ + 256] = (r * QSCALE).astype(BF16)


def _inproj_kernel(x_ref, pieces_ref, expand_ref, w_ref, wuq_ref, wukv_ref, gq_ref, gkv_ref, p_ref, tab_ref):
    tab_ref[...] = _dot_tn(pieces_ref[...], expand_ref[...])
    _project(x_ref[...].astype(BF16), tab_ref, w_ref, wuq_ref, wukv_ref, gq_ref, gkv_ref, p_ref)


def _inproj(x2d, pieces, expand3, w, wuq, wukv, gq, gkv):
    n = x2d.shape[0]
    const = lambda shape: pl.BlockSpec(shape, lambda i: (0,) * len(shape))
    return pl.pallas_call(
        _inproj_kernel,
        grid=(n // TM,),
        in_specs=[
            pl.BlockSpec((TM, D_MODEL), lambda i: (i, 0)),
            pl.BlockSpec((192, TM), lambda i: (0, i)),
            const((192, 512)),
            const((D_MODEL, W_WIDTH)),
            const((256, 512)),
            const((128, 512)),
            const((1, 256)),
            const((1, 128)),
        ],
        out_specs=[pl.BlockSpec((TM, P_WIDTH), lambda i: (i, 0)),
                   pl.BlockSpec((TM, 512), lambda i: (i, 0))],
        out_shape=[jax.ShapeDtypeStruct((n, P_WIDTH), BF16),
                   jax.ShapeDtypeStruct((n, 512), F32)],
        compiler_params=pltpu.CompilerParams(
            dimension_semantics=("parallel",), vmem_limit_bytes=VMEM_LIMIT),
        name="inproj",
    )(x2d, pieces, expand3, w, wuq, wukv, gq, gkv)


def _memkv_kernel(mem_ref, w_ref, o_ref):
    o_ref[0] = _dot(mem_ref[0].astype(BF16), w_ref[...]).astype(BF16)


def _memkv(mem, w_all):
    b = mem.shape[0]
    n = w_all.shape[1]
    return pl.pallas_call(
        _memkv_kernel,
        grid=(b,),
        in_specs=[pl.BlockSpec((1, MEM_LEN, D_MODEL), lambda i: (i, 0, 0)),
                  pl.BlockSpec((D_MODEL, n), lambda i: (0, 0))],
        out_specs=pl.BlockSpec((1, MEM_LEN, n), lambda i: (i, 0, 0)),
        out_shape=jax.ShapeDtypeStruct((b, MEM_LEN, n), BF16),
        compiler_params=pltpu.CompilerParams(
            dimension_semantics=("parallel",), vmem_limit_bytes=VMEM_LIMIT),
        name="memkv",
    )(mem, w_all)


def _window_attn_kernel(*refs, win, prev, dk, has_table, has_sink, unroll):
    refs = list(refs)
    sink_ref = refs.pop(0) if has_sink else None
    q_ref, k_ref, v_ref, g_ref = refs[:4]
    e_ref = refs[4] if has_table else None
    o_ref, sa_sc, sb_sc = refs[-3:]
    n_items = q_ref.shape[1] // TQ
    lanes = N_HEADS * TQ

    lane128 = lax.broadcasted_iota(jnp.int32, (TQ, 128), 1)
    lo, hi = lane128 < HEAD_DIM, lane128 >= HEAD_DIM
    if has_sink:
        col = lax.broadcasted_iota(jnp.int32, (1, lanes), 1)
        order = C_HEAD_ORDER if dk == 128 else tuple(range(N_HEADS))
        sink = jnp.where(col < TQ, sink_ref[order[0]],
                         jnp.where(col < 2 * TQ, sink_ref[order[1]],
                                   jnp.where(col < 3 * TQ, sink_ref[order[2]], sink_ref[order[3]]))) * LOG2E

    def window_start(item):
        if prev is None:
            return 0
        return pl.multiple_of(jnp.maximum(item * TQ - prev, 0), 128)

    def scores_into(item, s_sc):
        q = q_ref[0, pl.ds(pl.multiple_of(item * TQ, TQ), TQ), :].astype(F32)
        if dk == 256:
            zero = jnp.zeros((TQ, 128), F32)
            blocks = [jnp.concatenate([jnp.where(lo, q[:, 0:128], 0.0), zero], axis=1),
                      jnp.concatenate([jnp.where(hi, q[:, 0:128], 0.0), zero], axis=1),
                      jnp.concatenate([zero, jnp.where(lo, q[:, 128:256], 0.0)], axis=1),
                      jnp.concatenate([zero, jnp.where(hi, q[:, 128:256], 0.0)], axis=1)]
        else:
            blocks = [jnp.where(lo, q[:, 0:128], 0.0), jnp.where(hi, q[:, 0:128], 0.0),
                      jnp.where(lo, q[:, 128:256], 0.0), jnp.where(hi, q[:, 128:256], 0.0)]
        qs = jnp.concatenate(blocks, axis=0).astype(BF16)
        s = _dot_nt(k_ref[0, pl.ds(window_start(item), win), :], qs)
        if has_table:
            mb0 = jnp.maximum(prev // 128 - item, 0)
            s = s + jnp.concatenate([e_ref[mb0 + jb] for jb in range(win // 128)], axis=0)
        s_sc[...] = s

    def consume(item, s_sc):
        start = window_start(item)
        v = v_ref[0, pl.ds(start, win), :]
        parts = []
        for pr in range(2):
            cols = slice(256 * pr, 256 * (pr + 1))
            m = jnp.max(s_sc[:, cols], axis=0, keepdims=True)
            if has_sink:
                m = jnp.maximum(m, sink[:, cols])
            p = jnp.exp2(s_sc[:, cols] - m)
            l = jnp.sum(p, axis=0, keepdims=True)
            if has_sink:
                l = l + jnp.exp2(sink[:, cols] - m)
            inv = 1.0 / l
            vp = v[:, 128 * pr:128 * (pr + 1)] if dk == 256 else v
            ot = _dot_tn(vp, p.astype(BF16))
            for e in range(2):
                parts.append(ot[64 * e:64 * (e + 1), 128 * e:128 * (e + 1)] * inv[:, TQ * e:TQ * (e + 1)])
        o = jnp.concatenate(parts, axis=0).T
        rows = pl.ds(pl.multiple_of(item * TQ, TQ), TQ)
        o_ref[0, rows, :] = (o * g_ref[0, rows, :].astype(F32)).astype(BF16)

    bufs = (sa_sc, sb_sc)
    scores_into(0, bufs[0])

    def body(i, carry):
        for j in range(unroll):
            item = unroll * i + j
            scores_into(jnp.minimum(item + 1, n_items - 1), bufs[(j + 1) % 2])
            consume(item, bufs[j % 2])
        return carry

    lax.fori_loop(0, n_items // unroll, body, 0)


def _window_attn(q_src, q_col, k_src, k_col, v_src, v_col, g_col, *, win, prev, dk, table=None, sinks=None):
    b, s, _ = q_src.shape
    skv = k_src.shape[1]
    unroll = min(WIN_UNROLL[win], s // TQ)
    assert s % (unroll * TQ) == 0 and unroll % 2 == 0 and skv >= win
    kern = functools.partial(_window_attn_kernel, win=win, prev=prev, dk=dk,
                             has_table=table is not None, has_sink=sinks is not None, unroll=unroll)
    in_specs, args = [], []
    if sinks is not None:
        in_specs.append(pl.BlockSpec(memory_space=pltpu.SMEM)); args.append(sinks)
    in_specs += [pl.BlockSpec((1, s, 256), lambda i: (i, 0, q_col // 256)),
                 pl.BlockSpec((1, skv, dk), lambda i: (i, 0, k_col // dk)),
                 pl.BlockSpec((1, skv, dk), lambda i: (i, 0, v_col // dk)),
                 pl.BlockSpec((1, s, 256), lambda i: (i, 0, g_col // 256))]
    args += [q_src, k_src, v_src, q_src]
    if table is not None:
        in_specs.append(pl.BlockSpec(table.shape, lambda i: (0, 0, 0))); args.append(table)
    return pl.pallas_call(
        kern,
        grid=(b,),
        in_specs=in_specs,
        out_specs=pl.BlockSpec((1, s, 256), lambda i: (i, 0, 0)),
        out_shape=jax.ShapeDtypeStruct((b, s, GROUP), BF16),
        scratch_shapes=[pltpu.VMEM((win, N_HEADS * TQ), F32), pltpu.VMEM((win, N_HEADS * TQ), F32)],
        compiler_params=pltpu.CompilerParams(
            dimension_semantics=("parallel",), vmem_limit_bytes=VMEM_LIMIT),
        name="attn_win%d" % win,
    )(*args)


def _attn_b_kernel(q0_ref, q1_ref, k0_ref, k1_ref, v_ref, g_ref, o_ref,
                   qs_sc, vt_sc, sa_sc, sb_sc, m_sc, l_sc, acc_sc):
    t = pl.program_id(1)
    n_tiles = B_TQ // TQ
    lane = lax.broadcasted_iota(jnp.int32, (TQ, 128), 1)
    q_refs = (q0_ref, q1_ref)
    k_refs = (k0_ref, k1_ref)

    def stack_queries(c):
        for pr in range(2):
            q = q_refs[pr][0, TQ * c:TQ * (c + 1), :].astype(F32)
            nope, rope = q[:, 0:128], q[:, 128:256]
            head_a = jnp.concatenate([jnp.where(lane < 64, nope, 0.0), jnp.where(lane < 32, rope, 0.0)], axis=1)
            head_b = jnp.concatenate([jnp.where(lane >= 64, nope, 0.0), jnp.where(lane >= 32, rope, 0.0)], axis=1)
            qs_sc[2 * c + pr] = jnp.concatenate([head_a, head_b], axis=0).astype(BF16)

    lane2 = lax.broadcasted_iota(jnp.int32, (1, 2 * TQ), 1)
    hide_first_chunk = jnp.where((lane2 & (TQ - 1)) < CHUNK, NEG_INF, 0.0)

    def scores_into(kb, c, s_sc, nk=B_TK):
        start = pl.multiple_of(kb * B_TK, B_TK)
        for pr in range(2):
            s_sc[pr, 0:nk, :] = _dot_nt(k_refs[pr][0, pl.ds(start, nk), :], qs_sc[2 * c + pr])

    def transpose_values(kb, slot=0):
        vt_sc[slot] = v_ref[0, pl.ds(pl.multiple_of(kb * B_TK, B_TK), B_TK), :].T

    def consume(kb, c, s_sc, nk=B_TK, diagonal=False, slot=0):
        for pr in range(2):
            u = 2 * c + pr
            s = s_sc[pr, 0:nk, :]
            if diagonal:
                s = jnp.concatenate([s[:nk - CHUNK], s[nk - CHUNK:] + hide_first_chunk], axis=0)
            m_prev = m_sc[u]
            m_new = jnp.maximum(m_prev, jnp.max(s, axis=0, keepdims=True))
            alpha = jnp.exp2(m_prev - m_new)
            p = jnp.exp2(s - m_new)
            l_sc[u] = alpha * l_sc[u] + jnp.sum(p, axis=0, keepdims=True)
            m_sc[u] = m_new
            acc_sc[u] = alpha * acc_sc[u] + _dot(vt_sc[slot, 128 * pr:128 * (pr + 1), 0:nk], p.astype(BF16))

    bufs = (sa_sc, sb_sc)
    for c in range(n_tiles):
        stack_queries(c)
    scores_into(0, 0, bufs[0])
    m_sc[...] = jnp.full(m_sc.shape, NEG_INF, F32)
    l_sc[...] = jnp.zeros(l_sc.shape, F32)
    acc_sc[...] = jnp.zeros(acc_sc.shape, F32)

    def past_block(kb, slot):
        for c in range(n_tiles):
            if c + 1 < n_tiles:
                scores_into(kb, c + 1, bufs[(c + 1) % 2])
            else:
                scores_into(kb + 1, 0, bufs[0])
            consume(kb, c, bufs[c % 2], slot=slot)

    def past_blocks(n):
        def body(i, carry):
            for j in range(n):
                transpose_values(n * i + j, j)
            for j in range(n):
                past_block(n * i + j, j)
            return carry
        return body

    fours, twos = t >> 2, (t >> 1) & 1
    lax.fori_loop(0, fours, past_blocks(4), 0)
    lax.fori_loop(2 * fours, 2 * fours + twos, past_blocks(2), 0)
    lax.fori_loop(t & ~1, t, past_blocks(1), 0)

    transpose_values(t)
    for c in range(n_tiles):
        if c + 1 < n_tiles:
            scores_into(t, c + 1, bufs[(c + 1) % 2], nk=TQ * (c + 2))
        consume(t, c, bufs[c % 2], nk=TQ * (c + 1), diagonal=True)
        parts = []
        for pr in range(2):
            u = 2 * c + pr
            inv = 1.0 / l_sc[u]
            for e in range(2):
                parts.append(acc_sc[u, 64 * e:64 * (e + 1), 128 * e:128 * (e + 1)] * inv[:, 128 * e:128 * (e + 1)])
        o = jnp.concatenate(parts, axis=0).T
        o_ref[0, TQ * c:TQ * (c + 1), :] = (o * g_ref[0, TQ * c:TQ * (c + 1), :].astype(F32)).astype(BF16)


def _attn_b(p3):
    b, s, _ = p3.shape
    n_units = 2 * (B_TQ // TQ)
    return pl.pallas_call(
        _attn_b_kernel,
        grid=(b, s // B_TQ),
        in_specs=[
            pl.BlockSpec((1, B_TQ, 256), lambda i, t: (i, t, P_BQ0 // 256)),
            pl.BlockSpec((1, B_TQ, 256), lambda i, t: (i, t, P_BQ1 // 256)),
            pl.BlockSpec((1, s, 256), lambda i, t: (i, 0, P_BK0 // 256)),
            pl.BlockSpec((1, s, 256), lambda i, t: (i, 0, P_BK1 // 256)),
            pl.BlockSpec((1, s, 256), lambda i, t: (i, 0, P_BV // 256)),
            pl.BlockSpec((1, B_TQ, 256), lambda i, t: (i, t, P_GB // 256)),
        ],
        out_specs=pl.BlockSpec((1, B_TQ, 256), lambda i, t: (i, t, 0)),
        out_shape=jax.ShapeDtypeStruct((b, s, GROUP), BF16),
        scratch_shapes=[pltpu.VMEM((n_units, 2 * TQ, 256), BF16),
                        pltpu.VMEM((4, GROUP, B_TK), BF16),
                        pltpu.VMEM((2, B_TK, 2 * TQ), F32),
                        pltpu.VMEM((2, B_TK, 2 * TQ), F32),
                        pltpu.VMEM((n_units, 1, 2 * TQ), F32),
                        pltpu.VMEM((n_units, 1, 2 * TQ), F32),
                        pltpu.VMEM((n_units, 128, 2 * TQ), F32)],
        compiler_params=pltpu.CompilerParams(
            dimension_semantics=("parallel", "arbitrary"), vmem_limit_bytes=VMEM_LIMIT),
        name="attn_b",
    )(p3, p3, p3, p3, p3, p3)


def _residual_norm(ya_ref, yb_ref, yc_ref, ym_ref, x_ref, w_ref, g_ref, b_ref, rows=slice(None)):
    y = (_dot(ya_ref[rows, :], w_ref[0:256, :]) + _dot(yb_ref[rows, :], w_ref[256:512, :])
         + _dot(yc_ref[rows, :], w_ref[512:768, :]) + _dot(ym_ref[rows, :], w_ref[768:1024, :]))
    z = ALPHA * x_ref[rows, :] + y
    mu = jnp.mean(z, axis=-1, keepdims=True)
    zc = z - mu
    var = jnp.mean(zc * zc, axis=-1, keepdims=True)
    return zc * lax.rsqrt(var + 1e-5) * g_ref[...] + b_ref[...]


def _outproj_kernel(ya_ref, yb_ref, yc_ref, ym_ref, x_ref, w_ref, g_ref, b_ref, o_ref):
    for h in range(TM_OUT // 256):
        rows = pl.ds(h * 256, 256)
        o_ref[rows, :] = _residual_norm(ya_ref, yb_ref, yc_ref, ym_ref, x_ref, w_ref, g_ref, b_ref, rows)


def _out_in_proj_kernel(ya_ref, yb_ref, yc_ref, ym_ref, x_ref, wo_ref, g_ref, b_ref,
                        tab_ref, w_ref, wuq_ref, wukv_ref, gq_ref, gkv_ref, o_ref, p_ref):
    halves = [pl.ds(h * (TM // 2), TM // 2) for h in range(2)]
    xb = []
    for rows in halves:
        xn = _residual_norm(ya_ref, yb_ref, yc_ref, ym_ref, x_ref, wo_ref, g_ref, b_ref, rows)
        o_ref[rows, :] = xn
        xb.append(xn.astype(BF16))
    for rows, x in zip(halves, xb):
        _project(x, tab_ref, w_ref, wuq_ref, wukv_ref, gq_ref, gkv_ref, p_ref, rows)


def _outproj(ya, yb, yc, ym, x2d, w, g, bias):
    n = x2d.shape[0]
    ytile = pl.BlockSpec((TM_OUT, GROUP), lambda i: (i, 0))
    const = lambda shape: pl.BlockSpec(shape, lambda i: (0,) * len(shape))
    return pl.pallas_call(
        _outproj_kernel,
        grid=(n // TM_OUT,),
        in_specs=[ytile, ytile, ytile, ytile,
                  pl.BlockSpec((TM_OUT, D_MODEL), lambda i: (i, 0)),
                  const((D_MODEL, D_MODEL)), const((1, D_MODEL)), const((1, D_MODEL))],
        out_specs=pl.BlockSpec((TM_OUT, D_MODEL), lambda i: (i, 0)),
        out_shape=jax.ShapeDtypeStruct((n, D_MODEL), F32),
        compiler_params=pltpu.CompilerParams(
            dimension_semantics=("parallel",), vmem_limit_bytes=VMEM_LIMIT),
        name="outproj",
    )(ya, yb, yc, ym, x2d, w, g, bias)


def _out_in_proj(ya, yb, yc, ym, x2d, wo, g, bias, tab, w, wuq, wukv, gq, gkv):
    n = x2d.shape[0]
    ytile = pl.BlockSpec((TM, GROUP), lambda i: (i, 0))
    const = lambda shape: pl.BlockSpec(shape, lambda i: (0,) * len(shape))
    return pl.pallas_call(
        _out_in_proj_kernel,
        grid=(n // TM,),
        in_specs=[ytile, ytile, ytile, ytile,
                  pl.BlockSpec((TM, D_MODEL), lambda i: (i, 0)),
                  const((D_MODEL, D_MODEL)), const((1, D_MODEL)), const((1, D_MODEL)),
                  pl.BlockSpec((TM, 512), lambda i: (i, 0)),
                  const((D_MODEL, W_WIDTH)), const((256, 512)), const((128, 512)),
                  const((1, 256)), const((1, 128))],
        out_specs=[pl.BlockSpec((TM, D_MODEL), lambda i: (i, 0)),
                   pl.BlockSpec((TM, P_WIDTH), lambda i: (i, 0))],
        out_shape=[jax.ShapeDtypeStruct((n, D_MODEL), F32),
                   jax.ShapeDtypeStruct((n, P_WIDTH), BF16)],
        compiler_params=pltpu.CompilerParams(
            dimension_semantics=("parallel",), vmem_limit_bytes=VMEM_LIMIT),
        name="out_in_proj",
    )(ya, yb, yc, ym, x2d, wo, g, bias, tab, w, wuq, wukv, gq, gkv)


def kernel(x, mem, positions, w_in, rel_bias, mla_q_norm, w_uq, mla_kv_norm, w_ukv,
           swa_sinks, w_mem_kv, w_out, ln_gain, ln_bias):
    b, s, d = x.shape
    depth = w_in.shape[0]
    assert d == D_MODEL and depth == DEPTH and s % B_TK == 0 and s >= A_WIN and (b * s) % TM_OUT == 0 and TM_OUT % TM == 0

    cols, cperm = _inproj_cols()
    w_in_p = _take_cols(w_in, cols).astype(BF16)
    wuq_p = jnp.pad(_take_cols(w_uq, _uq_cols()), ((0, 0), (0, 256 - MLA_Q_RANK), (0, 0))).astype(BF16)
    wukv_p = _take_cols(w_ukv, _ukv_cols()).astype(BF16)
    gq = jnp.pad(mla_q_norm, ((0, 0), (0, 256 - MLA_Q_RANK)))[:, None, :]
    gkv = mla_kv_norm[:, None, :]
    rows = np.concatenate([np.arange(512), 512 + cperm, np.arange(768, 1024)])
    w_out_p = _take_cols(w_out, rows, axis=1).astype(BF16)
    w_mem_all = jnp.transpose(w_mem_kv, (1, 0, 2)).reshape(D_MODEL, depth * 512).astype(BF16)
    e_a = _bias_table_a(rel_bias)
    e_c = jnp.asarray(_mask_table_c())
    pieces, expand3 = _rope_pieces(positions)

    memkv = _memkv(mem, w_mem_all)
    h = x.reshape(b * s, d)
    p2, tab = _inproj(h, pieces, expand3, w_in_p[0], wuq_p[0], wukv_p[0], gq[0], gkv[0])
    for l in range(depth):
        p3 = p2.reshape(b, s, P_WIDTH)
        ya = _window_attn(p3, P_AQ, p3, P_AK, p3, P_AV, P_GA, win=A_WIN, prev=A_PREV * CHUNK, dk=256, table=e_a[l])
        yb = _attn_b(p3)
        yc = _window_attn(p3, P_CQ, p3, P_CK, p3, P_CV, P_GC, win=C_WIN, prev=SWA_PREV * CHUNK, dk=128,
                          table=e_c, sinks=swa_sinks[l])
        ym = _window_attn(p3, P_MQ, memkv, 512 * l, memkv, 512 * l + 256, P_GM, win=MEM_LEN, prev=None, dk=256)
        ys = [y.reshape(b * s, GROUP) for y in (ya, yb, yc, ym)]
        ln = (ln_gain[l][None, :], ln_bias[l][None, :])
        if l + 1 < depth:
            h, p2 = _out_in_proj(*ys, h, w_out_p[l], *ln, tab, w_in_p[l + 1], wuq_p[l + 1], wukv_p[l + 1],
                                 gq[l + 1], gkv[l + 1])
        else:
            h = _outproj(*ys, h, w_out_p[l], *ln)
    return h.reshape(b, s, d)
```

```python
import functools

import numpy as np
import jax
import jax.numpy as jnp
from jax import lax
from jax.experimental import pallas as pl
from jax.experimental.pallas import tpu as pltpu

F32 = jnp.float32
BF16 = jnp.bfloat16

D_MODEL = 1024
DEPTH = 4
CHUNK = 64
HEAD_DIM = 64
GROUP = 256
N_HEADS = 4
ROPE_THETA = 10000.0
NEG_INF = -1e30
A_PREV = 8
REL_CLIP = 128
MLA_NOPE = 64
MLA_ROPE = 32
MLA_Q_RANK = 192
MLA_KV_RANK = 128
SWA_PREV = 2
MEM_LEN = 256
ALPHA = (2.0 * DEPTH) ** 0.25

TQ = 128
A_WIN = TQ + A_PREV * CHUNK
C_WIN = TQ + SWA_PREV * CHUNK
WIN_UNROLL = {640: 16, 256: 32}
B_TK = 512
B_TQ = 512
TM = 512
TM_OUT = 1024
VMEM_LIMIT = 56 * 1024 * 1024
LOG2E = 1.4426950408889634
QSCALE = HEAD_DIM ** -0.5 * LOG2E
B_QSCALE = (MLA_NOPE + MLA_ROPE) ** -0.5 * LOG2E

P_AQ, P_AK, P_AV = 0, 256, 512
P_BQ0, P_BQ1, P_BK0, P_BK1, P_BV = 768, 1024, 1280, 1536, 1792
P_CQ, P_CK, P_CV = 2048, 2304, 2432
P_MQ = 2560
P_GA, P_GB, P_GC, P_GM = 2816, 3072, 3328, 3584
P_WIDTH = 3840

W_A, W_C, W_M, W_G, W_B, W_WIDTH = 0, 768, 1280, 1536, 2560, 3072

C_HEAD_ORDER = (0, 2, 1, 3)


def _inproj_cols():
    r = np.arange
    aq, ak, av, ag = 0, 256, 512, 768
    bcq, bckv, bkr, bg = 1024, 1216, 1344, 1376
    cq, ck, cv, cg = 1632, 1888, 2016, 2144
    mq, mg = 2400, 2656
    cperm = np.concatenate([r(64) + 64 * h for h in C_HEAD_ORDER])
    pad = lambda n: np.full(n, -1)
    cols = np.concatenate([
        aq + r(256), ak + r(256), av + r(256),
        cq + cperm, ck + r(128), cv + r(128),
        mq + r(256),
        ag + r(256), bg + r(256), cg + cperm, mg + r(256),
        bcq + r(192), pad(64), bckv + r(128), bkr + r(32), bkr + r(32), pad(64),
    ])
    assert cols.shape[0] == W_WIDTH
    return cols, cperm


def _take_cols(w, cols, axis=-1):
    axis = axis % w.ndim
    pieces, i = [], 0
    while i < len(cols):
        j = i + 1
        if cols[i] < 0:
            while j < len(cols) and cols[j] < 0:
                j += 1
            shape = w.shape[:axis] + (j - i,) + w.shape[axis + 1:]
            pieces.append(jnp.zeros(shape, w.dtype))
        else:
            while j < len(cols) and cols[j] == cols[j - 1] + 1:
                j += 1
            pieces.append(lax.slice_in_dim(w, int(cols[i]), int(cols[i]) + (j - i), axis=axis))
        i = j
    return jnp.concatenate(pieces, axis=axis)


def _uq_cols():
    r = np.arange
    per = MLA_NOPE + MLA_ROPE
    out = []
    for p in range(2):
        h0, h1 = 2 * p, 2 * p + 1
        out += [per * h0 + r(64), per * h1 + r(64),
                per * h0 + 64 + r(32), per * h1 + 64 + r(32), np.full(64, -1)]
    return np.concatenate(out)


def _ukv_cols():
    r = np.arange
    return np.concatenate([128 * h + r(64) for h in range(4)] + [128 * h + 64 + r(64) for h in range(4)])


def _rope_pieces(positions):
    pos = positions.astype(F32).reshape(1, -1)
    half = HEAD_DIM // 2
    inv = ROPE_THETA ** (-jnp.arange(0, HEAD_DIM, 2, dtype=F32) / HEAD_DIM)
    ang = inv[:, None] * pos
    x = jnp.concatenate([jnp.cos(ang), jnp.sin(ang)], axis=0)
    expand = np.zeros((2 * half, 512), np.float32)
    lane = np.arange(128)
    for t, d in enumerate((HEAD_DIM, MLA_ROPE)):
        k = ((lane % d) % (d // 2)) * (HEAD_DIM // d)
        expand[k, 256 * t + lane] = 1.0
        expand[half + k, 256 * t + 128 + lane] = np.where(lane % d < d // 2, -1.0, 1.0)
    x = lax.optimization_barrier(x)
    hi = x.astype(BF16)
    rest = x - hi.astype(F32)
    mid = rest.astype(BF16)
    lo = (rest - mid.astype(F32)).astype(BF16)
    return jnp.concatenate([hi, mid, lo], axis=0), jnp.asarray(np.concatenate([expand] * 3, axis=0), BF16)


def _bias_table_a(rel_bias):
    width, period = 9 * 128, 9 * 128 + TQ + 1
    k = np.arange(period)
    d = np.where(k < width, A_PREV * CHUNK - k, A_PREV * CHUNK + period - k)
    idx = np.clip(d, -REL_CLIP, REL_CLIP) + REL_CLIP
    n_hi = A_PREV * CHUNK - REL_CLIP + 1
    n_lo = width - n_hi - (2 * REL_CLIP - 1)
    expect = np.concatenate([np.full(n_hi, 2 * REL_CLIP), np.arange(2 * REL_CLIP - 1, 0, -1),
                             np.zeros(n_lo, np.int64), np.full(period - width, 2 * REL_CLIP)])
    assert np.array_equal(idx, expect)
    rep = lambda col, n: jnp.broadcast_to(rel_bias[:, :, col:col + 1], rel_bias.shape[:2] + (n,))
    gp = jnp.concatenate([rep(2 * REL_CLIP, n_hi), jnp.flip(rel_bias[:, :, 1:2 * REL_CLIP], axis=-1),
                          rep(0, n_lo), rep(2 * REL_CLIP, period - width)], axis=-1) * LOG2E
    flat = jnp.tile(gp, (1, 1, TQ))[:, :, :TQ * (period - 1)]
    skew = flat.reshape(gp.shape[0], N_HEADS, TQ, period - 1)[..., :width]
    i = np.arange(TQ)[:, None]
    m = np.arange(width)[None, :]
    dchunk = i // CHUNK + A_PREV - m // CHUNK
    valid = (dchunk >= 0) & (dchunk <= A_PREV)
    t = jnp.where(jnp.asarray(valid)[None, None], skew, NEG_INF)
    return jnp.transpose(t, (0, 3, 1, 2)).reshape(gp.shape[0], 9, 128, N_HEADS * TQ)


def _mask_table_c():
    m = np.arange(3 * 128)[:, None]
    i = np.arange(TQ)[None, :]
    dchunk = i // CHUNK + SWA_PREV - m // CHUNK
    valid = (dchunk >= 0) & (dchunk <= SWA_PREV)
    t = np.where(valid, 0.0, NEG_INF).astype(np.float32)
    return np.tile(t, (1, N_HEADS)).reshape(3, 128, N_HEADS * TQ)


def _dot(a, b):
    return jnp.dot(a, b, preferred_element_type=F32)


def _dot_nt(a, b):
    return lax.dot_general(a, b, (((1,), (1,)), ((), ())), preferred_element_type=F32)


def _dot_tn(a, b):
    return lax.dot_general(a, b, (((0,), (0,)), ((), ())), preferred_element_type=F32)


def _rope(x, cos, sin_signed, half):
    lane = lax.broadcasted_iota(jnp.int32, x.shape, 1)
    first = (lane & (2 * half - 1)) < half
    swapped = jnp.where(first, pltpu.roll(x, 128 - half, 1), pltpu.roll(x, half, 1))
    return x * cos + swapped * sin_signed


def _project(xb, tab_ref, w_ref, wuq_ref, wukv_ref, gq_ref, gkv_ref, p_ref, rows=slice(None)):
    tab_ref, p_ref = tab_ref.at[rows], p_ref.at[rows]
    cos64, sin64 = tab_ref[:, 0:128], tab_ref[:, 128:256]
    cos32, sin32 = tab_ref[:, 256:384], tab_ref[:, 384:512]

    def mm(lo, hi):
        return _dot(xb, w_ref[:, lo:hi])

    rb = mm(W_B, W_B + 512)
    cq = rb[:, 0:256]
    ms = jnp.sum(cq * cq, axis=-1, keepdims=True) * (1.0 / MLA_Q_RANK)
    qn = (cq * lax.rsqrt(ms + 1e-6) * gq_ref[...]).astype(BF16)
    ckv = rb[:, 256:384]
    ms = jnp.mean(ckv * ckv, axis=-1, keepdims=True)
    kvn = (ckv * lax.rsqrt(ms + 1e-6) * gkv_ref[...]).astype(BF16)
    krb = _rope(rb[:, 384:512], cos32, sin32, 16).astype(BF16)

    r = mm(W_A, W_A + 768)
    p_ref[:, P_AQ:P_AQ + 256] = (r[:, 0:256] * QSCALE).astype(BF16)
    p_ref[:, P_AK:P_AK + 512] = r[:, 256:768].astype(BF16)

    q = _dot(qn, wuq_ref[...]) * B_QSCALE
    for p in range(2):
        base = P_BQ0 + 256 * p
        p_ref[:, base:base + 128] = q[:, 256 * p:256 * p + 128].astype(BF16)
        p_ref[:, base + 128:base + 256] = _rope(q[:, 256 * p + 128:256 * p + 256], cos32, sin32, 16).astype(BF16)
    kv = _dot(kvn, wukv_ref[...])
    for p in range(2):
        base = P_BK0 + 256 * p
        p_ref[:, base:base + 128] = kv[:, 128 * p:128 * (p + 1)].astype(BF16)
        p_ref[:, base + 128:base + 256] = krb
    p_ref[:, P_BV:P_BV + 256] = kv[:, 256:512].astype(BF16)

    r = mm(W_G, W_G + 1024)
    p_ref[:, P_GA:P_GA + 1024] = (r * (1.0 / (1.0 + jnp.exp(-r)))).astype(BF16)

    r = mm(W_C, W_C + 512)
    for j in range(2):
        qj = _rope(r[:, 128 * j:128 * (j + 1)], cos64, sin64, 32)
        p_ref[:, P_CQ + 128 * j:P_CQ + 128 * (j + 1)] = (qj * QSCALE).astype(BF16)
    p_ref[:, P_CK:P_CK + 128] = _rope(r[:, 256:384], cos64, sin64, 32).astype(BF16)
    p_ref[:, P_CV:P_CV + 128] = r[:, 384:512].astype(BF16)

    r = mm(W_M, W_M + 256)
    p_ref[:, P_MQ:P_MQ + 256] = (r * QSCALE).astype(BF16)


def _inproj_kernel(x_ref, pieces_ref, expand_ref, w_ref, wuq_ref, wukv_ref, gq_ref, gkv_ref, p_ref, tab_ref):
    tab_ref[...] = _dot_tn(pieces_ref[...], expand_ref[...])
    _project(x_ref[...].astype(BF16), tab_ref, w_ref, wuq_ref, wukv_ref, gq_ref, gkv_ref, p_ref)


def _inproj(x2d, pieces, expand3, w, wuq, wukv, gq, gkv):
    n = x2d.shape[0]
    const = lambda shape: pl.BlockSpec(shape, lambda i: (0,) * len(shape))
    return pl.pallas_call(
        _inproj_kernel,
        grid=(n // TM,),
        in_specs=[
            pl.BlockSpec((TM, D_MODEL), lambda i: (i, 0)),
            pl.BlockSpec((192, TM), lambda i: (0, i)),
            const((192, 512)),
            const((D_MODEL, W_WIDTH)),
            const((256, 512)),
            const((128, 512)),
            const((1, 256)),
            const((1, 128)),
        ],
        out_specs=[pl.BlockSpec((TM, P_WIDTH), lambda i: (i, 0)),
                   pl.BlockSpec((TM, 512), lambda i: (i, 0))],
        out_shape=[jax.ShapeDtypeStruct((n, P_WIDTH), BF16),
                   jax.ShapeDtypeStruct((n, 512), F32)],
        compiler_params=pltpu.CompilerParams(
            dimension_semantics=("parallel",), vmem_limit_bytes=VMEM_LIMIT),
        name="inproj",
    )(x2d, pieces, expand3, w, wuq, wukv, gq, gkv)


def _memkv_kernel(mem_ref, w_ref, o_ref):
    o_ref[0] = _dot(mem_ref[0].astype(BF16), w_ref[...]).astype(BF16)


def _memkv(mem, w_all):
    b = mem.shape[0]
    n = w_all.shape[1]
    return pl.pallas_call(
        _memkv_kernel,
        grid=(b,),
        in_specs=[pl.BlockSpec((1, MEM_LEN, D_MODEL), lambda i: (i, 0, 0)),
                  pl.BlockSpec((D_MODEL, n), lambda i: (0, 0))],
        out_specs=pl.BlockSpec((1, MEM_LEN, n), lambda i: (i, 0, 0)),
        out_shape=jax.ShapeDtypeStruct((b, MEM_LEN, n), BF16),
        compiler_params=pltpu.CompilerParams(
            dimension_semantics=("parallel",), vmem_limit_bytes=VMEM_LIMIT),
        name="memkv",
    )(mem, w_all)


def _window_attn_kernel(*refs, win, prev, dk, has_table, has_sink, unroll):
    refs = list(refs)
    sink_ref = refs.pop(0) if has_sink else None
    q_ref, k_ref, v_ref, g_ref = refs[:4]
    e_ref = refs[4] if has_table else None
    o_ref, sa_sc, sb_sc = refs[-3:]
    n_items = q_ref.shape[1] // TQ
    lanes = N_HEADS * TQ

    lane128 = lax.broadcasted_iota(jnp.int32, (TQ, 128), 1)
    lo, hi = lane128 < HEAD_DIM, lane128 >= HEAD_DIM
    if has_sink:
        col = lax.broadcasted_iota(jnp.int32, (1, lanes), 1)
        order = C_HEAD_ORDER if dk == 128 else tuple(range(N_HEADS))
        sink = jnp.where(col < TQ, sink_ref[order[0]],
                         jnp.where(col < 2 * TQ, sink_ref[order[1]],
                                   jnp.where(col < 3 * TQ, sink_ref[order[2]], sink_ref[order[3]]))) * LOG2E

    def window_start(item):
        if prev is None:
            return 0
        return pl.multiple_of(jnp.maximum(item * TQ - prev, 0), 128)

    def scores_into(item, s_sc):
        q = q_ref[0, pl.ds(pl.multiple_of(item * TQ, TQ), TQ), :].astype(F32)
        if dk == 256:
            zero = jnp.zeros((TQ, 128), F32)
            blocks = [jnp.concatenate([jnp.where(lo, q[:, 0:128], 0.0), zero], axis=1),
                      jnp.concatenate([jnp.where(hi, q[:, 0:128], 0.0), zero], axis=1),
                      jnp.concatenate([zero, jnp.where(lo, q[:, 128:256], 0.0)], axis=1),
                      jnp.concatenate([zero, jnp.where(hi, q[:, 128:256], 0.0)], axis=1)]
        else:
            blocks = [jnp.where(lo, q[:, 0:128], 0.0), jnp.where(hi, q[:, 0:128], 0.0),
                      jnp.where(lo, q[:, 128:256], 0.0), jnp.where(hi, q[:, 128:256], 0.0)]
        qs = jnp.concatenate(blocks, axis=0).astype(BF16)
        s = _dot_nt(k_ref[0, pl.ds(window_start(item), win), :], qs)
        if has_table:
            mb0 = jnp.maximum(prev // 128 - item, 0)
            s = s + jnp.concatenate([e_ref[mb0 + jb] for jb in range(win // 128)], axis=0)
        s_sc[...] = s

    def consume(item, s_sc):
        start = window_start(item)
        v = v_ref[0, pl.ds(start, win), :]
        parts = []
        for pr in range(2):
            cols = slice(256 * pr, 256 * (pr + 1))
            m = jnp.max(s_sc[:, cols], axis=0, keepdims=True)
            if has_sink:
                m = jnp.maximum(m, sink[:, cols])
            p = jnp.exp2(s_sc[:, cols] - m)
            l = jnp.sum(p, axis=0, keepdims=True)
            if has_sink:
                l = l + jnp.exp2(sink[:, cols] - m)
            inv = 1.0 / l
            vp = v[:, 128 * pr:128 * (pr + 1)] if dk == 256 else v
            ot = _dot_tn(vp, p.astype(BF16))
            for e in range(2):
                parts.append(ot[64 * e:64 * (e + 1), 128 * e:128 * (e + 1)] * inv[:, TQ * e:TQ * (e + 1)])
        o = jnp.concatenate(parts, axis=0).T
        rows = pl.ds(pl.multiple_of(item * TQ, TQ), TQ)
        o_ref[0, rows, :] = (o * g_ref[0, rows, :].astype(F32)).astype(BF16)

    bufs = (sa_sc, sb_sc)
    scores_into(0, bufs[0])

    def body(i, carry):
        for j in range(unroll):
            item = unroll * i + j
            scores_into(jnp.minimum(item + 1, n_items - 1), bufs[(j + 1) % 2])
            consume(item, bufs[j % 2])
        return carry

    lax.fori_loop(0, n_items // unroll, body, 0)


def _window_attn(q_src, q_col, k_src, k_col, v_src, v_col, g_col, *, win, prev, dk, table=None, sinks=None):
    b, s, _ = q_src.shape
    skv = k_src.shape[1]
    unroll = min(WIN_UNROLL[win], s // TQ)
    assert s % (unroll * TQ) == 0 and unroll % 2 == 0 and skv >= win
    kern = functools.partial(_window_attn_kernel, win=win, prev=prev, dk=dk,
                             has_table=table is not None, has_sink=sinks is not None, unroll=unroll)
    in_specs, args = [], []
    if sinks is not None:
        in_specs.append(pl.BlockSpec(memory_space=pltpu.SMEM)); args.append(sinks)
    in_specs += [pl.BlockSpec((1, s, 256), lambda i: (i, 0, q_col // 256)),
                 pl.BlockSpec((1, skv, dk), lambda i: (i, 0, k_col // dk)),
                 pl.BlockSpec((1, skv, dk), lambda i: (i, 0, v_col // dk)),
                 pl.BlockSpec((1, s, 256), lambda i: (i, 0, g_col // 256))]
    args += [q_src, k_src, v_src, q_src]
    if table is not None:
        in_specs.append(pl.BlockSpec(table.shape, lambda i: (0, 0, 0))); args.append(table)
    return pl.pallas_call(
        kern,
        grid=(b,),
        in_specs=in_specs,
        out_specs=pl.BlockSpec((1, s, 256), lambda i: (i, 0, 0)),
        out_shape=jax.ShapeDtypeStruct((b, s, GROUP), BF16),
        scratch_shapes=[pltpu.VMEM((win, N_HEADS * TQ), F32), pltpu.VMEM((win, N_HEADS * TQ), F32)],
        compiler_params=pltpu.CompilerParams(
            dimension_semantics=("parallel",), vmem_limit_bytes=VMEM_LIMIT),
        name="attn_win%d" % win,
    )(*args)


def _attn_b_kernel(q0_ref, q1_ref, k0_ref, k1_ref, v_ref, g_ref, o_ref,
                   vt_sc, sa_sc, sb_sc, m_sc, l_sc, acc_sc, *qs_sc):
    t = pl.program_id(1)
    n_tiles = B_TQ // TQ
    lane = lax.broadcasted_iota(jnp.int32, (TQ, 128), 1)
    q_refs = (q0_ref, q1_ref)
    k_refs = (k0_ref, k1_ref)

    def stack_queries(c, zero=0.0):
        for pr in range(2):
            q = q_refs[pr][0, TQ * c:TQ * (c + 1), :].astype(F32)
            nope, rope = q[:, 0:128], q[:, 128:256]
            head_a = jnp.concatenate([jnp.where(lane < 64, nope, zero), jnp.where(lane < 32, rope, zero)], axis=1)
            head_b = jnp.concatenate([jnp.where(lane >= 64, nope, zero), jnp.where(lane >= 32, rope, zero)], axis=1)
            qs_sc[2 * c + pr][...] = jnp.concatenate([head_a, head_b], axis=0).astype(BF16)

    lane2 = lax.broadcasted_iota(jnp.int32, (1, 2 * TQ), 1)
    hide_first_chunk = jnp.where((lane2 & (TQ - 1)) < CHUNK, NEG_INF, 0.0)

    def scores_into(kb, c, s_sc, nk=B_TK):
        start = pl.multiple_of(kb * B_TK, B_TK)
        for pr in range(2):
            s_sc[pr, 0:nk, :] = _dot_nt(k_refs[pr][0, pl.ds(start, nk), :], qs_sc[2 * c + pr][...])

    def transpose_values(kb, slot=0):
        vt_sc[slot] = v_ref[0, pl.ds(pl.multiple_of(kb * B_TK, B_TK), B_TK), :].T

    def consume(kb, c, s_sc, nk=B_TK, diagonal=False, slot=0):
        for pr in range(2):
            u = 2 * c + pr
            s = s_sc[pr, 0:nk, :]
            if diagonal:
                s = jnp.concatenate([s[:nk - CHUNK], s[nk - CHUNK:] + hide_first_chunk], axis=0)
            m_prev = m_sc[u]
            m_new = jnp.maximum(m_prev, jnp.max(s, axis=0, keepdims=True))
            alpha = jnp.exp2(m_prev - m_new)
            p = jnp.exp2(s - m_new)
            l_sc[u] = alpha * l_sc[u] + jnp.sum(p, axis=0, keepdims=True)
            m_sc[u] = m_new
            acc_sc[u] = alpha * acc_sc[u] + _dot(vt_sc[slot, 128 * pr:128 * (pr + 1), 0:nk], p.astype(BF16))

    bufs = (sa_sc, sb_sc)
    stack_queries(0)
    scores_into(0, 0, bufs[0])
    bits = pltpu.bitcast(k0_ref[0, 0:16, 0:128], jnp.uint32)
    bits = lax.shift_right_logical(lax.shift_right_logical(bits, jnp.uint32(16)), jnp.uint32(16))
    zero = bits.astype(F32)[0:1, :]
    for c in range(1, n_tiles):
        stack_queries(c, zero)
    zero2 = jnp.concatenate([zero, zero], axis=1)[None]
    m_sc[...] = jnp.full(m_sc.shape, NEG_INF, F32) + zero2
    l_sc[...] = jnp.broadcast_to(zero2, l_sc.shape)
    acc_sc[...] = jnp.broadcast_to(zero2, acc_sc.shape)

    def past_block(kb, slot):
        for c in range(n_tiles):
            if c + 1 < n_tiles:
                scores_into(kb, c + 1, bufs[(c + 1) % 2])
            else:
                scores_into(kb + 1, 0, bufs[0])
            consume(kb, c, bufs[c % 2], slot=slot)

    def past_blocks(n):
        def body(i, carry):
            for j in range(n):
                transpose_values(n * i + j, j)
            for j in range(n):
                past_block(n * i + j, j)
            return carry
        return body

    fours, twos = t >> 2, (t >> 1) & 1
    lax.fori_loop(0, fours, past_blocks(4), 0)
    lax.fori_loop(2 * fours, 2 * fours + twos, past_blocks(2), 0)
    lax.fori_loop(t & ~1, t, past_blocks(1), 0)

    transpose_values(t)
    for c in range(n_tiles):
        if c + 1 < n_tiles:
            scores_into(t, c + 1, bufs[(c + 1) % 2], nk=TQ * (c + 2))
        consume(t, c, bufs[c % 2], nk=TQ * (c + 1), diagonal=True)
        parts = []
        for pr in range(2):
            u = 2 * c + pr
            inv = 1.0 / l_sc[u]
            for e in range(2):
                parts.append(acc_sc[u, 64 * e:64 * (e + 1), 128 * e:128 * (e + 1)] * inv[:, 128 * e:128 * (e + 1)])
        o = jnp.concatenate(parts, axis=0).T
        o_ref[0, TQ * c:TQ * (c + 1), :] = (o * g_ref[0, TQ * c:TQ * (c + 1), :].astype(F32)).astype(BF16)


def _attn_b(p3):
    b, s, _ = p3.shape
    n_units = 2 * (B_TQ // TQ)
    return pl.pallas_call(
        _attn_b_kernel,
        grid=(b, s // B_TQ),
        in_specs=[
            pl.BlockSpec((1, B_TQ, 256), lambda i, t: (i, t, P_BQ0 // 256)),
            pl.BlockSpec((1, B_TQ, 256), lambda i, t: (i, t, P_BQ1 // 256)),
            pl.BlockSpec((1, s, 256), lambda i, t: (i, 0, P_BK0 // 256)),
            pl.BlockSpec((1, s, 256), lambda i, t: (i, 0, P_BK1 // 256)),
            pl.BlockSpec((1, s, 256), lambda i, t: (i, 0, P_BV // 256)),
            pl.BlockSpec((1, B_TQ, 256), lambda i, t: (i, t, P_GB // 256)),
        ],
        out_specs=pl.BlockSpec((1, B_TQ, 256), lambda i, t: (i, t, 0)),
        out_shape=jax.ShapeDtypeStruct((b, s, GROUP), BF16),
        scratch_shapes=[pltpu.VMEM((4, GROUP, B_TK), BF16),
                        pltpu.VMEM((2, B_TK, 2 * TQ), F32),
                        pltpu.VMEM((2, B_TK, 2 * TQ), F32),
                        pltpu.VMEM((n_units, 1, 2 * TQ), F32),
                        pltpu.VMEM((n_units, 1, 2 * TQ), F32),
                        pltpu.VMEM((n_units, 128, 2 * TQ), F32)]
                       + [pltpu.VMEM((2 * TQ, 256), BF16)] * n_units,
        compiler_params=pltpu.CompilerParams(
            dimension_semantics=("parallel", "arbitrary"), vmem_limit_bytes=VMEM_LIMIT),
        name="attn_b",
    )(p3, p3, p3, p3, p3, p3)


def _residual_norm(ya_ref, yb_ref, yc_ref, ym_ref, x_ref, w_ref, g_ref, b_ref, rows=slice(None)):
    y = (_dot(ya_ref[rows, :], w_ref[0:256, :]) + _dot(yb_ref[rows, :], w_ref[256:512, :])
         + _dot(yc_ref[rows, :], w_ref[512:768, :]) + _dot(ym_ref[rows, :], w_ref[768:1024, :]))
    z = ALPHA * x_ref[rows, :] + y
    mu = jnp.mean(z, axis=-1, keepdims=True)
    zc = z - mu
    var = jnp.mean(zc * zc, axis=-1, keepdims=True)
    return zc * lax.rsqrt(var + 1e-5) * g_ref[...] + b_ref[...]


def _outproj_kernel(ya_ref, yb_ref, yc_ref, ym_ref, x_ref, w_ref, g_ref, b_ref, o_ref):
    for h in range(TM_OUT // 256):
        rows = pl.ds(h * 256, 256)
        o_ref[rows, :] = _residual_norm(ya_ref, yb_ref, yc_ref, ym_ref, x_ref, w_ref, g_ref, b_ref, rows)


def _out_in_proj_kernel(ya_ref, yb_ref, yc_ref, ym_ref, x_ref, wo_ref, g_ref, b_ref,
                        tab_ref, w_ref, wuq_ref, wukv_ref, gq_ref, gkv_ref, o_ref, p_ref):
    halves = [pl.ds(h * (TM // 2), TM // 2) for h in range(2)]
    xb = []
    for rows in halves:
        xn = _residual_norm(ya_ref, yb_ref, yc_ref, ym_ref, x_ref, wo_ref, g_ref, b_ref, rows)
        o_ref[rows, :] = xn
        xb.append(xn.astype(BF16))
    for rows, x in zip(halves, xb):
        _project(x, tab_ref, w_ref, wuq_ref, wukv_ref, gq_ref, gkv_ref, p_ref, rows)


def _outproj(ya, yb, yc, ym, x2d, w, g, bias):
    n = x2d.shape[0]
    ytile = pl.BlockSpec((TM_OUT, GROUP), lambda i: (i, 0))
    const = lambda shape: pl.BlockSpec(shape, lambda i: (0,) * len(shape))
    return pl.pallas_call(
        _outproj_kernel,
        grid=(n // TM_OUT,),
        in_specs=[ytile, ytile, ytile, ytile,
                  pl.BlockSpec((TM_OUT, D_MODEL), lambda i: (i, 0)),
                  const((D_MODEL, D_MODEL)), const((1, D_MODEL)), const((1, D_MODEL))],
        out_specs=pl.BlockSpec((TM_OUT, D_MODEL), lambda i: (i, 0)),
        out_shape=jax.ShapeDtypeStruct((n, D_MODEL), F32),
        compiler_params=pltpu.CompilerParams(
            dimension_semantics=("parallel",), vmem_limit_bytes=VMEM_LIMIT),
        name="outproj",
    )(ya, yb, yc, ym, x2d, w, g, bias)


def _out_in_proj(ya, yb, yc, ym, x2d, wo, g, bias, tab, w, wuq, wukv, gq, gkv):
    n = x2d.shape[0]
    ytile = pl.BlockSpec((TM, GROUP), lambda i: (i, 0))
    const = lambda shape: pl.BlockSpec(shape, lambda i: (0,) * len(shape))
    return pl.pallas_call(
        _out_in_proj_kernel,
        grid=(n // TM,),
        in_specs=[ytile, ytile, ytile, ytile,
                  pl.BlockSpec((TM, D_MODEL), lambda i: (i, 0)),
                  const((D_MODEL, D_MODEL)), const((1, D_MODEL)), const((1, D_MODEL)),
                  pl.BlockSpec((TM, 512), lambda i: (i, 0)),
                  const((D_MODEL, W_WIDTH)), const((256, 512)), const((128, 512)),
                  const((1, 256)), const((1, 128))],
        out_specs=[pl.BlockSpec((TM, D_MODEL), lambda i: (i, 0)),
                   pl.BlockSpec((TM, P_WIDTH), lambda i: (i, 0))],
        out_shape=[jax.ShapeDtypeStruct((n, D_MODEL), F32),
                   jax.ShapeDtypeStruct((n, P_WIDTH), BF16)],
        compiler_params=pltpu.CompilerParams(
            dimension_semantics=("parallel",), vmem_limit_bytes=VMEM_LIMIT),
        name="out_in_proj",
    )(ya, yb, yc, ym, x2d, wo, g, bias, tab, w, wuq, wukv, gq, gkv)


def kernel(x, mem, positions, w_in, rel_bias, mla_q_norm, w_uq, mla_kv_norm, w_ukv,
           swa_sinks, w_mem_kv, w_out, ln_gain, ln_bias):
    b, s, d = x.shape
    depth = w_in.shape[0]
    assert d == D_MODEL and depth == DEPTH and s % B_TK == 0 and s >= A_WIN and (b * s) % TM_OUT == 0 and TM_OUT % TM == 0

    cols, cperm = _inproj_cols()
    w_in_p = _take_cols(w_in, cols).astype(BF16)
    wuq_p = jnp.pad(_take_cols(w_uq, _uq_cols()), ((0, 0), (0, 256 - MLA_Q_RANK), (0, 0))).astype(BF16)
    wukv_p = _take_cols(w_ukv, _ukv_cols()).astype(BF16)
    gq = jnp.pad(mla_q_norm, ((0, 0), (0, 256 - MLA_Q_RANK)))[:, None, :]
    gkv = mla_kv_norm[:, None, :]
    rows = np.concatenate([np.arange(512), 512 + cperm, np.arange(768, 1024)])
    w_out_p = _take_cols(w_out, rows, axis=1).astype(BF16)
    w_mem_all = jnp.transpose(w_mem_kv, (1, 0, 2)).reshape(D_MODEL, depth * 512).astype(BF16)
    e_a = _bias_table_a(rel_bias)
    e_c = jnp.asarray(_mask_table_c())
    pieces, expand3 = _rope_pieces(positions)

    memkv = _memkv(mem, w_mem_all)
    h = x.reshape(b * s, d)
    p2, tab = _inproj(h, pieces, expand3, w_in_p[0], wuq_p[0], wukv_p[0], gq[0], gkv[0])
    for l in range(depth):
        p3 = p2.reshape(b, s, P_WIDTH)
        ya = _window_attn(p3, P_AQ, p3, P_AK, p3, P_AV, P_GA, win=A_WIN, prev=A_PREV * CHUNK, dk=256, table=e_a[l])
        yb = _attn_b(p3)
        yc = _window_attn(p3, P_CQ, p3, P_CK, p3, P_CV, P_GC, win=C_WIN, prev=SWA_PREV * CHUNK, dk=128,
                          table=e_c, sinks=swa_sinks[l])
        ym = _window_attn(p3, P_MQ, memkv, 512 * l, memkv, 512 * l + 256, P_GM, win=MEM_LEN, prev=None, dk=256)
        ys = [y.reshape(b * s, GROUP) for y in (ya, yb, yc, ym)]
        ln = (ln_gain[l][None, :], ln_bias[l][None, :])
        if l + 1 < depth:
            h, p2 = _out_in_proj(*ys, h, w_out_p[l], *ln, tab, w_in_p[l + 1], wuq_p[l + 1], wukv_p[l + 1],
                                 gq[l + 1], gkv[l + 1])
        else:
            h = _outproj(*ys, h, w_out_p[l], *ln)
    return h.reshape(b, s, d)
```
